```python
import math
import jax, jax.numpy as jnp
from jax import lax
import numpy as np

D_MODEL = 2048
BATCH = 16
SEQ = 256
DEPTH = 1
DEC_BATCH = 8
DEC_SEQ = 1024
PAST_LEN = 256

GRID_W = 64
EPS = 1e-6
S5_WIDTH = 1024
S5_GROUP = 16
S5_GROUPS = S5_WIDTH // S5_GROUP
S5_STATE = 64
RET_HEADS = 8
RET_DK = 128
RET_DV = 256
RET_QK = RET_HEADS * RET_DK
RET_V = RET_HEADS * RET_DV
RET_CHUNK = 128
ROPE_BASE = 10000.0
N_EXPERTS = 16
EXPERT_FF = 4096
CAPACITY_FACTOR = 2
IN_COLS = S5_WIDTH + 2 * RET_QK + 2 * RET_V + 2 * D_MODEL
SPLITS = (S5_WIDTH, S5_WIDTH + RET_QK, S5_WIDTH + 2 * RET_QK, S5_WIDTH + 2 * RET_QK + RET_V, S5_WIDTH + 2 * RET_QK + 2 * RET_V, S5_WIDTH + 2 * RET_QK + 2 * RET_V + D_MODEL)

kernel_name = 'hybrid_s5_retention_ecmoe_diffusion_step'

F32 = jnp.float32


def _rmsnorm(x, w):
    xf = x.astype(F32)
    y = xf * lax.rsqrt(jnp.mean(xf * xf, axis=-1, keepdims=True) + EPS)
    return (y * w.astype(F32)).astype(x.dtype)


def _adaln(cvec, w_ada, b_ada):
    m = jax.nn.silu(cvec) @ w_ada + b_ada
    return jnp.split(m[..., None, :], 6, axis=-1)


def _rope(x, pos):
    half = x.shape[-1] // 2
    freqs = ROPE_BASE ** (-jnp.arange(half, dtype=F32) / half)
    ang = pos.astype(F32)[:, None] * freqs[None, :]
    cos = jnp.cos(ang)[None, :, None, :]
    sin = jnp.sin(ang)[None, :, None, :]
    x1, x2 = x[..., :half], x[..., half:]
    return jnp.concatenate([x1 * cos - x2 * sin, x1 * sin + x2 * cos], axis=-1)


def _grid_rope(x):
    t = x.shape[1]
    rows = t // GRID_W
    row = jnp.repeat(jnp.arange(rows), GRID_W)
    col = jnp.tile(jnp.arange(GRID_W), rows)
    h = x.shape[-1] // 2
    return jnp.concatenate([_rope(x[..., :h], row), _rope(x[..., h:], col)], axis=-1)


def _affine_combine(e1, e2):
    a1, b1 = e1
    a2, b2 = e2
    return (a2 * a1, a2 * b1 + b2)


def _s5_direction(u, a_dt, b_bar, c_mat, h0, reverse):
    t = u.shape[1]
    bu = jnp.einsum('gph,btgh->btgp', b_bar, u)
    a_seq = jnp.broadcast_to(jnp.exp(a_dt), bu.shape)
    _, h = lax.associative_scan(_affine_combine, (a_seq, bu), reverse=reverse, axis=1)
    if h0 is not None:
        steps = jnp.arange(1, t + 1, dtype=F32)
        if reverse:
            steps = steps[::-1]
        h = h + jnp.exp(a_dt[None] * steps[:, None, None])[None] * h0[:, None]
    y = jnp.einsum('ghp,btgp->btgh', c_mat, h).real
    h_last = h[:, 0] if reverse else h[:, -1]
    return y, h_last


def _s5_branch(u, a_re, a_im, log_dt, b_re, b_im, c_re, c_im, d_skip, w_glu, h0):
    bsz, t, _ = u.shape
    uf = u.astype(F32).reshape(bsz, t, S5_GROUPS, S5_GROUP)
    a = lax.complex(a_re.astype(F32), a_im.astype(F32))
    a_dt = a * jnp.exp(log_dt.astype(F32))[..., None]
    b_bar = ((jnp.exp(a_dt) - 1.0) / a)[..., None] * lax.complex(b_re.astype(F32), b_im.astype(F32))
    c_mat = lax.complex(c_re.astype(F32), c_im.astype(F32))
    uc = uf.astype(jnp.complex64)
    h0_f = None if h0 is None else h0[:, 0]
    h0_b = None if h0 is None else h0[:, 1]
    y_f, h_f = _s5_direction(uc, a_dt[0], b_bar[0], c_mat[0], h0_f, False)
    y_b, h_b = _s5_direction(uc, a_dt[1], b_bar[1], c_mat[1], h0_b, True)
    y = (y_f + y_b + uf * d_skip.astype(F32).reshape(S5_GROUPS, S5_GROUP)).reshape(bsz, t, S5_WIDTH)
    z = jax.nn.gelu(y).astype(u.dtype) @ w_glu
    za, zb = jnp.split(z, 2, axis=-1)
    return za * jax.nn.sigmoid(zb), jnp.stack([h_f, h_b], axis=1)


def _retention_chunkwise(q, k, v, log_gamma, s0, inclusive):
    bsz, t, nh, _ = q.shape
    dv = v.shape[-1]
    n = t // RET_CHUNK

    def to_chunks(z):
        return jnp.moveaxis(z.reshape(bsz, n, RET_CHUNK, nh, z.shape[-1]), 1, 0)

    idx = jnp.arange(RET_CHUNK, dtype=F32)
    diff = idx[:, None] - idx[None, :]
    mask = (diff >= 0) if inclusive else (diff > 0)
    intra = jnp.where(mask[None], jnp.exp(log_gamma[:, None, None] * jnp.where(mask, diff, 0.0)[None]), 0.0)
    q_dec = jnp.exp(log_gamma[None, :] * (idx + 1.0)[:, None])[None, :, :, None]
    k_dec = jnp.exp(log_gamma[None, :] * (RET_CHUNK - 1.0 - idx)[:, None])[None, :, :, None]
    chunk_dec = jnp.exp(log_gamma * RET_CHUNK)[None, :, None, None]

    def step(s, qkv):
        qc, kc, vc = qkv
        scores = jnp.einsum('bihd,bjhd->bhij', qc, kc) * intra[None]
        inner = jnp.einsum('bhij,bjhe->bihe', scores, vc)
        cross = jnp.einsum('bihd,bhde->bihe', qc, s) * q_dec
        s_new = s * chunk_dec + jnp.einsum('bjhd,bjhe->bhde', kc * k_dec, vc)
        return s_new, inner + cross

    s_fin, out = lax.scan(step, s0, (to_chunks(q), to_chunks(k), to_chunks(v)))
    return jnp.moveaxis(out, 0, 1).reshape(bsz, t, nh, dv), s_fin


def _retention_branch(q, k, v, g, decay_logit, gn_w, w_ret_out, s0, latent):
    bsz, t, _ = q.shape
    qf = q.astype(F32).reshape(bsz, t, RET_HEADS, RET_DK) * (RET_DK ** -0.5)
    kf = k.astype(F32).reshape(bsz, t, RET_HEADS, RET_DK)
    vf = v.astype(F32).reshape(bsz, t, RET_HEADS, RET_DV)
    if latent:
        qf = _grid_rope(qf)
        kf = _grid_rope(kf)
    log_gamma = jax.nn.log_sigmoid(decay_logit.astype(F32))
    if s0 is None:
        s0 = jnp.zeros((bsz, 2, RET_HEADS, RET_DK, RET_DV), F32)
    o_f, s_f = _retention_chunkwise(qf, kf, vf, log_gamma[0], s0[:, 0], True)
    o_b, s_b = _retention_chunkwise(jnp.flip(qf, 1), jnp.flip(kf, 1), jnp.flip(vf, 1), log_gamma[1], s0[:, 1], False)
    o = o_f + jnp.flip(o_b, 1)
    mu = jnp.mean(o, axis=-1, keepdims=True)
    var = jnp.mean(jnp.square(o - mu), axis=-1, keepdims=True)
    o = ((o - mu) * lax.rsqrt(var + EPS)).reshape(bsz, t, RET_V) * gn_w.astype(F32)
    o = (jax.nn.silu(g.astype(F32)) * o).astype(q.dtype)
    return o @ w_ret_out, jnp.stack([s_f, s_b], axis=1)


def _expert_choice_ffn(h, w_router, w_gate, w_up, w_down):
    bsz, t, d = h.shape
    cap = CAPACITY_FACTOR * t // N_EXPERTS
    logits = jnp.einsum('btd,de->bte', h.astype(F32), w_router.astype(F32))
    affinity = jax.nn.softmax(logits, axis=-1)
    gate, idx = lax.top_k(jnp.swapaxes(affinity, 1, 2), cap)
    xs = jax.vmap(lambda hb, ib: hb[ib])(h, idx)
    hid = jax.nn.silu(jnp.einsum('becd,edf->becf', xs, w_gate)) * jnp.einsum('becd,edf->becf', xs, w_up)
    out = jnp.einsum('becf,efd->becd', hid, w_down) * gate[..., None].astype(h.dtype)
    return jax.vmap(lambda ob, ib: jnp.zeros((t, d), h.dtype).at[ib.reshape(-1)].add(ob.reshape(-1, d)))(out, idx)


def _layer(x, mod, latent, s5_h0, ret_s0, p, l):
    shift1, scale1, gate1, shift2, scale2, gate2 = mod
    h = (_rmsnorm(x, p['norm1'][l]) * (1.0 + scale1) + shift1).astype(x.dtype)
    u, q, k, v, g, ga, gb = jnp.split(h @ p['w_in'][l], SPLITS, axis=-1)
    out_a, s5_fin = _s5_branch(u, p['s5_a_re'][l], p['s5_a_im'][l], p['s5_log_dt'][l], p['s5_b_re'][l], p['s5_b_im'][l], p['s5_c_re'][l], p['s5_c_im'][l], p['s5_d'][l], p['w_s5_glu'][l], s5_h0)
    out_b, ret_fin = _retention_branch(q, k, v, g, p['ret_decay_logit'][l], p['ret_gn_w'][l], p['w_ret_out'][l], ret_s0, latent)
    merged = jax.nn.sigmoid(ga) * out_a + jax.nn.sigmoid(gb) * out_b
    x = (x + gate1 * (merged @ p['w_out'][l])).astype(x.dtype)
    h2 = (_rmsnorm(x, p['norm2'][l]) * (1.0 + scale2) + shift2).astype(x.dtype)
    x = (x + gate2 * _expert_choice_ffn(h2, p['w_router'][l], p['w_exp_gate'][l], p['w_exp_up'][l], p['w_exp_down'][l])).astype(x.dtype)
    return x, s5_fin, ret_fin


def setup_inputs(seed: int = 0) -> dict:
    key = jax.random.key(seed)
    ks = jax.random.split(key, 32)

    def nrm(k, shape, s):
        return jax.random.normal(k, shape, F32) * s

    G, P, HG = S5_GROUPS, S5_STATE, S5_GROUP
    n_idx = jnp.arange(P, dtype=F32)
    ret_logit0 = jnp.log(2.0 ** (5.0 + jnp.arange(RET_HEADS, dtype=F32)) - 1.0)
    return {
        'x_prompt': nrm(ks[0], (BATCH, SEQ, D_MODEL), 1.0),
        'x_sample': nrm(ks[1], (DEC_BATCH, DEC_SEQ, D_MODEL), 1.0),
        'state_s5_re': nrm(ks[2], (DEC_BATCH, DEPTH, 2, G, P), 0.1),
        'state_s5_im': nrm(ks[3], (DEC_BATCH, DEPTH, 2, G, P), 0.1),
        'state_ret': nrm(ks[4], (DEC_BATCH, DEPTH, 2, RET_HEADS, RET_DK, RET_DV), 1.0),
        'c': nrm(ks[5], (DEC_BATCH, D_MODEL), 1.0),
        'c_ctx': nrm(ks[6], (D_MODEL,), 1.0),
        'final_norm': 1.0 + nrm(ks[7], (D_MODEL,), 0.01),
        'w_ada': nrm(ks[8], (DEPTH, D_MODEL, 6 * D_MODEL), 0.5 * D_MODEL ** -0.5),
        'b_ada': nrm(ks[9], (DEPTH, 6 * D_MODEL), 0.01),
        'norm1': 1.0 + nrm(ks[10], (DEPTH, D_MODEL), 0.01),
        'norm2': 1.0 + nrm(ks[11], (DEPTH, D_MODEL), 0.01),
        'w_in': nrm(ks[12], (DEPTH, D_MODEL, IN_COLS), D_MODEL ** -0.5),
        's5_a_re': -0.5 + nrm(ks[13], (DEPTH, 2, G, P), 0.01),
        's5_a_im': math.pi * n_idx + nrm(ks[14], (DEPTH, 2, G, P), 0.01),
        's5_log_dt': jax.random.uniform(ks[15], (DEPTH, 2, G), F32, math.log(1e-3), math.log(1e-1)),
        's5_b_re': nrm(ks[16], (DEPTH, 2, G, P, HG), (2.0 * HG) ** -0.5),
        's5_b_im': nrm(ks[17], (DEPTH, 2, G, P, HG), (2.0 * HG) ** -0.5),
        's5_c_re': nrm(ks[18], (DEPTH, 2, G, HG, P), (2.0 * P) ** -0.5),
        's5_c_im': nrm(ks[19], (DEPTH, 2, G, HG, P), (2.0 * P) ** -0.5),
        's5_d': nrm(ks[20], (DEPTH, S5_WIDTH), 1.0),
        'w_s5_glu': nrm(ks[21], (DEPTH, S5_WIDTH, 2 * D_MODEL), S5_WIDTH ** -0.5),
        'ret_decay_logit': ret_logit0 + nrm(ks[22], (DEPTH, 2, RET_HEADS), 0.01),
        'ret_gn_w': 1.0 + nrm(ks[23], (DEPTH, RET_V), 0.01),
        'w_ret_out': nrm(ks[24], (DEPTH, RET_V, D_MODEL), RET_V ** -0.5),
        'w_out': nrm(ks[25], (DEPTH, D_MODEL, D_MODEL), D_MODEL ** -0.5),
        'w_router': nrm(ks[26], (DEPTH, D_MODEL, N_EXPERTS), D_MODEL ** -0.5),
        'w_exp_gate': nrm(ks[27], (DEPTH, N_EXPERTS, D_MODEL, EXPERT_FF), D_MODEL ** -0.5),
        'w_exp_up': nrm(ks[28], (DEPTH, N_EXPERTS, D_MODEL, EXPERT_FF), D_MODEL ** -0.5),
        'w_exp_down': nrm(ks[29], (DEPTH, N_EXPERTS, EXPERT_FF, D_MODEL), EXPERT_FF ** -0.5),
    }


def reference(x_prompt, x_sample, state_s5_re, state_s5_im, state_ret, c, c_ctx, final_norm, w_ada, b_ada, norm1, norm2, w_in, s5_a_re, s5_a_im, s5_log_dt, s5_b_re, s5_b_im, s5_c_re, s5_c_im, s5_d, w_s5_glu, ret_decay_logit, ret_gn_w, w_ret_out, w_out, w_router, w_exp_gate, w_exp_up, w_exp_down):
    p = {'norm1': norm1, 'norm2': norm2, 'w_in': w_in, 's5_a_re': s5_a_re, 's5_a_im': s5_a_im, 's5_log_dt': s5_log_dt, 's5_b_re': s5_b_re, 's5_b_im': s5_b_im, 's5_c_re': s5_c_re, 's5_c_im': s5_c_im, 's5_d': s5_d, 'w_s5_glu': w_s5_glu, 'ret_decay_logit': ret_decay_logit, 'ret_gn_w': ret_gn_w, 'w_ret_out': w_ret_out, 'w_out': w_out, 'w_router': w_router, 'w_exp_gate': w_exp_gate, 'w_exp_up': w_exp_up, 'w_exp_down': w_exp_down}
    xp = x_prompt
    xs = x_sample
    new_re, new_im, new_ret = [], [], []
    for l in range(DEPTH):
        mod_ctx = _adaln(c_ctx, w_ada[l], b_ada[l])
        mod_lat = _adaln(c, w_ada[l], b_ada[l])
        xp, s5_fin, ret_fin = _layer(xp, mod_ctx, False, None, None, p, l)
        new_re.append(s5_fin.real)
        new_im.append(s5_fin.imag)
        new_ret.append(ret_fin)
        h0 = lax.complex(state_s5_re[:, l].astype(F32), state_s5_im[:, l].astype(F32))
        xs, _, _ = _layer(xs, mod_lat, True, h0, state_ret[:, l].astype(F32), p, l)
    y_prompt = _rmsnorm(xp, final_norm)
    y_sample = _rmsnorm(xs, final_norm)
    return (y_prompt, y_sample, jnp.stack(new_re, axis=1), jnp.stack(new_im, axis=1), jnp.stack(new_ret, axis=1))
```

```python
import functools
import math

import jax
import jax.numpy as jnp
from jax import lax
from jax.experimental import pallas as pl
from jax.experimental.pallas import tpu as pltpu

F32 = jnp.float32
BF16 = jnp.bfloat16
HIGHEST = lax.Precision.HIGHEST

EPS = 1e-6
GRID_W = 64
S5_WIDTH = 1024
S5_GROUP = 16
S5_STATE = 64
RET_HEADS = 8
RET_DK = 128
RET_DV = 256
ROPE_BASE = 10000.0
N_EXPERTS = 16
CAPACITY_FACTOR = 2

LANES = 128
S5_CHUNK = 8
S5_GB = LANES // S5_GROUP
S5_NGB = S5_WIDTH // LANES
S5_SCOLS = 4 * S5_GB * S5_STATE
ROW_TILE = 256
VMEM_LIMIT = 56 * 1024 * 1024


def _params(*sem):
    return pltpu.CompilerParams(dimension_semantics=sem, vmem_limit_bytes=VMEM_LIMIT)


def _sigmoid(x):
    return 1.0 / (1.0 + jnp.exp(-x))


def _gelu_tanh(x):
    return 0.5 * x * (1.0 + jnp.tanh(math.sqrt(2.0 / math.pi) * (x + 0.044715 * (x * x * x))))


def _dot(a, b):
    return jnp.dot(a, b, preferred_element_type=F32)


def _ada_kernel(c_ref, w_ref, b_ref, o_ref):
    c = c_ref[...]
    s = c * _sigmoid(c)
    o_ref[...] = jnp.dot(s, w_ref[...], precision=HIGHEST, preferred_element_type=F32) + b_ref[...]


def _ada_call(cvec, w_ada, b_ada):
    r, d = cvec.shape
    n = w_ada.shape[1]
    tn = 1024 if n % 1024 == 0 else 512
    return pl.pallas_call(
        _ada_kernel,
        grid=(n // tn,),
        in_specs=[pl.BlockSpec((r, d), lambda j: (0, 0)),
                  pl.BlockSpec((d, tn), lambda j: (0, j)),
                  pl.BlockSpec((1, tn), lambda j: (0, j))],
        out_specs=pl.BlockSpec((r, tn), lambda j: (0, j)),
        out_shape=jax.ShapeDtypeStruct((r, n), F32),
        compiler_params=_params("arbitrary"),
        name="ada",
    )(cvec, w_ada, b_ada.reshape(1, n))


def _norm_mod_kernel(xp_ref, xs_ref, sc_ref, sh_ref, w_ref, o_ref, *, n_prompt_tiles):
    i = pl.program_id(0)

    def body(x_ref):
        x = x_ref[...]
        y = x * lax.rsqrt(jnp.mean(x * x, axis=-1, keepdims=True) + EPS)
        h = (y * w_ref[...]) * (1.0 + sc_ref[...]) + sh_ref[...]
        o_ref[...] = h.astype(o_ref.dtype)

    @pl.when(i < n_prompt_tiles)
    def _():
        body(xp_ref)

    @pl.when(i >= n_prompt_tiles)
    def _():
        body(xs_ref)


def _mod_row(i, n_prompt_tiles, tiles_per_sample_seq):
    return jnp.where(i < n_prompt_tiles, 0, 1 + (i - n_prompt_tiles) // tiles_per_sample_seq)


def _norm_mod_call(xp, xs, scale, shift, w, t_sample):
    n_p, d = xp.shape
    n_s = xs.shape[0]
    tm = ROW_TILE
    npt, nst = n_p // tm, n_s // tm
    tps = t_sample // tm
    row = functools.partial(_mod_row, n_prompt_tiles=npt, tiles_per_sample_seq=tps)
    return pl.pallas_call(
        functools.partial(_norm_mod_kernel, n_prompt_tiles=npt),
        grid=(npt + nst,),
        in_specs=[pl.BlockSpec((tm, d), lambda i: (jnp.minimum(i, npt - 1), 0)),
                  pl.BlockSpec((tm, d), lambda i: (jnp.maximum(i - npt, 0), 0)),
                  pl.BlockSpec((None, 1, d), lambda i: (row(i), 0, 0)),
                  pl.BlockSpec((None, 1, d), lambda i: (row(i), 0, 0)),
                  pl.BlockSpec((1, d), lambda i: (0, 0))],
        out_specs=pl.BlockSpec((tm, d), lambda i: (i, 0)),
        out_shape=jax.ShapeDtypeStruct((n_p + n_s, d), BF16),
        compiler_params=_params("arbitrary"),
        name="norm_mod",
    )(xp, xs, scale, shift, w)


def _proj_kernel(a_ref, w_ref, o_ref, wb_ref):
    @pl.when(pl.program_id(1) == 0)
    def _():
        wb_ref[...] = w_ref[...].astype(BF16)

    o_ref[...] = _dot(a_ref[...], wb_ref[...]).astype(o_ref.dtype)


def _proj_call(a, w):
    m, k = a.shape
    n = w.shape[1]
    tm = 1024 if m % 1024 == 0 else m
    tn = 1024 if n % 1024 == 0 else 512
    return pl.pallas_call(
        _proj_kernel,
        grid=(n // tn, m // tm),
        in_specs=[pl.BlockSpec((tm, k), lambda j, i: (i, 0)),
                  pl.BlockSpec((k, tn), lambda j, i: (0, j))],
        out_specs=pl.BlockSpec((tm, tn), lambda j, i: (i, j)),
        out_shape=jax.ShapeDtypeStruct((m, n), BF16),
        scratch_shapes=[pltpu.VMEM((k, tn), BF16)],
        compiler_params=_params("arbitrary", "arbitrary"),
        name="in_proj",
    )(a, w)


def _cmul(ar, ai, br, bi):
    return ar * br - ai * bi, ar * bi + ai * br


def _s5_chunk_weights(a_re, a_im, log_dt, b_re, b_im, c_re, c_im, d_skip):
    L, G, P, HG, GB, NGB = S5_CHUNK, S5_WIDTH // S5_GROUP, S5_STATE, S5_GROUP, S5_GB, S5_NGB
    dt = jnp.exp(log_dt.astype(F32))[..., None]
    adt_re, adt_im = a_re.astype(F32) * dt, a_im.astype(F32) * dt
    mag = jnp.exp(adt_re)
    lam_re, lam_im = mag * jnp.cos(adt_im), mag * jnp.sin(adt_im)
    den = a_re * a_re + a_im * a_im
    q_re = ((lam_re - 1.0) * a_re + lam_im * a_im) / den
    q_im = (lam_im * a_re - (lam_re - 1.0) * a_im) / den
    bb_re, bb_im = _cmul(q_re[..., None], q_im[..., None], b_re.astype(F32), b_im.astype(F32))
    n = jnp.arange(L + 1, dtype=F32)
    pmag = jnp.exp(adt_re[..., None] * n)
    pw_re, pw_im = pmag * jnp.cos(adt_im[..., None] * n), pmag * jnp.sin(adt_im[..., None] * n)
    cr, ci = c_re.astype(F32), c_im.astype(F32)
    cl_re, cl_im = _cmul(cr[..., None], ci[..., None], pw_re[:, :, None], pw_im[:, :, None])
    kk = (jnp.einsum('dgopn,dgpi->dgnoi', cl_re, bb_re, precision=HIGHEST)
          - jnp.einsum('dgopn,dgpi->dgnoi', cl_im, bb_im, precision=HIGHEST))[:, :, :L]
    k0 = kk[0, :, 0] + kk[1, :, 0] + jax.vmap(jnp.diag)(d_skip.astype(F32).reshape(G, HG))
    kc = jnp.concatenate([kk[1][:, 1:][:, ::-1], k0[:, None], kk[0][:, 1:]], axis=1)
    idx = jnp.arange(L)[None, :] - jnp.arange(L)[:, None] + (L - 1)
    tpl = jnp.transpose(kc[:, idx], (0, 1, 4, 2, 3)).reshape(NGB, GB, L, HG, L, HG)
    m_cmp = jnp.transpose(tpl, (0, 2, 1, 3, 4, 5)).reshape(NGB, L * LANES, L * HG)

    pf_re, pf_im = pw_re[0][..., :L][..., ::-1], pw_im[0][..., :L][..., ::-1]
    pb_re, pb_im = pw_re[1][..., :L], pw_im[1][..., :L]
    sf_re, sf_im = _cmul(pf_re[..., None], pf_im[..., None], bb_re[0][:, :, None], bb_im[0][:, :, None])
    sb_re, sb_im = _cmul(pb_re[..., None], pb_im[..., None], bb_re[1][:, :, None], bb_im[1][:, :, None])
    ws = jnp.stack([jnp.stack([sf_re, sb_re]), jnp.stack([sf_im, sb_im])])
    ws = jnp.transpose(ws, (2, 4, 5, 0, 1, 3)).reshape(NGB, GB, L, HG, 2, 2, P)
    ws_cmp = jnp.transpose(ws, (0, 2, 1, 3, 4, 5, 6)).reshape(NGB, L * LANES, 4 * P)

    yf_re, yf_im = cl_re[0][..., 1:], cl_im[0][..., 1:]
    yb_re, yb_im = cl_re[1][..., 1:][..., ::-1], cl_im[1][..., 1:][..., ::-1]
    wy = jnp.stack([jnp.stack([yf_re, yb_re]), jnp.stack([-yf_im, -yb_im])])
    wy = jnp.transpose(wy, (2, 0, 1, 4, 5, 3)).reshape(NGB, GB, 2, 2, P, L, HG)
    wy_cmp = jnp.transpose(wy, (0, 2, 3, 1, 4, 5, 6)).reshape(NGB, S5_SCOLS, L * HG)

    def lanes(x):
        return jnp.transpose(x.reshape(2, NGB, GB * P), (1, 0, 2)).reshape(NGB, 1, 2 * GB * P)

    a_step = jnp.concatenate([lanes(pw_re[..., L]), lanes(pw_im[..., L])], axis=1)
    return m_cmp.astype(BF16), ws_cmp.astype(BF16), wy_cmp.astype(BF16), a_step


def _s5_expand_kernel(xm_ref, xws_ref, zwy_ref, m_ref, ws_ref, wy_ref):
    hg, p = S5_GROUP, S5_STATE
    lg_hg, lg_p = hg.bit_length() - 1, p.bit_length() - 1
    gmask = S5_GB - 1

    def expand(x, n_out, src_of_col, row_group, col_group, o_ref):
        rows, k = x.shape
        width = min(n_out, 1024)
        for c0 in range(0, n_out, width):
            kk = lax.broadcasted_iota(jnp.int32, (k, width), 0)
            cc = lax.broadcasted_iota(jnp.int32, (k, width), 1) + c0
            spread = jnp.where(kk == src_of_col(cc), 1.0, 0.0).astype(BF16)
            full = _dot(x, spread)
            rr = lax.broadcasted_iota(jnp.int32, (rows, width), 0)
            c2 = lax.broadcasted_iota(jnp.int32, (rows, width), 1) + c0
            o_ref[:, c0:c0 + width] = jnp.where(row_group(rr) == col_group(c2), full, 0.0).astype(o_ref.dtype)

    lane_group = lambda i: jnp.right_shift(i, lg_hg) & gmask
    state_group = lambda i: jnp.right_shift(i, lg_p) & gmask
    chunk_src = lambda c: jnp.right_shift(c, 7) * hg + (c & (hg - 1))
    state_src = lambda c: jnp.right_shift(c, lg_p + 3) * p + (c & (p - 1))
    expand(xm_ref[...], m_ref.shape[1], chunk_src, lane_group, lane_group, m_ref)
    expand(xws_ref[...], ws_ref.shape[1], state_src, lane_group, state_group, ws_ref)
    expand(zwy_ref[...], wy_ref.shape[1], chunk_src, state_group, lane_group, wy_ref)


def _s5_expand_call(m_cmp, ws_cmp, wy_cmp):
    ngb, k, _ = m_cmp.shape
    sc = S5_SCOLS
    blk = lambda a, b: pl.BlockSpec((None, a, b), lambda g: (g, 0, 0))
    return pl.pallas_call(
        _s5_expand_kernel,
        grid=(ngb,),
        in_specs=[blk(k, m_cmp.shape[2]), blk(k, ws_cmp.shape[2]), blk(sc, wy_cmp.shape[2])],
        out_specs=[blk(k, k), blk(k, sc), blk(sc, k)],
        out_shape=[jax.ShapeDtypeStruct((ngb, k, k), BF16), jax.ShapeDtypeStruct((ngb, k, sc), BF16),
                   jax.ShapeDtypeStruct((ngb, sc, k), BF16)],
        compiler_params=_params("arbitrary"),
        name="s5_expand",
    )(m_cmp, ws_cmp, wy_cmp)


def _s5_kernel(u_ref, m_ref, ws_ref, wy_ref, a_ref, h0_ref, y_ref, hfin_ref, s_scr, hb_scr, *, nseq, nchunk):
    half = S5_SCOLS // 4
    u = u_ref[...]
    for j in range(4):
        s_scr[:, j * half:(j + 1) * half] = _dot(u, ws_ref[:, j * half:(j + 1) * half])

    a = a_ref[...]
    af_re, ab_re = a[0:1, 0:half], a[0:1, half:2 * half]
    af_im, ab_im = a[1:2, 0:half], a[1:2, half:2 * half]
    h0 = h0_ref[...]
    carry0 = (h0[:, 0:half], h0[:, 2 * half:3 * half], h0[:, half:2 * half], h0[:, 3 * half:4 * half])

    def step(i, carry):
        hf_re, hf_im, hb_re, hb_im = carry
        rf = pl.multiple_of(i * nseq, nseq)
        rb = pl.multiple_of((nchunk - 1 - i) * nseq, nseq)
        sf_re = s_scr[pl.ds(rf, nseq), 0:half]
        sf_im = s_scr[pl.ds(rf, nseq), 2 * half:3 * half]
        sb_re = s_scr[pl.ds(rb, nseq), half:2 * half]
        sb_im = s_scr[pl.ds(rb, nseq), 3 * half:4 * half]
        s_scr[pl.ds(rf, nseq), 0:half] = hf_re
        s_scr[pl.ds(rf, nseq), 2 * half:3 * half] = hf_im
        s_scr[pl.ds(rb, nseq), half:2 * half] = hb_re
        s_scr[pl.ds(rb, nseq), 3 * half:4 * half] = hb_im
        nf_re = af_re * hf_re - af_im * hf_im + sf_re
        nf_im = af_re * hf_im + af_im * hf_re + sf_im
        nb_re = ab_re * hb_re - ab_im * hb_im + sb_re
        nb_im = ab_re * hb_im + ab_im * hb_re + sb_im
        return nf_re, nf_im, nb_re, nb_im

    hf_re, hf_im, hb_re, hb_im = lax.fori_loop(0, nchunk, step, carry0)
    hfin_ref[:, 0:half] = hf_re
    hfin_ref[:, half:2 * half] = hb_re
    hfin_ref[:, 2 * half:3 * half] = hf_im
    hfin_ref[:, 3 * half:4 * half] = hb_im

    hb_scr[...] = s_scr[...].astype(BF16)
    cw = 2 * LANES
    for j in range(u.shape[1] // cw):
        y = _dot(u, m_ref[:, j * cw:(j + 1) * cw]) + _dot(hb_scr[...], wy_ref[:, j * cw:(j + 1) * cw])
        y_ref[:, j * cw:(j + 1) * cw] = _gelu_tanh(y).astype(y_ref.dtype)


def _s5_call(u_t, m_mat, ws_mat, wy_mat, a_step, h0, nseq, nchunk):
    ngb, r, k = u_t.shape
    sc = S5_SCOLS
    return pl.pallas_call(
        functools.partial(_s5_kernel, nseq=nseq, nchunk=nchunk),
        grid=(ngb,),
        in_specs=[pl.BlockSpec((None, r, k), lambda g: (g, 0, 0)),
                  pl.BlockSpec((None, k, k), lambda g: (g, 0, 0)),
                  pl.BlockSpec((None, k, sc), lambda g: (g, 0, 0)),
                  pl.BlockSpec((None, sc, k), lambda g: (g, 0, 0)),
                  pl.BlockSpec((None, 2, sc // 2), lambda g: (g, 0, 0)),
                  pl.BlockSpec((None, nseq, sc), lambda g: (g, 0, 0))],
        out_specs=[pl.BlockSpec((None, r, k), lambda g: (g, 0, 0)),
                   pl.BlockSpec((None, nseq, sc), lambda g: (g, 0, 0))],
        out_shape=[jax.ShapeDtypeStruct((ngb, r, k), BF16),
                   jax.ShapeDtypeStruct((ngb, nseq, sc), F32)],
        scratch_shapes=[pltpu.VMEM((r, sc), F32), pltpu.VMEM((r, sc), BF16)],
        compiler_params=_params("arbitrary"),
        name="s5",
    )(u_t, m_mat, ws_mat, wy_mat, a_step, h0)


def _to_chunk_major(u, nseq, t):
    nchunk = t // S5_CHUNK
    x = u.reshape(nseq, nchunk, S5_CHUNK, S5_NGB, LANES)
    return jnp.transpose(x, (3, 1, 0, 2, 4)).reshape(S5_NGB, nchunk * nseq, S5_CHUNK * LANES)


def _from_chunk_major(y, nseq, t):
    nchunk = t // S5_CHUNK
    x = y.reshape(S5_NGB, nchunk, nseq, S5_CHUNK, LANES)
    return jnp.transpose(x, (2, 1, 3, 0, 4)).reshape(nseq * t, S5_WIDTH)


def _state_to_lanes(s_re, s_im):
    b = s_re.shape[0]

    def one(x):
        return jnp.transpose(x.reshape(b, 2, S5_NGB, S5_GB * S5_STATE), (2, 0, 1, 3)).reshape(S5_NGB, b, -1)

    return jnp.concatenate([one(s_re), one(s_im)], axis=-1)


def _lanes_to_state(h):
    b = h.shape[1]
    x = h.reshape(S5_NGB, b, 2, 2, S5_GB, S5_STATE)
    x = jnp.transpose(x, (2, 1, 3, 0, 4, 5)).reshape(2, b, 2, S5_NGB * S5_GB, S5_STATE)
    return x[0], x[1]


def _ret_kernel(lg_ref, q_ref, k_ref, v_ref, g_ref, gn_ref, *rest, t, latent, want_state):
    rest = list(rest)
    if latent:
        cos_ref, sin_ref, s0_ref = rest[:3]
        rest = rest[3:]
    o_ref = rest.pop(0)
    sfin_ref = rest.pop(0) if want_state else None
    dmat_ref = rest.pop(0)

    h = pl.program_id(0)
    lgf = lg_ref[0, h]
    lgb = lg_ref[1, h]

    @pl.when(pl.program_id(1) == 0)
    def _():
        ti = lax.broadcasted_iota(jnp.int32, (t, t), 0)
        si = lax.broadcasted_iota(jnp.int32, (t, t), 1)
        diff = (ti - si).astype(F32)
        dmat_ref[...] = jnp.exp(jnp.where(diff >= 0, lgf * diff, -lgb * diff))

    q = q_ref[...].astype(F32) * (RET_DK ** -0.5)
    k = k_ref[...].astype(F32)
    if latent:
        lane = lax.broadcasted_iota(jnp.int32, (t, RET_DK), 1)
        first = (lane % (RET_DK // 2)) < (RET_DK // 4)

        def rope(x):
            swapped = jnp.where(first, pltpu.roll(x, RET_DK - RET_DK // 4, 1), pltpu.roll(x, RET_DK // 4, 1))
            return x * cos_ref[...] + swapped * sin_ref[...]

        q = rope(q)
        k = rope(k)
    qb = q.astype(BF16)
    kb = k.astype(BF16)
    v = v_ref[...]

    scores = lax.dot_general(qb, kb, (((1,), (1,)), ((), ())), preferred_element_type=F32)
    o = _dot((scores * dmat_ref[...]).astype(BF16), v)

    pos = lax.broadcasted_iota(jnp.int32, (t, 1), 0).astype(F32)
    if latent:
        o = o + _dot(qb, s0_ref[0].astype(BF16)) * jnp.exp(lgf * (pos + 1.0))
        o = o + _dot(qb, s0_ref[1].astype(BF16)) * jnp.exp(lgb * (t - pos))
    if want_state:
        kf = (k * jnp.exp(lgf * (t - 1.0 - pos))).astype(BF16)
        kr = (k * jnp.exp(lgb * pos)).astype(BF16)
        tn = (((0,), (0,)), ((), ()))
        sfin_ref[0] = lax.dot_general(kf, v, tn, preferred_element_type=F32)
        sfin_ref[1] = lax.dot_general(kr, v, tn, preferred_element_type=F32)

    mu = jnp.mean(o, axis=-1, keepdims=True)
    d = o - mu
    var = jnp.mean(d * d, axis=-1, keepdims=True)
    on = d * lax.rsqrt(var + EPS) * gn_ref[...]
    g = g_ref[...].astype(F32)
    o_ref[...] = (g * _sigmoid(g) * on).astype(o_ref.dtype)


def _ret_call(proj, log_gamma, gn_w, rope_tabs, s0, *, row0, nseq, t, latent, want_state):
    h, dk, dv = RET_HEADS, RET_DK, RET_DV
    rb0 = row0 // t
    q0 = S5_WIDTH // dk
    k0 = q0 + h
    v0 = (S5_WIDTH + 2 * h * dk) // dv
    g0 = v0 + h
    in_specs = [pl.BlockSpec(memory_space=pltpu.SMEM),
                pl.BlockSpec((t, dk), lambda hh, b: (rb0 + b, q0 + hh)),
                pl.BlockSpec((t, dk), lambda hh, b: (rb0 + b, k0 + hh)),
                pl.BlockSpec((t, dv), lambda hh, b: (rb0 + b, v0 + hh)),
                pl.BlockSpec((t, dv), lambda hh, b: (rb0 + b, g0 + hh)),
                pl.BlockSpec((1, dv), lambda hh, b: (0, hh))]
    args = [log_gamma, proj, proj, proj, proj, gn_w]
    if latent:
        in_specs += [pl.BlockSpec((t, dk), lambda hh, b: (0, 0)),
                     pl.BlockSpec((t, dk), lambda hh, b: (0, 0)),
                     pl.BlockSpec((None, 2, None, dk, dv), lambda hh, b: (b, 0, hh, 0, 0))]
        args += [rope_tabs[0], rope_tabs[1], s0]
    out_specs = [pl.BlockSpec((t, dv), lambda hh, b: (b, hh))]
    out_shape = [jax.ShapeDtypeStruct((nseq * t, h * dv), BF16)]
    if want_state:
        out_specs.append(pl.BlockSpec((None, 2, None, dk, dv), lambda hh, b: (b, 0, hh, 0, 0)))
        out_shape.append(jax.ShapeDtypeStruct((nseq, 2, h, dk, dv), F32))
    return pl.pallas_call(
        functools.partial(_ret_kernel, t=t, latent=latent, want_state=want_state),
        grid=(h, nseq),
        in_specs=in_specs,
        out_specs=out_specs,
        out_shape=out_shape,
        scratch_shapes=[pltpu.VMEM((t, t), F32)],
        compiler_params=_params("arbitrary", "arbitrary"),
        name="retention_latent" if latent else "retention_ctx",
    )(*args)


def _grid_rope_tables(t):
    quarter = RET_DK // 4
    freqs = ROPE_BASE ** (-jnp.arange(quarter, dtype=F32) / quarter)
    pos = jnp.arange(t)
    row = (pos // GRID_W).astype(F32)[:, None] * freqs[None, :]
    col = (pos % GRID_W).astype(F32)[:, None] * freqs[None, :]
    cos = jnp.concatenate([jnp.cos(row), jnp.cos(row), jnp.cos(col), jnp.cos(col)], axis=-1)
    sin = jnp.concatenate([-jnp.sin(row), jnp.sin(row), -jnp.sin(col), jnp.sin(col)], axis=-1)
    return cos, sin


def _merge_kernel(yg_ref, op_ref, os_ref, ga_ref, gb_ref, wga_ref, wgb_ref, wr_ref, out_ref, wga_b, wgb_b, wr_b,
                  *, n_prompt_tiles):
    i = pl.program_id(1)

    @pl.when(i == 0)
    def _():
        wga_b[...] = wga_ref[...].astype(BF16)
        wgb_b[...] = wgb_ref[...].astype(BF16)
        wr_b[...] = wr_ref[...].astype(BF16)

    def body(o_ref):
        yg = yg_ref[...]
        za = _dot(yg, wga_b[...])
        zb = _dot(yg, wgb_b[...])
        ob = _dot(o_ref[...], wr_b[...])
        out_a = za * _sigmoid(zb)
        ga = ga_ref[...].astype(F32)
        gb = gb_ref[...].astype(F32)
        out_ref[...] = (_sigmoid(ga) * out_a + _sigmoid(gb) * ob).astype(out_ref.dtype)

    @pl.when(i < n_prompt_tiles)
    def _():
        body(op_ref)

    @pl.when(i >= n_prompt_tiles)
    def _():
        body(os_ref)


def _merge_call(yg, o_p, o_s, proj, w_glu, w_ret_out):
    m, ks = yg.shape
    kr = o_p.shape[1]
    d = w_ret_out.shape[1]
    tm = 1024 if m % 1024 == 0 else m
    tn = 512 if d % 512 == 0 else d
    ga0 = (proj.shape[1] - 2 * d) // tn
    gb0 = (proj.shape[1] - d) // tn
    nb = d // tn
    npt = o_p.shape[0] // tm
    return pl.pallas_call(
        functools.partial(_merge_kernel, n_prompt_tiles=npt),
        grid=(nb, m // tm),
        in_specs=[pl.BlockSpec((tm, ks), lambda j, i: (i, 0)),
                  pl.BlockSpec((tm, kr), lambda j, i: (jnp.minimum(i, npt - 1), 0)),
                  pl.BlockSpec((tm, kr), lambda j, i: (jnp.maximum(i - npt, 0), 0)),
                  pl.BlockSpec((tm, tn), lambda j, i: (i, ga0 + j)),
                  pl.BlockSpec((tm, tn), lambda j, i: (i, gb0 + j)),
                  pl.BlockSpec((ks, tn), lambda j, i: (0, j)),
                  pl.BlockSpec((ks, tn), lambda j, i: (0, nb + j)),
                  pl.BlockSpec((kr, tn), lambda j, i: (0, j))],
        out_specs=pl.BlockSpec((tm, tn), lambda j, i: (i, j)),
        out_shape=jax.ShapeDtypeStruct((m, d), BF16),
        scratch_shapes=[pltpu.VMEM((ks, tn), BF16), pltpu.VMEM((ks, tn), BF16), pltpu.VMEM((kr, tn), BF16)],
        compiler_params=_params("arbitrary", "arbitrary"),
        name="merge",
    )(yg, o_p, o_s, proj, proj, w_glu, w_glu, w_ret_out)


def _outproj_kernel(mg_ref, xp_ref, xs_ref, w_ref, g1_ref, sc_ref, sh_ref, n2_ref, wr_ref,
                    x1_ref, h2_ref, aff_ref, *, n_prompt_tiles, n_experts):
    i = pl.program_id(0)
    upd = g1_ref[...] * _dot(mg_ref[...], w_ref[...])

    def body(x_ref):
        x1 = x_ref[...] + upd
        x1_ref[...] = x1
        y = x1 * lax.rsqrt(jnp.mean(x1 * x1, axis=-1, keepdims=True) + EPS)
        h2 = (y * n2_ref[...]) * (1.0 + sc_ref[...]) + sh_ref[...]
        h2_ref[...] = h2.astype(h2_ref.dtype)
        logits = jnp.dot(h2, wr_ref[...], precision=HIGHEST, preferred_element_type=F32)
        lane = lax.broadcasted_iota(jnp.int32, logits.shape, 1)
        logits = jnp.where(lane < n_experts, logits, -jnp.inf)
        e = jnp.exp(logits - jnp.max(logits, axis=-1, keepdims=True))
        aff_ref[...] = e / jnp.sum(e, axis=-1, keepdims=True)

    @pl.when(i < n_prompt_tiles)
    def _():
        body(xp_ref)

    @pl.when(i >= n_prompt_tiles)
    def _():
        body(xs_ref)


def _outproj_call(merged, xp, xs, w_out_b, gate1, scale2, shift2, norm2, w_router_pad, t_sample):
    m, d = merged.shape
    n_p = xp.shape[0]
    tm = 2 * ROW_TILE
    npt = n_p // tm
    tps = t_sample // tm
    row = functools.partial(_mod_row, n_prompt_tiles=npt, tiles_per_sample_seq=tps)
    mod = pl.BlockSpec((None, 1, d), lambda i: (row(i), 0, 0))
    return pl.pallas_call(
        functools.partial(_outproj_kernel, n_prompt_tiles=npt, n_experts=N_EXPERTS),
        grid=(m // tm,),
        in_specs=[pl.BlockSpec((tm, d), lambda i: (i, 0)),
                  pl.BlockSpec((tm, d), lambda i: (jnp.minimum(i, npt - 1), 0)),
                  pl.BlockSpec((tm, d), lambda i: (jnp.maximum(i - npt, 0), 0)),
                  pl.BlockSpec((d, d), lambda i: (0, 0)),
                  mod, mod, mod,
                  pl.BlockSpec((1, d), lambda i: (0, 0)),
                  pl.BlockSpec((d, LANES), lambda i: (0, 0))],
        out_specs=[pl.BlockSpec((tm, d), lambda i: (i, 0)),
                   pl.BlockSpec((tm, d), lambda i: (i, 0)),
                   pl.BlockSpec((tm, LANES), lambda i: (i, 0))],
        out_shape=[jax.ShapeDtypeStruct((m, d), F32),
                   jax.ShapeDtypeStruct((m, d), BF16),
                   jax.ShapeDtypeStruct((m, LANES), F32)],
        compiler_params=_params("arbitrary"),
        name="out_proj_router",
    )(merged, xp, xs, w_out_b, gate1, scale2, shift2, norm2, w_router_pad)


def _select_kernel(a_ref, slot_ref, *, cap):
    a = a_ref[...]
    r, t = a.shape

    def as_float(bits):
        return pltpu.bitcast(bits, F32)

    def bisect(_, carry):
        lo, hi = carry
        mid = lo + jnp.right_shift(hi - lo + 1, 1)
        cnt = jnp.sum(jnp.where(a >= as_float(mid), 1.0, 0.0), axis=-1, keepdims=True)
        ok = cnt >= cap
        return jnp.where(ok, mid, lo), jnp.where(ok, hi, mid - 1)

    lo0 = jnp.zeros((r, 1), jnp.int32)
    hi0 = jnp.full((r, 1), 0x3F800000, jnp.int32)
    thr, _ = lax.fori_loop(0, 31, bisect, (lo0, hi0))

    gt = jnp.where(a >= as_float(thr + 1), 1.0, 0.0)
    eq = jnp.where(a >= as_float(thr), 1.0, 0.0) - gt
    need = cap - jnp.sum(gt, axis=-1, keepdims=True)
    before = lax.broadcasted_iota(jnp.int32, (t, t), 0) < lax.broadcasted_iota(jnp.int32, (t, t), 1)
    tri = jnp.where(before, 1.0, 0.0).astype(BF16)
    eq_rank = _dot(eq.astype(BF16), tri)
    sel = gt + eq * jnp.where(eq_rank < need, 1.0, 0.0)
    pos = _dot(sel.astype(BF16), tri)
    slot_ref[...] = jnp.where(sel > 0.5, pos, -1.0).astype(jnp.int32)


def _select_call(aff_t, cap):
    r, t = aff_t.shape
    return pl.pallas_call(
        functools.partial(_select_kernel, cap=cap),
        grid=(1,),
        in_specs=[pl.BlockSpec((r, t), lambda i: (0, 0))],
        out_specs=pl.BlockSpec((r, t), lambda i: (0, 0)),
        out_shape=jax.ShapeDtypeStruct((r, t), jnp.int32),
        compiler_params=_params("arbitrary"),
        name="select",
    )(aff_t)


def _gather_kernel(slot_ref, aff_ref, h_ref, xs_ref, gate_ref, *, cap, group):
    e_total, t = slot_ref.shape
    h = h_ref[...]
    ci = lax.broadcasted_iota(jnp.int32, (cap, t), 0)
    for e0 in range(0, e_total, group):
        hots = []
        for e in range(e0, e0 + group):
            hit = ci == slot_ref[e:e + 1, :]
            hots.append(jnp.where(hit, 1.0, 0.0).astype(BF16))
            gate_ref[e] = jnp.sum(jnp.where(hit, aff_ref[e:e + 1, :], 0.0), axis=-1, keepdims=True)
        onehot = hots[0] if group == 1 else jnp.concatenate(hots, axis=0)
        xs = _dot(onehot, h).astype(xs_ref.dtype)
        xs_ref[e0:e0 + group] = xs.reshape(group, cap, xs.shape[-1])


def _gather_call(slot_t, aff_t, h2, *, row0, nseq, t, cap):
    e = slot_t.shape[1]
    d = h2.shape[1]
    rb0 = row0 // t
    group = max(1, min(e, 512 // cap))
    return pl.pallas_call(
        functools.partial(_gather_kernel, cap=cap, group=group),
        grid=(nseq,),
        in_specs=[pl.BlockSpec((None, e, t), lambda b: (b, 0, 0)),
                  pl.BlockSpec((None, e, t), lambda b: (b, 0, 0)),
                  pl.BlockSpec((t, d), lambda b: (rb0 + b, 0))],
        out_specs=[pl.BlockSpec((e, cap, d), lambda b: (0, b, 0)),
                   pl.BlockSpec((e, cap, 1), lambda b: (0, b, 0))],
        out_shape=[jax.ShapeDtypeStruct((e, nseq * cap, d), BF16),
                   jax.ShapeDtypeStruct((e, nseq * cap, 1), F32)],
        compiler_params=_params("arbitrary"),
        name="gather",
    )(slot_t, aff_t, h2)


def _ffn_kernel(xp_ref, xs_ref, gp_ref, gs_ref, wg_ref, wu_ref, wd_ref, yp_ref, ys_ref,
                accp, accs, wgb, wub, wdb, *, chunk):
    f = pl.program_id(1)
    wgb[...] = wg_ref[...].astype(BF16)
    wub[...] = wu_ref[...].astype(BF16)
    wdb[...] = wd_ref[...].astype(BF16)

    @pl.when(f == 0)
    def _():
        accp[...] = jnp.zeros_like(accp)
        accs[...] = jnp.zeros_like(accs)

    def part(x_ref, acc):
        m = x_ref.shape[0]
        mc = min(chunk, m)
        for m0 in range(0, m, mc):
            x = x_ref[m0:m0 + mc, :]
            hg = _dot(x, wgb[...])
            hu = _dot(x, wub[...])
            hid = (hg * _sigmoid(hg) * hu).astype(BF16)
            acc[m0:m0 + mc, :] += _dot(hid, wdb[...])

    part(xp_ref, accp)
    part(xs_ref, accs)

    @pl.when(f == pl.num_programs(1) - 1)
    def _():
        yp_ref[...] = (accp[...] * gp_ref[...]).astype(yp_ref.dtype)
        ys_ref[...] = (accs[...] * gs_ref[...]).astype(ys_ref.dtype)


def _ffn_call(xs_p, xs_s, gate_p, gate_s, w_gate, w_up, w_down):
    e, mp, d = xs_p.shape
    ms = xs_s.shape[1]
    ff = w_gate.shape[2]
    tf = 256 if ff % 256 == 0 else ff
    tok = lambda m, w: pl.BlockSpec((None, m, w), lambda ee, f: (ee, 0, 0))
    return pl.pallas_call(
        functools.partial(_ffn_kernel, chunk=512),
        grid=(e, ff // tf),
        in_specs=[tok(mp, d), tok(ms, d), tok(mp, 1), tok(ms, 1),
                  pl.BlockSpec((None, d, tf), lambda ee, f: (ee, 0, f)),
                  pl.BlockSpec((None, d, tf), lambda ee, f: (ee, 0, f)),
                  pl.BlockSpec((None, tf, d), lambda ee, f: (ee, f, 0))],
        out_specs=[pl.BlockSpec((None, mp, d), lambda ee, f: (ee, 0, 0), pipeline_mode=pl.Buffered(1)),
                   pl.BlockSpec((None, ms, d), lambda ee, f: (ee, 0, 0), pipeline_mode=pl.Buffered(1))],
        out_shape=[jax.ShapeDtypeStruct((e, mp, d), BF16), jax.ShapeDtypeStruct((e, ms, d), BF16)],
        scratch_shapes=[pltpu.VMEM((mp, d), F32), pltpu.VMEM((ms, d), F32),
                        pltpu.VMEM((d, tf), BF16), pltpu.VMEM((d, tf), BF16), pltpu.VMEM((tf, d), BF16)],
        compiler_params=_params("arbitrary", "arbitrary"),
        name="expert_ffn",
    )(xs_p, xs_s, gate_p, gate_s, w_gate, w_up, w_down)


def _scatter_kernel(slot_ref, y_ref, x1_ref, g2_ref, wn_ref, o_ref, *, cap):
    e_total = y_ref.shape[0]
    tm = slot_ref.shape[0]
    slot = slot_ref[...]
    lane = lax.broadcasted_iota(jnp.int32, (tm, LANES), 1)
    per_block = max(1, LANES // cap)
    blocks = []
    for b0 in range(0, e_total, per_block):
        acc = jnp.zeros((tm, LANES), F32)
        for j in range(per_block):
            s = slot[:, b0 + j:b0 + j + 1]
            key = jnp.where(s >= 0, s + j * cap, -1)
            acc = acc + jnp.where(lane == key, 1.0, 0.0)
        blocks.append(acc.astype(BF16))
    onehot = jnp.concatenate(blocks, axis=1)
    y = y_ref[...].reshape(e_total * cap, y_ref.shape[-1])
    moe = _dot(onehot, y)
    x2 = x1_ref[...] + g2_ref[...] * moe
    o_ref[...] = x2 * lax.rsqrt(jnp.mean(x2 * x2, axis=-1, keepdims=True) + EPS) * wn_ref[...]


def _scatter_call(slot, y, x1, gate2, final_norm, *, row0, nseq, t, cap, mod_row0, mod_per_seq):
    e, _, d = y.shape
    tm = ROW_TILE
    nt = t // tm
    rb0 = row0 // tm
    assert cap == LANES or LANES % cap == 0
    return pl.pallas_call(
        functools.partial(_scatter_kernel, cap=cap),
        grid=(nseq, nt),
        in_specs=[pl.BlockSpec((None, tm, e), lambda b, i: (b, i, 0)),
                  pl.BlockSpec((e, cap, d), lambda b, i: (0, b, 0)),
                  pl.BlockSpec((tm, d), lambda b, i: (rb0 + b * nt + i, 0)),
                  pl.BlockSpec((None, 1, d), lambda b, i: (mod_row0 + b * mod_per_seq, 0, 0)),
                  pl.BlockSpec((1, d), lambda b, i: (0, 0))],
        out_specs=pl.BlockSpec((tm, d), lambda b, i: (b * nt + i, 0)),
        out_shape=jax.ShapeDtypeStruct((nseq * t, d), F32),
        compiler_params=_params("arbitrary", "arbitrary"),
        name="scatter_final",
    )(slot, y, x1, gate2, final_norm)


def kernel(x_prompt, x_sample, state_s5_re, state_s5_im, state_ret, c, c_ctx, final_norm, w_ada, b_ada, norm1, norm2, w_in, s5_a_re, s5_a_im, s5_log_dt, s5_b_re, s5_b_im, s5_c_re, s5_c_im, s5_d, w_s5_glu, ret_decay_logit, ret_gn_w, w_ret_out, w_out, w_router, w_exp_gate, w_exp_up, w_exp_down):
    bp, tp, d = x_prompt.shape
    bs, ts, _ = x_sample.shape
    depth = w_ada.shape[0]
    n_p, n_s = bp * tp, bs * ts
    xp = x_prompt.reshape(n_p, d)
    xs = x_sample.reshape(n_s, d)

    mod_rows = 16
    cvec = jnp.zeros((mod_rows, d), F32).at[0].set(c_ctx).at[1:1 + bs].set(c)
    rope_tabs = _grid_rope_tables(ts)
    cap_p = CAPACITY_FACTOR * tp // N_EXPERTS
    cap_s = CAPACITY_FACTOR * ts // N_EXPERTS

    new_re, new_im, new_ret = [], [], []
    for l in range(depth):
        mods = _ada_call(cvec, w_ada[l], b_ada[l])
        shift1, scale1, gate1, shift2, scale2, gate2 = [m.reshape(mod_rows, 1, d) for m in jnp.split(mods, 6, axis=-1)]

        h = _norm_mod_call(xp, xs, scale1, shift1, norm1[l].reshape(1, d), ts)
        proj = _proj_call(h, w_in[l])

        m_cmp, ws_cmp, wy_cmp, a_step = _s5_chunk_weights(
            s5_a_re[l], s5_a_im[l], s5_log_dt[l], s5_b_re[l], s5_b_im[l], s5_c_re[l], s5_c_im[l], s5_d[l])
        m_mat, ws_mat, wy_mat = _s5_expand_call(m_cmp, ws_cmp, wy_cmp)
        u = proj[:, :S5_WIDTH]
        h0_p = jnp.zeros((S5_NGB, bp, S5_SCOLS), F32)
        h0_s = _state_to_lanes(state_s5_re[:, l].astype(F32), state_s5_im[:, l].astype(F32))
        yg_p, hfin_p = _s5_call(_to_chunk_major(u[:n_p], bp, tp), m_mat, ws_mat, wy_mat, a_step, h0_p, bp, tp // S5_CHUNK)
        yg_s, _ = _s5_call(_to_chunk_major(u[n_p:], bs, ts), m_mat, ws_mat, wy_mat, a_step, h0_s, bs, ts // S5_CHUNK)
        yg = jnp.concatenate([_from_chunk_major(yg_p, bp, tp), _from_chunk_major(yg_s, bs, ts)], axis=0)
        s5_re, s5_im = _lanes_to_state(hfin_p)
        new_re.append(s5_re)
        new_im.append(s5_im)

        log_gamma = jax.nn.log_sigmoid(ret_decay_logit[l].astype(F32))
        gn_w = ret_gn_w[l].reshape(1, -1).astype(F32)
        o_p, sfin = _ret_call(proj, log_gamma, gn_w, None, None, row0=0, nseq=bp, t=tp, latent=False, want_state=True)
        o_s, = _ret_call(proj, log_gamma, gn_w, rope_tabs, state_ret[:, l].astype(F32), row0=n_p, nseq=bs, t=ts,
                         latent=True, want_state=False)
        new_ret.append(sfin)
        merged = _merge_call(yg, o_p, o_s, proj, w_s5_glu[l], w_ret_out[l])
        w_router_pad = jnp.zeros((d, LANES), F32).at[:, :N_EXPERTS].set(w_router[l].astype(F32))
        x1, h2, aff = _outproj_call(merged, xp, xs, w_out[l].astype(BF16), gate1, scale2, shift2,
                                    norm2[l].reshape(1, d), w_router_pad, ts)

        aff = aff[:, :N_EXPERTS]
        aff_p = jnp.transpose(aff[:n_p].reshape(bp, tp, N_EXPERTS), (0, 2, 1))
        aff_s = jnp.transpose(aff[n_p:].reshape(bs, ts, N_EXPERTS), (0, 2, 1))
        slot_p = _select_call(aff_p.reshape(bp * N_EXPERTS, tp), cap_p).reshape(bp, N_EXPERTS, tp)
        slot_s = _select_call(aff_s.reshape(bs * N_EXPERTS, ts), cap_s).reshape(bs, N_EXPERTS, ts)
        xe_p, ge_p = _gather_call(slot_p, aff_p, h2, row0=0, nseq=bp, t=tp, cap=cap_p)
        xe_s, ge_s = _gather_call(slot_s, aff_s, h2, row0=n_p, nseq=bs, t=ts, cap=cap_s)
        ye_p, ye_s = _ffn_call(xe_p, xe_s, ge_p, ge_s, w_exp_gate[l], w_exp_up[l], w_exp_down[l])

        last = l == depth - 1
        wn = final_norm.reshape(1, d).astype(F32) if last else None
        assert last, "only the final layer applies the output norm in the scatter kernel"
        yp = _scatter_call(jnp.transpose(slot_p, (0, 2, 1)), ye_p, x1, gate2, wn, row0=0, nseq=bp, t=tp, cap=cap_p,
                           mod_row0=0, mod_per_seq=0)
        ysm = _scatter_call(jnp.transpose(slot_s, (0, 2, 1)), ye_s, x1, gate2, wn, row0=n_p, nseq=bs, t=ts, cap=cap_s,
                            mod_row0=1, mod_per_seq=1)

    y_prompt = yp.reshape(bp, tp, d)
    y_sample = ysm.reshape(bs, ts, d)
    return (y_prompt, y_sample, jnp.stack(new_re, axis=1), jnp.stack(new_im, axis=1), jnp.stack(new_ret, axis=1))
```

```python
import functools
import math

import jax
import jax.numpy as jnp
from jax import lax
from jax.experimental import pallas as pl
from jax.experimental.pallas import tpu as pltpu

F32 = jnp.float32
BF16 = jnp.bfloat16
HIGHEST = lax.Precision.HIGHEST

EPS = 1e-6
GRID_W = 64
S5_WIDTH = 1024
S5_GROUP = 16
S5_STATE = 64
RET_HEADS = 8
RET_DK = 128
RET_DV = 256
ROPE_BASE = 10000.0
N_EXPERTS = 16
CAPACITY_FACTOR = 2

LANES = 128
S5_CHUNK = 8
S5_GB = LANES // S5_GROUP
S5_NGB = S5_WIDTH // LANES
S5_SCOLS = 4 * S5_GB * S5_STATE
ROW_TILE = 256
VMEM_LIMIT = 56 * 1024 * 1024


def _params(*sem):
    return pltpu.CompilerParams(dimension_semantics=sem, vmem_limit_bytes=VMEM_LIMIT)


def _sigmoid(x):
    return 1.0 / (1.0 + jnp.exp(-x))


def _gelu_tanh(x):
    return 0.5 * x * (1.0 + jnp.tanh(math.sqrt(2.0 / math.pi) * (x + 0.044715 * (x * x * x))))


def _dot(a, b):
    return jnp.dot(a, b, preferred_element_type=F32)


def _ada_kernel(c_ref, w_ref, b_ref, o_ref):
    c = c_ref[...]
    s = c * _sigmoid(c)
    o_ref[...] = jnp.dot(s, w_ref[...], precision=HIGHEST, preferred_element_type=F32) + b_ref[...]


def _ada_call(cvec, w_ada, b_ada):
    r, d = cvec.shape
    n = w_ada.shape[1]
    tn = 1024 if n % 1024 == 0 else 512
    return pl.pallas_call(
        _ada_kernel,
        grid=(n // tn,),
        in_specs=[pl.BlockSpec((r, d), lambda j: (0, 0)),
                  pl.BlockSpec((d, tn), lambda j: (0, j)),
                  pl.BlockSpec((1, tn), lambda j: (0, j))],
        out_specs=pl.BlockSpec((r, tn), lambda j: (0, j)),
        out_shape=jax.ShapeDtypeStruct((r, n), F32),
        compiler_params=_params("arbitrary"),
        name="ada",
    )(cvec, w_ada, b_ada.reshape(1, n))


def _norm_mod_kernel(xp_ref, xs_ref, sc_ref, sh_ref, w_ref, o_ref, *, n_prompt_tiles):
    i = pl.program_id(0)

    def body(x_ref):
        x = x_ref[...]
        y = x * lax.rsqrt(jnp.mean(x * x, axis=-1, keepdims=True) + EPS)
        h = (y * w_ref[...]) * (1.0 + sc_ref[...]) + sh_ref[...]
        o_ref[...] = h.astype(o_ref.dtype)

    @pl.when(i < n_prompt_tiles)
    def _():
        body(xp_ref)

    @pl.when(i >= n_prompt_tiles)
    def _():
        body(xs_ref)


def _mod_row(i, n_prompt_tiles, tiles_per_sample_seq):
    return jnp.where(i < n_prompt_tiles, 0, 1 + (i - n_prompt_tiles) // tiles_per_sample_seq)


def _norm_mod_call(xp, xs, scale, shift, w, t_sample):
    n_p, d = xp.shape
    n_s = xs.shape[0]
    tm = ROW_TILE
    npt, nst = n_p // tm, n_s // tm
    tps = t_sample // tm
    row = functools.partial(_mod_row, n_prompt_tiles=npt, tiles_per_sample_seq=tps)
    return pl.pallas_call(
        functools.partial(_norm_mod_kernel, n_prompt_tiles=npt),
        grid=(npt + nst,),
        in_specs=[pl.BlockSpec((tm, d), lambda i: (jnp.minimum(i, npt - 1), 0)),
                  pl.BlockSpec((tm, d), lambda i: (jnp.maximum(i - npt, 0), 0)),
                  pl.BlockSpec((None, 1, d), lambda i: (row(i), 0, 0)),
                  pl.BlockSpec((None, 1, d), lambda i: (row(i), 0, 0)),
                  pl.BlockSpec((1, d), lambda i: (0, 0))],
        out_specs=pl.BlockSpec((tm, d), lambda i: (i, 0)),
        out_shape=jax.ShapeDtypeStruct((n_p + n_s, d), BF16),
        compiler_params=_params("arbitrary"),
        name="norm_mod",
    )(xp, xs, scale, shift, w)


def _proj_kernel(a_ref, w_ref, o_ref, wb_ref):
    @pl.when(pl.program_id(1) == 0)
    def _():
        wb_ref[...] = w_ref[...].astype(BF16)

    o_ref[...] = _dot(a_ref[...], wb_ref[...]).astype(o_ref.dtype)


def _proj_call(a, w):
    m, k = a.shape
    n = w.shape[1]
    tm = 1024 if m % 1024 == 0 else m
    tn = 1024 if n % 1024 == 0 else 512
    return pl.pallas_call(
        _proj_kernel,
        grid=(n // tn, m // tm),
        in_specs=[pl.BlockSpec((tm, k), lambda j, i: (i, 0)),
                  pl.BlockSpec((k, tn), lambda j, i: (0, j))],
        out_specs=pl.BlockSpec((tm, tn), lambda j, i: (i, j)),
        out_shape=jax.ShapeDtypeStruct((m, n), BF16),
        scratch_shapes=[pltpu.VMEM((k, tn), BF16)],
        compiler_params=_params("arbitrary", "arbitrary"),
        name="in_proj",
    )(a, w)


def _cmul(ar, ai, br, bi):
    return ar * br - ai * bi, ar * bi + ai * br


def _s5_chunk_weights(a_re, a_im, log_dt, b_re, b_im, c_re, c_im, d_skip):
    L, G, P, HG, GB, NGB = S5_CHUNK, S5_WIDTH // S5_GROUP, S5_STATE, S5_GROUP, S5_GB, S5_NGB
    dt = jnp.exp(log_dt.astype(F32))[..., None]
    adt_re, adt_im = a_re.astype(F32) * dt, a_im.astype(F32) * dt
    mag = jnp.exp(adt_re)
    lam_re, lam_im = mag * jnp.cos(adt_im), mag * jnp.sin(adt_im)
    den = a_re * a_re + a_im * a_im
    q_re = ((lam_re - 1.0) * a_re + lam_im * a_im) / den
    q_im = (lam_im * a_re - (lam_re - 1.0) * a_im) / den
    bb_re, bb_im = _cmul(q_re[..., None], q_im[..., None], b_re.astype(F32), b_im.astype(F32))
    n = jnp.arange(L + 1, dtype=F32)
    pmag = jnp.exp(adt_re[..., None] * n)
    pw_re, pw_im = pmag * jnp.cos(adt_im[..., None] * n), pmag * jnp.sin(adt_im[..., None] * n)
    cr, ci = c_re.astype(F32), c_im.astype(F32)
    cl_re, cl_im = _cmul(cr[..., None], ci[..., None], pw_re[:, :, None], pw_im[:, :, None])
    kk = (jnp.einsum('dgopn,dgpi->dgnoi', cl_re, bb_re, precision=HIGHEST)
          - jnp.einsum('dgopn,dgpi->dgnoi', cl_im, bb_im, precision=HIGHEST))[:, :, :L]
    k0 = kk[0, :, 0] + kk[1, :, 0] + jax.vmap(jnp.diag)(d_skip.astype(F32).reshape(G, HG))
    kc = jnp.concatenate([kk[1][:, 1:][:, ::-1], k0[:, None], kk[0][:, 1:]], axis=1)
    idx = jnp.arange(L)[None, :] - jnp.arange(L)[:, None] + (L - 1)
    tpl = jnp.transpose(kc[:, idx], (0, 1, 4, 2, 3)).reshape(NGB, GB, L, HG, L, HG)
    m_cmp = jnp.transpose(tpl, (0, 2, 1, 3, 4, 5)).reshape(NGB, L * LANES, L * HG)

    pf_re, pf_im = pw_re[0][..., :L][..., ::-1], pw_im[0][..., :L][..., ::-1]
    pb_re, pb_im = pw_re[1][..., :L], pw_im[1][..., :L]
    sf_re, sf_im = _cmul(pf_re[..., None], pf_im[..., None], bb_re[0][:, :, None], bb_im[0][:, :, None])
    sb_re, sb_im = _cmul(pb_re[..., None], pb_im[..., None], bb_re[1][:, :, None], bb_im[1][:, :, None])
    ws = jnp.stack([jnp.stack([sf_re, sb_re]), jnp.stack([sf_im, sb_im])])
    ws = jnp.transpose(ws, (2, 4, 5, 0, 1, 3)).reshape(NGB, GB, L, HG, 2, 2, P)
    ws_cmp = jnp.transpose(ws, (0, 2, 1, 3, 4, 5, 6)).reshape(NGB, L * LANES, 4 * P)

    yf_re, yf_im = cl_re[0][..., 1:], cl_im[0][..., 1:]
    yb_re, yb_im = cl_re[1][..., 1:][..., ::-1], cl_im[1][..., 1:][..., ::-1]
    wy = jnp.stack([jnp.stack([yf_re, yb_re]), jnp.stack([-yf_im, -yb_im])])
    wy = jnp.transpose(wy, (2, 0, 1, 4, 5, 3)).reshape(NGB, GB, 2, 2, P, L, HG)
    wy_cmp = jnp.transpose(wy, (0, 2, 3, 1, 4, 5, 6)).reshape(NGB, S5_SCOLS, L * HG)

    def lanes(x):
        return jnp.transpose(x.reshape(2, NGB, GB * P), (1, 0, 2)).reshape(NGB, 1, 2 * GB * P)

    a_step = jnp.concatenate([lanes(pw_re[..., L]), lanes(pw_im[..., L])], axis=1)
    return m_cmp.astype(BF16), ws_cmp.astype(BF16), wy_cmp.astype(BF16), a_step


def _s5_expand(xm_ref, xws_ref, zwy_ref, m_ref, ws_ref, wy_ref):
    hg, p = S5_GROUP, S5_STATE
    lg_hg, lg_p, lg_lanes = hg.bit_length() - 1, p.bit_length() - 1, LANES.bit_length() - 1
    gmask = S5_GB - 1
    rblk, cblk = 512, 1024

    def expand(x_ref, src_of_col, row_group, col_group, o_ref):
        rows, k = x_ref.shape
        for c0 in range(0, o_ref.shape[1], cblk):
            kk = lax.broadcasted_iota(jnp.int32, (k, cblk), 0)
            cc = lax.broadcasted_iota(jnp.int32, (k, cblk), 1) + c0
            spread = jnp.where(kk == src_of_col(cc), 1.0, 0.0).astype(BF16)
            for r0 in range(0, rows, rblk):
                full = _dot(x_ref[r0:r0 + rblk, :], spread)
                rr = lax.broadcasted_iota(jnp.int32, (rblk, cblk), 0) + r0
                c2 = lax.broadcasted_iota(jnp.int32, (rblk, cblk), 1) + c0
                keep = row_group(rr) == col_group(c2)
                o_ref[r0:r0 + rblk, c0:c0 + cblk] = jnp.where(keep, full, 0.0).astype(o_ref.dtype)

    lane_group = lambda i: jnp.right_shift(i, lg_hg) & gmask
    state_group = lambda i: jnp.right_shift(i, lg_p) & gmask
    chunk_src = lambda c: jnp.right_shift(c, lg_lanes) * hg + (c & (hg - 1))
    state_src = lambda c: jnp.right_shift(c, lg_p + 3) * p + (c & (p - 1))
    expand(xm_ref, chunk_src, lane_group, lane_group, m_ref)
    expand(xws_ref, state_src, lane_group, state_group, ws_ref)
    expand(zwy_ref, chunk_src, state_group, lane_group, wy_ref)


def _s5_kernel(*refs, paths):
    nu = S5_CHUNK
    u_refs = refs[:nu]
    (xm_ref, xws_ref, zwy_ref, a_ref, h0_ref, y_ref, hfin_ref,
     wm, wws, wwy, s_scr, hb_scr, y_scr) = refs[nu:]
    half = S5_SCOLS // 4
    nsl = half // LANES
    cw = 2 * LANES

    _s5_expand(xm_ref, xws_ref, zwy_ref, wm, wws, wwy)
    a = a_ref[...]

    for row0, nseq, nchunk, has_h0 in paths:
        r = nseq * nchunk

        def u_rows():
            return jnp.concatenate([ur[row0:row0 + r, :] for ur in u_refs], axis=1)

        for q in range(4):
            sq = _dot(u_rows(), wws[:, q * half:(q + 1) * half])
            for k in range(nsl):
                for s in range(nseq):
                    s_scr[q * nsl + k, pl.ds(s, nchunk, stride=nseq), :] = (
                        sq[s * nchunk:(s + 1) * nchunk, k * LANES:(k + 1) * LANES])

        if has_h0:
            h0 = h0_ref[...]
            carry0 = tuple(h0[:, j * LANES:(j + 1) * LANES] for j in range(4 * nsl))
        else:
            carry0 = tuple(jnp.zeros((nseq, LANES), F32) for _ in range(4 * nsl))

        def step(i, carry, nseq=nseq, nchunk=nchunk):
            rf = pl.multiple_of(i * nseq, nseq)
            rb = pl.multiple_of((nchunk - 1 - i) * nseq, nseq)
            new = list(carry)
            for k in range(nsl):
                for re, im, rows, acol in ((k, 2 * nsl + k, rf, k), (nsl + k, 3 * nsl + k, rb, nsl + k)):
                    s_re = s_scr[re, pl.ds(rows, nseq), :]
                    s_im = s_scr[im, pl.ds(rows, nseq), :]
                    h_re, h_im = carry[re], carry[im]
                    s_scr[re, pl.ds(rows, nseq), :] = h_re
                    s_scr[im, pl.ds(rows, nseq), :] = h_im
                    a_re = a[0:1, acol * LANES:(acol + 1) * LANES]
                    a_im = a[1:2, acol * LANES:(acol + 1) * LANES]
                    new[re] = a_re * h_re - a_im * h_im + s_re
                    new[im] = a_re * h_im + a_im * h_re + s_im
            return tuple(new)

        fin = lax.fori_loop(0, nchunk, step, carry0)
        if not has_h0:
            for j in range(4 * nsl):
                hfin_ref[:, j * LANES:(j + 1) * LANES] = fin[j]

        for j in range(4 * nsl):
            for s in range(nseq):
                hb_scr[s * nchunk:(s + 1) * nchunk, j * LANES:(j + 1) * LANES] = (
                    s_scr[j, pl.ds(s, nchunk, stride=nseq), :].astype(BF16))

        for j in range(nu * LANES // cw):
            y = _dot(u_rows(), wm[:, j * cw:(j + 1) * cw]) + _dot(hb_scr[0:r, :], wwy[:, j * cw:(j + 1) * cw])
            g = _gelu_tanh(y)
            for t2 in range(cw // LANES):
                tau = j * (cw // LANES) + t2
                y_scr[pl.ds(row0 * nu + tau, r, stride=nu), :] = g[:, t2 * LANES:(t2 + 1) * LANES]

    y_ref[...] = y_scr[...].astype(y_ref.dtype)


def _s5_call(proj, m_cmp, ws_cmp, wy_cmp, a_step, h0, paths):
    n_tok, cols = proj.shape
    rows = n_tok // S5_CHUNK
    cb = cols // LANES
    proj_rows = proj.reshape(rows, S5_CHUNK * cols)
    sc, k = S5_SCOLS, S5_CHUNK * LANES
    rmax = max(p[1] * p[2] for p in paths)
    nseq_fin = [p[1] for p in paths if not p[3]][0]
    blk = lambda a, b: pl.BlockSpec((None, a, b), lambda g: (g, 0, 0))
    u_specs = [pl.BlockSpec((rows, LANES), lambda g, tau=tau: (0, tau * cb + g)) for tau in range(S5_CHUNK)]
    return pl.pallas_call(
        functools.partial(_s5_kernel, paths=paths),
        grid=(S5_NGB,),
        in_specs=u_specs + [blk(k, m_cmp.shape[2]), blk(k, ws_cmp.shape[2]), blk(sc, wy_cmp.shape[2]),
                            blk(2, sc // 2), blk(h0.shape[1], sc)],
        out_specs=[pl.BlockSpec((n_tok, LANES), lambda g: (0, g)), blk(nseq_fin, sc)],
        out_shape=[jax.ShapeDtypeStruct((n_tok, S5_WIDTH), BF16),
                   jax.ShapeDtypeStruct((S5_NGB, nseq_fin, sc), F32)],
        scratch_shapes=[pltpu.VMEM((k, k), BF16), pltpu.VMEM((k, sc), BF16), pltpu.VMEM((sc, k), BF16),
                        pltpu.VMEM((4 * (sc // 4 // LANES), rmax, LANES), F32), pltpu.VMEM((rmax, sc), BF16),
                        pltpu.VMEM((n_tok, LANES), F32)],
        compiler_params=_params("arbitrary"),
        name="s5",
    )(*([proj_rows] * S5_CHUNK), m_cmp, ws_cmp, wy_cmp, a_step, h0)


def _state_to_lanes(s_re, s_im):
    b = s_re.shape[0]

    def one(x):
        return jnp.transpose(x.reshape(b, 2, S5_NGB, S5_GB * S5_STATE), (2, 0, 1, 3)).reshape(S5_NGB, b, -1)

    return jnp.concatenate([one(s_re), one(s_im)], axis=-1)


def _lanes_to_state(h):
    b = h.shape[1]
    x = h.reshape(S5_NGB, b, 2, 2, S5_GB, S5_STATE)
    x = jnp.transpose(x, (2, 1, 3, 0, 4, 5)).reshape(2, b, 2, S5_NGB * S5_GB, S5_STATE)
    return x[0], x[1]


def _ret_kernel(lg_ref, q_ref, k_ref, v_ref, g_ref, gn_ref, *rest, t, latent, want_state):
    rest = list(rest)
    if latent:
        cos_ref, sin_ref, s0_ref = rest[:3]
        rest = rest[3:]
    o_ref = rest.pop(0)
    sfin_ref = rest.pop(0) if want_state else None
    dmat_ref = rest.pop(0)

    h = pl.program_id(0)
    lgf = lg_ref[0, h]
    lgb = lg_ref[1, h]

    @pl.when(pl.program_id(1) == 0)
    def _():
        ti = lax.broadcasted_iota(jnp.int32, (t, t), 0)
        si = lax.broadcasted_iota(jnp.int32, (t, t), 1)
        diff = (ti - si).astype(F32)
        dmat_ref[...] = jnp.exp(jnp.where(diff >= 0, lgf * diff, -lgb * diff))

    q = q_ref[...].astype(F32) * (RET_DK ** -0.5)
    k = k_ref[...].astype(F32)
    if latent:
        lane = lax.broadcasted_iota(jnp.int32, (t, RET_DK), 1)
        first = (lane % (RET_DK // 2)) < (RET_DK // 4)

        def rope(x):
            swapped = jnp.where(first, pltpu.roll(x, RET_DK - RET_DK // 4, 1), pltpu.roll(x, RET_DK // 4, 1))
            return x * cos_ref[...] + swapped * sin_ref[...]

        q = rope(q)
        k = rope(k)
    qb = q.astype(BF16)
    kb = k.astype(BF16)
    v = v_ref[...]

    scores = lax.dot_general(qb, kb, (((1,), (1,)), ((), ())), preferred_element_type=F32)
    o = _dot((scores * dmat_ref[...]).astype(BF16), v)

    pos = lax.broadcasted_iota(jnp.int32, (t, 1), 0).astype(F32)
    if latent:
        o = o + _dot(qb, s0_ref[0].astype(BF16)) * jnp.exp(lgf * (pos + 1.0))
        o = o + _dot(qb, s0_ref[1].astype(BF16)) * jnp.exp(lgb * (t - pos))
    if want_state:
        kf = (k * jnp.exp(lgf * (t - 1.0 - pos))).astype(BF16)
        kr = (k * jnp.exp(lgb * pos)).astype(BF16)
        tn = (((0,), (0,)), ((), ()))
        sfin_ref[0] = lax.dot_general(kf, v, tn, preferred_element_type=F32)
        sfin_ref[1] = lax.dot_general(kr, v, tn, preferred_element_type=F32)

    mu = jnp.mean(o, axis=-1, keepdims=True)
    d = o - mu
    var = jnp.mean(d * d, axis=-1, keepdims=True)
    on = d * lax.rsqrt(var + EPS) * gn_ref[...]
    g = g_ref[...].astype(F32)
    o_ref[...] = (g * _sigmoid(g) * on).astype(o_ref.dtype)


def _ret_call(proj, log_gamma, gn_w, rope_tabs, s0, *, row0, nseq, t, latent, want_state):
    h, dk, dv = RET_HEADS, RET_DK, RET_DV
    rb0 = row0 // t
    q0 = S5_WIDTH // dk
    k0 = q0 + h
    v0 = (S5_WIDTH + 2 * h * dk) // dv
    g0 = v0 + h
    in_specs = [pl.BlockSpec(memory_space=pltpu.SMEM),
                pl.BlockSpec((t, dk), lambda hh, b: (rb0 + b, q0 + hh)),
                pl.BlockSpec((t, dk), lambda hh, b: (rb0 + b, k0 + hh)),
                pl.BlockSpec((t, dv), lambda hh, b: (rb0 + b, v0 + hh)),
                pl.BlockSpec((t, dv), lambda hh, b: (rb0 + b, g0 + hh)),
                pl.BlockSpec((1, dv), lambda hh, b: (0, hh))]
    args = [log_gamma, proj, proj, proj, proj, gn_w]
    if latent:
        in_specs += [pl.BlockSpec((t, dk), lambda hh, b: (0, 0)),
                     pl.BlockSpec((t, dk), lambda hh, b: (0, 0)),
                     pl.BlockSpec((None, 2, None, dk, dv), lambda hh, b: (b, 0, hh, 0, 0))]
        args += [rope_tabs[0], rope_tabs[1], s0]
    out_specs = [pl.BlockSpec((t, dv), lambda hh, b: (b, hh))]
    out_shape = [jax.ShapeDtypeStruct((nseq * t, h * dv), BF16)]
    if want_state:
        out_specs.append(pl.BlockSpec((None, 2, None, dk, dv), lambda hh, b: (b, 0, hh, 0, 0)))
        out_shape.append(jax.ShapeDtypeStruct((nseq, 2, h, dk, dv), F32))
    return pl.pallas_call(
        functools.partial(_ret_kernel, t=t, latent=latent, want_state=want_state),
        grid=(h, nseq),
        in_specs=in_specs,
        out_specs=out_specs,
        out_shape=out_shape,
        scratch_shapes=[pltpu.VMEM((t, t), F32)],
        compiler_params=_params("arbitrary", "arbitrary"),
        name="retention_latent" if latent else "retention_ctx",
    )(*args)


def _grid_rope_tables(t):
    quarter = RET_DK // 4
    freqs = ROPE_BASE ** (-jnp.arange(quarter, dtype=F32) / quarter)
    pos = jnp.arange(t)
    row = (pos // GRID_W).astype(F32)[:, None] * freqs[None, :]
    col = (pos % GRID_W).astype(F32)[:, None] * freqs[None, :]
    cos = jnp.concatenate([jnp.cos(row), jnp.cos(row), jnp.cos(col), jnp.cos(col)], axis=-1)
    sin = jnp.concatenate([-jnp.sin(row), jnp.sin(row), -jnp.sin(col), jnp.sin(col)], axis=-1)
    return cos, sin


def _merge_kernel(yg_ref, op_ref, os_ref, ga_ref, gb_ref, wga_ref, wgb_ref, wr_ref, out_ref, wga_b, wgb_b, wr_b,
                  *, n_prompt_tiles):
    i = pl.program_id(1)

    @pl.when(i == 0)
    def _():
        wga_b[...] = wga_ref[...].astype(BF16)
        wgb_b[...] = wgb_ref[...].astype(BF16)
        wr_b[...] = wr_ref[...].astype(BF16)

    def body(o_ref):
        yg = yg_ref[...]
        za = _dot(yg, wga_b[...])
        zb = _dot(yg, wgb_b[...])
        ob = _dot(o_ref[...], wr_b[...])
        out_a = za * _sigmoid(zb)
        ga = ga_ref[...].astype(F32)
        gb = gb_ref[...].astype(F32)
        out_ref[...] = (_sigmoid(ga) * out_a + _sigmoid(gb) * ob).astype(out_ref.dtype)

    @pl.when(i < n_prompt_tiles)
    def _():
        body(op_ref)

    @pl.when(i >= n_prompt_tiles)
    def _():
        body(os_ref)


def _merge_call(yg, o_p, o_s, proj, w_glu, w_ret_out):
    m, ks = yg.shape
    kr = o_p.shape[1]
    d = w_ret_out.shape[1]
    tm = 1024 if m % 1024 == 0 else m
    tn = 512 if d % 512 == 0 else d
    ga0 = (proj.shape[1] - 2 * d) // tn
    gb0 = (proj.shape[1] - d) // tn
    nb = d // tn
    npt = o_p.shape[0] // tm
    return pl.pallas_call(
        functools.partial(_merge_kernel, n_prompt_tiles=npt),
        grid=(nb, m // tm),
        in_specs=[pl.BlockSpec((tm, ks), lambda j, i: (i, 0)),
                  pl.BlockSpec((tm, kr), lambda j, i: (jnp.minimum(i, npt - 1), 0)),
                  pl.BlockSpec((tm, kr), lambda j, i: (jnp.maximum(i - npt, 0), 0)),
                  pl.BlockSpec((tm, tn), lambda j, i: (i, ga0 + j)),
                  pl.BlockSpec((tm, tn), lambda j, i: (i, gb0 + j)),
                  pl.BlockSpec((ks, tn), lambda j, i: (0, j)),
                  pl.BlockSpec((ks, tn), lambda j, i: (0, nb + j)),
                  pl.BlockSpec((kr, tn), lambda j, i: (0, j))],
        out_specs=pl.BlockSpec((tm, tn), lambda j, i: (i, j)),
        out_shape=jax.ShapeDtypeStruct((m, d), BF16),
        scratch_shapes=[pltpu.VMEM((ks, tn), BF16), pltpu.VMEM((ks, tn), BF16), pltpu.VMEM((kr, tn), BF16)],
        compiler_params=_params("arbitrary", "arbitrary"),
        name="merge",
    )(yg, o_p, o_s, proj, proj, w_glu, w_glu, w_ret_out)


def _outproj_kernel(mg_ref, xp_ref, xs_ref, w_ref, g1_ref, sc_ref, sh_ref, n2_ref, wr_ref,
                    x1_ref, h2_ref, aff_ref, *, n_prompt_tiles, n_experts):
    i = pl.program_id(0)
    upd = g1_ref[...] * _dot(mg_ref[...], w_ref[...])

    def body(x_ref):
        x1 = x_ref[...] + upd
        x1_ref[...] = x1
        y = x1 * lax.rsqrt(jnp.mean(x1 * x1, axis=-1, keepdims=True) + EPS)
        h2 = (y * n2_ref[...]) * (1.0 + sc_ref[...]) + sh_ref[...]
        h2_ref[...] = h2.astype(h2_ref.dtype)
        wr = wr_ref[...]
        w_hi = wr.astype(BF16)
        w_lo = (wr - w_hi.astype(F32)).astype(BF16)
        h_hi = h2.astype(BF16)
        h_lo = (h2 - h_hi.astype(F32)).astype(BF16)
        logits = _dot(h_hi, w_hi) + (_dot(h_hi, w_lo) + _dot(h_lo, w_hi))
        lane = lax.broadcasted_iota(jnp.int32, logits.shape, 1)
        logits = jnp.where(lane < n_experts, logits, -jnp.inf)
        e = jnp.exp(logits - jnp.max(logits, axis=-1, keepdims=True))
        aff_ref[...] = e / jnp.sum(e, axis=-1, keepdims=True)

    @pl.when(i < n_prompt_tiles)
    def _():
        body(xp_ref)

    @pl.when(i >= n_prompt_tiles)
    def _():
        body(xs_ref)


def _outproj_call(merged, xp, xs, w_out_b, gate1, scale2, shift2, norm2, w_router_pad, t_sample):
    m, d = merged.shape
    n_p = xp.shape[0]
    tm = 2 * ROW_TILE
    npt = n_p // tm
    tps = t_sample // tm
    row = functools.partial(_mod_row, n_prompt_tiles=npt, tiles_per_sample_seq=tps)
    mod = pl.BlockSpec((None, 1, d), lambda i: (row(i), 0, 0))
    return pl.pallas_call(
        functools.partial(_outproj_kernel, n_prompt_tiles=npt, n_experts=N_EXPERTS),
        grid=(m // tm,),
        in_specs=[pl.BlockSpec((tm, d), lambda i: (i, 0)),
                  pl.BlockSpec((tm, d), lambda i: (jnp.minimum(i, npt - 1), 0)),
                  pl.BlockSpec((tm, d), lambda i: (jnp.maximum(i - npt, 0), 0)),
                  pl.BlockSpec((d, d), lambda i: (0, 0)),
                  mod, mod, mod,
                  pl.BlockSpec((1, d), lambda i: (0, 0)),
                  pl.BlockSpec((d, LANES), lambda i: (0, 0))],
        out_specs=[pl.BlockSpec((tm, d), lambda i: (i, 0)),
                   pl.BlockSpec((tm, d), lambda i: (i, 0)),
                   pl.BlockSpec((tm, LANES), lambda i: (i, 0))],
        out_shape=[jax.ShapeDtypeStruct((m, d), F32),
                   jax.ShapeDtypeStruct((m, d), BF16),
                   jax.ShapeDtypeStruct((m, LANES), F32)],
        compiler_params=_params("arbitrary"),
        name="out_proj_router",
    )(merged, xp, xs, w_out_b, gate1, scale2, shift2, norm2, w_router_pad)


def _select_kernel(a_ref, slot_ref, *, cap):
    a = a_ref[...]
    r, t = a.shape

    def as_float(bits):
        return pltpu.bitcast(bits, F32)

    def bisect(_, carry):
        lo, hi = carry
        mid = lo + jnp.right_shift(hi - lo + 1, 1)
        cnt = jnp.sum(jnp.where(a >= as_float(mid), 1.0, 0.0), axis=-1, keepdims=True)
        ok = cnt >= cap
        return jnp.where(ok, mid, lo), jnp.where(ok, hi, mid - 1)

    lo0 = jnp.zeros((r, 1), jnp.int32)
    hi0 = jnp.full((r, 1), 0x3F800000, jnp.int32)
    thr, _ = lax.fori_loop(0, 31, bisect, (lo0, hi0))

    gt = jnp.where(a >= as_float(thr + 1), 1.0, 0.0)
    eq = jnp.where(a >= as_float(thr), 1.0, 0.0) - gt
    need = cap - jnp.sum(gt, axis=-1, keepdims=True)
    before = lax.broadcasted_iota(jnp.int32, (t, t), 0) < lax.broadcasted_iota(jnp.int32, (t, t), 1)
    tri = jnp.where(before, 1.0, 0.0).astype(BF16)
    eq_rank = _dot(eq.astype(BF16), tri)
    sel = gt + eq * jnp.where(eq_rank < need, 1.0, 0.0)
    pos = _dot(sel.astype(BF16), tri)
    slot_ref[...] = jnp.where(sel > 0.5, pos, -1.0).astype(jnp.int32)


def _select_call(aff_t, cap):
    r, t = aff_t.shape
    return pl.pallas_call(
        functools.partial(_select_kernel, cap=cap),
        grid=(1,),
        in_specs=[pl.BlockSpec((r, t), lambda i: (0, 0))],
        out_specs=pl.BlockSpec((r, t), lambda i: (0, 0)),
        out_shape=jax.ShapeDtypeStruct((r, t), jnp.int32),
        compiler_params=_params("arbitrary"),
        name="select",
    )(aff_t)


def _gather_kernel(slot_ref, aff_ref, h_ref, xs_ref, gate_ref, *, cap, group):
    e_total, t = slot_ref.shape
    h = h_ref[...]
    ci = lax.broadcasted_iota(jnp.int32, (cap, t), 0)
    for e0 in range(0, e_total, group):
        hots = []
        for e in range(e0, e0 + group):
            hit = ci == slot_ref[e:e + 1, :]
            hots.append(jnp.where(hit, 1.0, 0.0).astype(BF16))
            gate_ref[e] = jnp.sum(jnp.where(hit, aff_ref[e:e + 1, :], 0.0), axis=-1, keepdims=True)
        onehot = hots[0] if group == 1 else jnp.concatenate(hots, axis=0)
        xs = _dot(onehot, h).astype(xs_ref.dtype)
        xs_ref[e0:e0 + group] = xs.reshape(group, cap, xs.shape[-1])


def _gather_call(slot_t, aff_t, h2, *, row0, nseq, t, cap):
    e = slot_t.shape[1]
    d = h2.shape[1]
    rb0 = row0 // t
    group = max(1, min(e, 512 // cap))
    return pl.pallas_call(
        functools.partial(_gather_kernel, cap=cap, group=group),
        grid=(nseq,),
        in_specs=[pl.BlockSpec((None, e, t), lambda b: (b, 0, 0)),
                  pl.BlockSpec((None, e, t), lambda b: (b, 0, 0)),
                  pl.BlockSpec((t, d), lambda b: (rb0 + b, 0))],
        out_specs=[pl.BlockSpec((e, cap, d), lambda b: (0, b, 0)),
                   pl.BlockSpec((e, cap, 1), lambda b: (0, b, 0))],
        out_shape=[jax.ShapeDtypeStruct((e, nseq * cap, d), BF16),
                   jax.ShapeDtypeStruct((e, nseq * cap, 1), F32)],
        compiler_params=_params("arbitrary"),
        name="gather",
    )(slot_t, aff_t, h2)


def _ffn_kernel(xp_ref, xs_ref, gp_ref, gs_ref, wg_ref, wu_ref, wd_ref, yp_ref, ys_ref,
                accp, accs, wgb, wub, wdb, *, chunk):
    f = pl.program_id(1)
    wgb[...] = wg_ref[...].astype(BF16)
    wub[...] = wu_ref[...].astype(BF16)
    wdb[...] = wd_ref[...].astype(BF16)

    @pl.when(f == 0)
    def _():
        accp[...] = jnp.zeros_like(accp)
        accs[...] = jnp.zeros_like(accs)

    def part(x_ref, acc):
        m = x_ref.shape[0]
        mc = min(chunk, m)
        for m0 in range(0, m, mc):
            x = x_ref[m0:m0 + mc, :]
            hg = _dot(x, wgb[...])
            hu = _dot(x, wub[...])
            hid = (hg * _sigmoid(hg) * hu).astype(BF16)
            acc[m0:m0 + mc, :] += _dot(hid, wdb[...])

    part(xp_ref, accp)
    part(xs_ref, accs)

    @pl.when(f == pl.num_programs(1) - 1)
    def _():
        yp_ref[...] = (accp[...] * gp_ref[...]).astype(yp_ref.dtype)
        ys_ref[...] = (accs[...] * gs_ref[...]).astype(ys_ref.dtype)


def _ffn_call(xs_p, xs_s, gate_p, gate_s, w_gate, w_up, w_down):
    e, mp, d = xs_p.shape
    ms = xs_s.shape[1]
    ff = w_gate.shape[2]
    tf = 256 if ff % 256 == 0 else ff
    tok = lambda m, w: pl.BlockSpec((None, m, w), lambda ee, f: (ee, 0, 0))
    return pl.pallas_call(
        functools.partial(_ffn_kernel, chunk=512),
        grid=(e, ff // tf),
        in_specs=[tok(mp, d), tok(ms, d), tok(mp, 1), tok(ms, 1),
                  pl.BlockSpec((None, d, tf), lambda ee, f: (ee, 0, f)),
                  pl.BlockSpec((None, d, tf), lambda ee, f: (ee, 0, f)),
                  pl.BlockSpec((None, tf, d), lambda ee, f: (ee, f, 0))],
        out_specs=[pl.BlockSpec((None, mp, d), lambda ee, f: (ee, 0, 0), pipeline_mode=pl.Buffered(1)),
                   pl.BlockSpec((None, ms, d), lambda ee, f: (ee, 0, 0), pipeline_mode=pl.Buffered(1))],
        out_shape=[jax.ShapeDtypeStruct((e, mp, d), BF16), jax.ShapeDtypeStruct((e, ms, d), BF16)],
        scratch_shapes=[pltpu.VMEM((mp, d), F32), pltpu.VMEM((ms, d), F32),
                        pltpu.VMEM((d, tf), BF16), pltpu.VMEM((d, tf), BF16), pltpu.VMEM((tf, d), BF16)],
        compiler_params=_params("arbitrary", "arbitrary"),
        name="expert_ffn",
    )(xs_p, xs_s, gate_p, gate_s, w_gate, w_up, w_down)


def _scatter_kernel(slot_ref, y_ref, x1_ref, g2_ref, wn_ref, o_ref, *, cap):
    e_total = y_ref.shape[0]
    tm = slot_ref.shape[0]
    slot = slot_ref[...]
    lane = lax.broadcasted_iota(jnp.int32, (tm, LANES), 1)
    per_block = max(1, LANES // cap)
    blocks = []
    for b0 in range(0, e_total, per_block):
        acc = jnp.zeros((tm, LANES), F32)
        for j in range(per_block):
            s = slot[:, b0 + j:b0 + j + 1]
            key = jnp.where(s >= 0, s + j * cap, -1)
            acc = acc + jnp.where(lane == key, 1.0, 0.0)
        blocks.append(acc.astype(BF16))
    onehot = jnp.concatenate(blocks, axis=1)
    y = y_ref[...].reshape(e_total * cap, y_ref.shape[-1])
    moe = _dot(onehot, y)
    x2 = x1_ref[...] + g2_ref[...] * moe
    o_ref[...] = x2 * lax.rsqrt(jnp.mean(x2 * x2, axis=-1, keepdims=True) + EPS) * wn_ref[...]


def _scatter_call(slot, y, x1, gate2, final_norm, *, row0, nseq, t, cap, mod_row0, mod_per_seq):
    e, _, d = y.shape
    tm = ROW_TILE
    nt = t // tm
    rb0 = row0 // tm
    assert cap == LANES or LANES % cap == 0
    return pl.pallas_call(
        functools.partial(_scatter_kernel, cap=cap),
        grid=(nseq, nt),
        in_specs=[pl.BlockSpec((None, tm, e), lambda b, i: (b, i, 0)),
                  pl.BlockSpec((e, cap, d), lambda b, i: (0, b, 0)),
                  pl.BlockSpec((tm, d), lambda b, i: (rb0 + b * nt + i, 0)),
                  pl.BlockSpec((None, 1, d), lambda b, i: (mod_row0 + b * mod_per_seq, 0, 0)),
                  pl.BlockSpec((1, d), lambda b, i: (0, 0))],
        out_specs=pl.BlockSpec((tm, d), lambda b, i: (b * nt + i, 0)),
        out_shape=jax.ShapeDtypeStruct((nseq * t, d), F32),
        compiler_params=_params("arbitrary", "arbitrary"),
        name="scatter_final",
    )(slot, y, x1, gate2, final_norm)


def kernel(x_prompt, x_sample, state_s5_re, state_s5_im, state_ret, c, c_ctx, final_norm, w_ada, b_ada, norm1, norm2, w_in, s5_a_re, s5_a_im, s5_log_dt, s5_b_re, s5_b_im, s5_c_re, s5_c_im, s5_d, w_s5_glu, ret_decay_logit, ret_gn_w, w_ret_out, w_out, w_router, w_exp_gate, w_exp_up, w_exp_down):
    bp, tp, d = x_prompt.shape
    bs, ts, _ = x_sample.shape
    depth = w_ada.shape[0]
    n_p, n_s = bp * tp, bs * ts
    xp = x_prompt.reshape(n_p, d)
    xs = x_sample.reshape(n_s, d)

    mod_rows = 16
    cvec = jnp.zeros((mod_rows, d), F32).at[0].set(c_ctx).at[1:1 + bs].set(c)
    rope_tabs = _grid_rope_tables(ts)
    cap_p = CAPACITY_FACTOR * tp // N_EXPERTS
    cap_s = CAPACITY_FACTOR * ts // N_EXPERTS

    new_re, new_im, new_ret = [], [], []
    for l in range(depth):
        mods = _ada_call(cvec, w_ada[l], b_ada[l])
        shift1, scale1, gate1, shift2, scale2, gate2 = [m.reshape(mod_rows, 1, d) for m in jnp.split(mods, 6, axis=-1)]

        h = _norm_mod_call(xp, xs, scale1, shift1, norm1[l].reshape(1, d), ts)
        proj = _proj_call(h, w_in[l])

        m_cmp, ws_cmp, wy_cmp, a_step = _s5_chunk_weights(
            s5_a_re[l], s5_a_im[l], s5_log_dt[l], s5_b_re[l], s5_b_im[l], s5_c_re[l], s5_c_im[l], s5_d[l])
        h0_s = _state_to_lanes(state_s5_re[:, l].astype(F32), state_s5_im[:, l].astype(F32))
        s5_paths = ((0, bp, tp // S5_CHUNK, False), (n_p // S5_CHUNK, bs, ts // S5_CHUNK, True))
        yg, hfin_p = _s5_call(proj, m_cmp, ws_cmp, wy_cmp, a_step, h0_s, s5_paths)
        s5_re, s5_im = _lanes_to_state(hfin_p)
        new_re.append(s5_re)
        new_im.append(s5_im)

        log_gamma = jax.nn.log_sigmoid(ret_decay_logit[l].astype(F32))
        gn_w = ret_gn_w[l].reshape(1, -1).astype(F32)
        o_p, sfin = _ret_call(proj, log_gamma, gn_w, None, None, row0=0, nseq=bp, t=tp, latent=False, want_state=True)
        o_s, = _ret_call(proj, log_gamma, gn_w, rope_tabs, state_ret[:, l].astype(F32), row0=n_p, nseq=bs, t=ts,
                         latent=True, want_state=False)
        new_ret.append(sfin)
        merged = _merge_call(yg, o_p, o_s, proj, w_s5_glu[l], w_ret_out[l])
        w_router_pad = jnp.zeros((d, LANES), F32).at[:, :N_EXPERTS].set(w_router[l].astype(F32))
        x1, h2, aff = _outproj_call(merged, xp, xs, w_out[l].astype(BF16), gate1, scale2, shift2,
                                    norm2[l].reshape(1, d), w_router_pad, ts)

        aff = aff[:, :N_EXPERTS]
        aff_p = jnp.transpose(aff[:n_p].reshape(bp, tp, N_EXPERTS), (0, 2, 1))
        aff_s = jnp.transpose(aff[n_p:].reshape(bs, ts, N_EXPERTS), (0, 2, 1))
        slot_p = _select_call(aff_p.reshape(bp * N_EXPERTS, tp), cap_p).reshape(bp, N_EXPERTS, tp)
        slot_s = _select_call(aff_s.reshape(bs * N_EXPERTS, ts), cap_s).reshape(bs, N_EXPERTS, ts)
        xe_p, ge_p = _gather_call(slot_p, aff_p, h2, row0=0, nseq=bp, t=tp, cap=cap_p)
        xe_s, ge_s = _gather_call(slot_s, aff_s, h2, row0=n_p, nseq=bs, t=ts, cap=cap_s)
        ye_p, ye_s = _ffn_call(xe_p, xe_s, ge_p, ge_s, w_exp_gate[l], w_exp_up[l], w_exp_down[l])

        last = l == depth - 1
        wn = final_norm.reshape(1, d).astype(F32) if last else None
        assert last, "only the final layer applies the output norm in the scatter kernel"
        yp = _scatter_call(jnp.transpose(slot_p, (0, 2, 1)), ye_p, x1, gate2, wn, row0=0, nseq=bp, t=tp, cap=cap_p,
                           mod_row0=0, mod_per_seq=0)
        ysm = _scatter_call(jnp.transpose(slot_s, (0, 2, 1)), ye_s, x1, gate2, wn, row0=n_p, nseq=bs, t=ts, cap=cap_s,
                            mod_row0=1, mod_per_seq=1)

    y_prompt = yp.reshape(bp, tp, d)
    y_sample = ysm.reshape(bs, ts, d)
    return (y_prompt, y_sample, jnp.stack(new_re, axis=1), jnp.stack(new_im, axis=1), jnp.stack(new_ret, axis=1))
```

```python
import functools
import math

import jax
import jax.numpy as jnp
from jax import lax
from jax.experimental import pallas as pl
from jax.experimental.pallas import tpu as pltpu

F32 = jnp.float32
BF16 = jnp.bfloat16
HIGHEST = lax.Precision.HIGHEST

EPS = 1e-6
GRID_W = 64
S5_WIDTH = 1024
S5_GROUP = 16
S5_STATE = 64
RET_HEADS = 8
RET_DK = 128
RET_DV = 256
ROPE_BASE = 10000.0
N_EXPERTS = 16
CAPACITY_FACTOR = 2

LANES = 128
S5_CHUNK = 8
S5_GB = LANES // S5_GROUP
S5_NGB = S5_WIDTH // LANES
S5_SCOLS = 4 * S5_GB * S5_STATE
ROW_TILE = 256
VMEM_LIMIT = 56 * 1024 * 1024


def _params(*sem):
    return pltpu.CompilerParams(dimension_semantics=sem, vmem_limit_bytes=VMEM_LIMIT)


def _sigmoid(x):
    return 1.0 / (1.0 + jnp.exp(-x))


def _gelu_tanh(x):
    return 0.5 * x * (1.0 + jnp.tanh(math.sqrt(2.0 / math.pi) * (x + 0.044715 * (x * x * x))))


def _dot(a, b):
    return jnp.dot(a, b, preferred_element_type=F32)


def _ada_kernel(c_ref, w_ref, b_ref, o_ref):
    c = c_ref[...]
    s = c * _sigmoid(c)
    o_ref[...] = jnp.dot(s, w_ref[...], precision=HIGHEST, preferred_element_type=F32) + b_ref[...]


def _ada_call(cvec, w_ada, b_ada):
    r, d = cvec.shape
    n = w_ada.shape[1]
    tn = 1024 if n % 1024 == 0 else 512
    return pl.pallas_call(
        _ada_kernel,
        grid=(n // tn,),
        in_specs=[pl.BlockSpec((r, d), lambda j: (0, 0)),
                  pl.BlockSpec((d, tn), lambda j: (0, j)),
                  pl.BlockSpec((1, tn), lambda j: (0, j))],
        out_specs=pl.BlockSpec((r, tn), lambda j: (0, j)),
        out_shape=jax.ShapeDtypeStruct((r, n), F32),
        compiler_params=_params("arbitrary"),
        name="ada",
    )(cvec, w_ada, b_ada.reshape(1, n))


def _norm_mod_kernel(xp_ref, xs_ref, sc_ref, sh_ref, w_ref, o_ref, *, n_prompt_tiles):
    i = pl.program_id(0)

    def body(x_ref):
        x = x_ref[...]
        y = x * lax.rsqrt(jnp.mean(x * x, axis=-1, keepdims=True) + EPS)
        h = (y * w_ref[...]) * (1.0 + sc_ref[...]) + sh_ref[...]
        o_ref[...] = h.astype(o_ref.dtype)

    @pl.when(i < n_prompt_tiles)
    def _():
        body(xp_ref)

    @pl.when(i >= n_prompt_tiles)
    def _():
        body(xs_ref)


def _mod_row(i, n_prompt_tiles, tiles_per_sample_seq):
    return jnp.where(i < n_prompt_tiles, 0, 1 + (i - n_prompt_tiles) // tiles_per_sample_seq)


def _norm_mod_call(xp, xs, scale, shift, w, t_sample):
    n_p, d = xp.shape
    n_s = xs.shape[0]
    tm = ROW_TILE
    npt, nst = n_p // tm, n_s // tm
    tps = t_sample // tm
    row = functools.partial(_mod_row, n_prompt_tiles=npt, tiles_per_sample_seq=tps)
    return pl.pallas_call(
        functools.partial(_norm_mod_kernel, n_prompt_tiles=npt),
        grid=(npt + nst,),
        in_specs=[pl.BlockSpec((tm, d), lambda i: (jnp.minimum(i, npt - 1), 0)),
                  pl.BlockSpec((tm, d), lambda i: (jnp.maximum(i - npt, 0), 0)),
                  pl.BlockSpec((None, 1, d), lambda i: (row(i), 0, 0)),
                  pl.BlockSpec((None, 1, d), lambda i: (row(i), 0, 0)),
                  pl.BlockSpec((1, d), lambda i: (0, 0))],
        out_specs=pl.BlockSpec((tm, d), lambda i: (i, 0)),
        out_shape=jax.ShapeDtypeStruct((n_p + n_s, d), BF16),
        compiler_params=_params("arbitrary"),
        name="norm_mod",
    )(xp, xs, scale, shift, w)


def _proj_kernel(a_ref, w_ref, o_ref, wb_ref):
    @pl.when(pl.program_id(1) == 0)
    def _():
        wb_ref[...] = w_ref[...].astype(BF16)

    o_ref[...] = _dot(a_ref[...], wb_ref[...]).astype(o_ref.dtype)


def _proj_call(a, w, col0):
    m, k = a.shape
    n = w.shape[1] - col0
    tm = 1024 if m % 1024 == 0 else m
    tn = 1024 if n % 1024 == 0 else 512
    cb0 = col0 // tn
    return pl.pallas_call(
        _proj_kernel,
        grid=(n // tn, m // tm),
        in_specs=[pl.BlockSpec((tm, k), lambda j, i: (i, 0)),
                  pl.BlockSpec((k, tn), lambda j, i: (0, cb0 + j))],
        out_specs=pl.BlockSpec((tm, tn), lambda j, i: (i, j)),
        out_shape=jax.ShapeDtypeStruct((m, n), BF16),
        scratch_shapes=[pltpu.VMEM((k, tn), BF16)],
        compiler_params=_params("arbitrary", "arbitrary"),
        name="in_proj",
    )(a, w)


def _cmul(ar, ai, br, bi):
    return ar * br - ai * bi, ar * bi + ai * br


def _uproj_kernel(a_ref, w_ref, o_ref, wb_ref, r_scr):
    @pl.when(pl.program_id(0) == 0)
    def _():
        wb_ref[...] = w_ref[...].astype(BF16)

    res = _dot(a_ref[...], wb_ref[...])
    for g in range(S5_NGB):
        r_scr[g] = res[:, g * LANES:(g + 1) * LANES]
    rows = a_ref.shape[0] // S5_CHUNK
    for g in range(S5_NGB):
        for tau in range(S5_CHUNK):
            c0 = (g * S5_CHUNK + tau) * LANES
            o_ref[:, c0:c0 + LANES] = r_scr[g, pl.ds(tau, rows, stride=S5_CHUNK), :].astype(o_ref.dtype)


def _uproj_call(a, w):
    m, k = a.shape
    tm = 1024
    return pl.pallas_call(
        _uproj_kernel,
        grid=(m // tm,),
        in_specs=[pl.BlockSpec((tm, k), lambda i: (i, 0)),
                  pl.BlockSpec((k, S5_WIDTH), lambda i: (0, 0))],
        out_specs=pl.BlockSpec((tm // S5_CHUNK, S5_CHUNK * S5_WIDTH), lambda i: (i, 0)),
        out_shape=jax.ShapeDtypeStruct((m // S5_CHUNK, S5_CHUNK * S5_WIDTH), BF16),
        scratch_shapes=[pltpu.VMEM((k, S5_WIDTH), BF16), pltpu.VMEM((S5_NGB, tm, LANES), F32)],
        compiler_params=_params("arbitrary"),
        name="u_proj",
    )(a, w)


def _s5_chunk_weights(a_re, a_im, log_dt, b_re, b_im, c_re, c_im, d_skip):
    L, G, P, HG, GB, NGB = S5_CHUNK, S5_WIDTH // S5_GROUP, S5_STATE, S5_GROUP, S5_GB, S5_NGB
    a_re, a_im = a_re.astype(F32), a_im.astype(F32)
    dt = jnp.exp(log_dt.astype(F32))[..., None]
    adt_re, adt_im = a_re * dt, a_im * dt
    mag = jnp.exp(adt_re)
    lam_re, lam_im = mag * jnp.cos(adt_im), mag * jnp.sin(adt_im)
    den = a_re * a_re + a_im * a_im
    q_re = ((lam_re - 1.0) * a_re + lam_im * a_im) / den
    q_im = (lam_im * a_re - (lam_re - 1.0) * a_im) / den
    bb_re, bb_im = _cmul(q_re[..., None], q_im[..., None], b_re.astype(F32), b_im.astype(F32))
    n = jnp.arange(L + 1, dtype=F32)
    pmag = jnp.exp(adt_re[..., None] * n)
    pw_re, pw_im = pmag * jnp.cos(adt_im[..., None] * n), pmag * jnp.sin(adt_im[..., None] * n)

    def ws_exponents(pw):
        return jnp.transpose(jnp.stack([pw[0][..., :L][..., ::-1], pw[1][..., :L]]), (3, 1, 0, 2))

    bt_re, bt_im = jnp.transpose(bb_re, (1, 3, 0, 2)), jnp.transpose(bb_im, (1, 3, 0, 2))
    w_re, w_im = _cmul(ws_exponents(pw_re)[:, :, None], ws_exponents(pw_im)[:, :, None], bt_re[None], bt_im[None])
    ws = jnp.stack([w_re, w_im], axis=3).reshape(L, NGB, GB * HG, 4 * P)

    def wy_exponents(pw):
        return jnp.stack([pw[0], pw[1][..., ::-1]])

    ct_re = jnp.transpose(c_re.astype(F32), (0, 1, 3, 2))[:, :, :, None]
    ct_im = jnp.transpose(c_im.astype(F32), (0, 1, 3, 2))[:, :, :, None]
    y_re, y_im = _cmul(wy_exponents(pw_re)[..., None], wy_exponents(pw_im)[..., None], ct_re, ct_im)
    wy = jnp.stack([y_re, -y_im]).reshape(2, 2, NGB, GB * P, (L + 1) * HG)

    def lanes(x):
        return jnp.transpose(x.reshape(2, NGB, GB * P), (1, 0, 2)).reshape(NGB, 1, 2 * GB * P)

    a_step = jnp.concatenate([lanes(pw_re[..., L]), lanes(pw_im[..., L])], axis=1)
    return ws.astype(BF16), wy.astype(BF16), a_step, d_skip.astype(F32).reshape(NGB, 1, LANES)


def _s5_expand(xws_ref, zwy_ref, d_ref, wm, wws, wwy):
    L, hg, p = S5_CHUNK, S5_GROUP, S5_STATE
    lg_hg, lg_p, lg_lanes = hg.bit_length() - 1, p.bit_length() - 1, LANES.bit_length() - 1
    gmask = S5_GB - 1
    rblk = 512

    def expand(x, src_of_col, row_group, col_group, o_ref, cblk):
        rows, k = x.shape
        for c0 in range(0, o_ref.shape[1], cblk):
            kk = lax.broadcasted_iota(jnp.int32, (k, cblk), 0)
            cc = lax.broadcasted_iota(jnp.int32, (k, cblk), 1) + c0
            spread = jnp.where(kk == src_of_col(cc), 1.0, 0.0).astype(BF16)
            for r0 in range(0, rows, rblk):
                full = _dot(x[r0:r0 + rblk, :], spread)
                rr = lax.broadcasted_iota(jnp.int32, (rblk, cblk), 0) + r0
                c2 = lax.broadcasted_iota(jnp.int32, (rblk, cblk), 1) + c0
                keep = row_group(rr) == col_group(c2)
                o_ref[r0:r0 + rblk, c0:c0 + cblk] = jnp.where(keep, full, 0.0).astype(o_ref.dtype)

    lane_group = lambda i: jnp.right_shift(i, lg_hg) & gmask
    state_group = lambda i: jnp.right_shift(i, lg_p) & gmask
    chunk_src = lambda c: jnp.right_shift(c, lg_lanes) * hg + (c & (hg - 1))
    state_src = lambda c: jnp.right_shift(c, lg_p + 3) * p + (c & (p - 1))
    xws = xws_ref[...]
    expand(xws.reshape(L * LANES, xws.shape[-1]), state_src, lane_group, state_group, wws, 1024)
    zwy = zwy_ref[...]
    expand(zwy.reshape(S5_SCOLS, zwy.shape[-1]), chunk_src, state_group, lane_group, wwy, wwy.shape[1])

    h = S5_SCOLS // 4
    last, first = wws[(L - 1) * LANES:L * LANES, :], wws[0:LANES, :]
    pf = _dot(last[:, 0:h], wwy[0:h, :]) + _dot(last[:, 2 * h:3 * h], wwy[2 * h:3 * h, :])
    pb = _dot(first[:, h:2 * h], wwy[h:2 * h, :]) + _dot(first[:, 3 * h:4 * h], wwy[3 * h:4 * h, :])
    eye = lax.broadcasted_iota(jnp.int32, (LANES, LANES), 0) == lax.broadcasted_iota(jnp.int32, (LANES, LANES), 1)
    p0 = pf[:, 0:LANES] + pb[:, L * LANES:(L + 1) * LANES] + jnp.where(eye, d_ref[...], 0.0)
    for s in range(L):
        for t in range(L):
            if t > s:
                blk = pf[:, (t - s) * LANES:(t - s + 1) * LANES]
            elif t < s:
                blk = pb[:, (L - (s - t)) * LANES:(L - (s - t) + 1) * LANES]
            else:
                blk = p0
            wm[s * LANES:(s + 1) * LANES, t * LANES:(t + 1) * LANES] = blk.astype(wm.dtype)


def _s5_kernel(u_ref, xws_ref, zwy_ref, d_ref, a_ref, h0_ref, y_ref, hfin_ref,
               wm, wws, wwy, s_scr, hb_scr, y_scr, *, paths):
    nu = S5_CHUNK
    half = S5_SCOLS // 4
    nsl = half // LANES
    cw = 2 * LANES

    _s5_expand(xws_ref, zwy_ref, d_ref, wm, wws, wwy)
    a = a_ref[...]

    for row0, nseq, nchunk, has_h0 in paths:
        r = nseq * nchunk

        for q in range(4):
            sq = _dot(u_ref[row0:row0 + r, :], wws[:, q * half:(q + 1) * half])
            for k in range(nsl):
                for s in range(nseq):
                    s_scr[q * nsl + k, pl.ds(s, nchunk, stride=nseq), :] = (
                        sq[s * nchunk:(s + 1) * nchunk, k * LANES:(k + 1) * LANES])

        if has_h0:
            h0 = h0_ref[...]
            carry0 = tuple(h0[:, j * LANES:(j + 1) * LANES] for j in range(4 * nsl))
        else:
            carry0 = tuple(jnp.zeros((nseq, LANES), F32) for _ in range(4 * nsl))

        def step(i, carry, nseq=nseq, nchunk=nchunk):
            rf = pl.multiple_of(i * nseq, nseq)
            rb = pl.multiple_of((nchunk - 1 - i) * nseq, nseq)
            new = list(carry)
            for k in range(nsl):
                for re, im, rows, acol in ((k, 2 * nsl + k, rf, k), (nsl + k, 3 * nsl + k, rb, nsl + k)):
                    s_re = s_scr[re, pl.ds(rows, nseq), :]
                    s_im = s_scr[im, pl.ds(rows, nseq), :]
                    h_re, h_im = carry[re], carry[im]
                    s_scr[re, pl.ds(rows, nseq), :] = h_re
                    s_scr[im, pl.ds(rows, nseq), :] = h_im
                    a_re = a[0:1, acol * LANES:(acol + 1) * LANES]
                    a_im = a[1:2, acol * LANES:(acol + 1) * LANES]
                    new[re] = a_re * h_re - a_im * h_im + s_re
                    new[im] = a_re * h_im + a_im * h_re + s_im
            return tuple(new)

        fin = lax.fori_loop(0, nchunk, step, carry0)
        if not has_h0:
            for j in range(4 * nsl):
                hfin_ref[:, j * LANES:(j + 1) * LANES] = fin[j]

        for j in range(4 * nsl):
            for s in range(nseq):
                hb_scr[s * nchunk:(s + 1) * nchunk, j * LANES:(j + 1) * LANES] = (
                    s_scr[j, pl.ds(s, nchunk, stride=nseq), :].astype(BF16))

        for j in range(nu * LANES // cw):
            c0 = j * cw
            y = _dot(u_ref[row0:row0 + r, :], wm[:, c0:c0 + cw])
            for q in range(4):
                off = c0 + (LANES if q % 2 == 0 else 0)
                y = y + _dot(hb_scr[0:r, q * half:(q + 1) * half], wwy[q * half:(q + 1) * half, off:off + cw])
            g = _gelu_tanh(y)
            for t2 in range(cw // LANES):
                tau = j * (cw // LANES) + t2
                y_scr[pl.ds(row0 * nu + tau, r, stride=nu), :] = g[:, t2 * LANES:(t2 + 1) * LANES]

    y_ref[...] = y_scr[...].astype(y_ref.dtype)


def _s5_call(u_rows, ws_cmp, wy_cmp, a_step, d_rows, h0, paths):
    rows = u_rows.shape[0]
    n_tok = rows * S5_CHUNK
    sc, k = S5_SCOLS, S5_CHUNK * LANES
    rmax = max(p[1] * p[2] for p in paths)
    nseq_fin = [p[1] for p in paths if not p[3]][0]
    blk = lambda a, b: pl.BlockSpec((None, a, b), lambda g: (g, 0, 0))
    return pl.pallas_call(
        functools.partial(_s5_kernel, paths=paths),
        grid=(S5_NGB,),
        in_specs=[pl.BlockSpec((rows, k), lambda g: (0, g)),
                  pl.BlockSpec((S5_CHUNK, None, LANES, ws_cmp.shape[-1]), lambda g: (0, g, 0, 0)),
                  pl.BlockSpec((2, 2, None, sc // 4, wy_cmp.shape[-1]), lambda g: (0, 0, g, 0, 0)),
                  blk(1, LANES), blk(2, sc // 2), blk(h0.shape[1], sc)],
        out_specs=[pl.BlockSpec((n_tok, LANES), lambda g: (0, g)), blk(nseq_fin, sc)],
        out_shape=[jax.ShapeDtypeStruct((n_tok, S5_WIDTH), BF16),
                   jax.ShapeDtypeStruct((S5_NGB, nseq_fin, sc), F32)],
        scratch_shapes=[pltpu.VMEM((k, k), BF16), pltpu.VMEM((k, sc), BF16),
                        pltpu.VMEM((sc, (S5_CHUNK + 1) * LANES), BF16),
                        pltpu.VMEM((4 * (sc // 4 // LANES), rmax, LANES), F32), pltpu.VMEM((rmax, sc), BF16),
                        pltpu.VMEM((n_tok, LANES), F32)],
        compiler_params=_params("arbitrary"),
        name="s5",
    )(u_rows, ws_cmp, wy_cmp, d_rows, a_step, h0)


def _state_to_lanes(s_re, s_im):
    b = s_re.shape[0]

    def one(x):
        return jnp.transpose(x.reshape(b, 2, S5_NGB, S5_GB * S5_STATE), (2, 0, 1, 3)).reshape(S5_NGB, b, -1)

    return jnp.concatenate([one(s_re), one(s_im)], axis=-1)


def _lanes_to_state(h):
    b = h.shape[1]
    x = h.reshape(S5_NGB, b, 2, 2, S5_GB, S5_STATE)
    x = jnp.transpose(x, (2, 1, 3, 0, 4, 5)).reshape(2, b, 2, S5_NGB * S5_GB, S5_STATE)
    return x[0], x[1]


def _ret_kernel(lg_ref, q_ref, k_ref, v_ref, g_ref, gn_ref, *rest, t, latent, want_state):
    rest = list(rest)
    if latent:
        cos_ref, sin_ref, s0_ref = rest[:3]
        rest = rest[3:]
    o_ref = rest.pop(0)
    sfin_ref = rest.pop(0) if want_state else None
    dmat_ref = rest.pop(0)

    h = pl.program_id(0)
    lgf = lg_ref[0, h]
    lgb = lg_ref[1, h]

    @pl.when(pl.program_id(1) == 0)
    def _():
        ti = lax.broadcasted_iota(jnp.int32, (t, t), 0)
        si = lax.broadcasted_iota(jnp.int32, (t, t), 1)
        diff = (ti - si).astype(F32)
        dmat_ref[...] = jnp.exp(jnp.where(diff >= 0, lgf * diff, -lgb * diff))

    q = q_ref[...].astype(F32) * (RET_DK ** -0.5)
    k = k_ref[...].astype(F32)
    if latent:
        lane = lax.broadcasted_iota(jnp.int32, (t, RET_DK), 1)
        first = (lane % (RET_DK // 2)) < (RET_DK // 4)

        def rope(x):
            swapped = jnp.where(first, pltpu.roll(x, RET_DK - RET_DK // 4, 1), pltpu.roll(x, RET_DK // 4, 1))
            return x * cos_ref[...] + swapped * sin_ref[...]

        q = rope(q)
        k = rope(k)
    qb = q.astype(BF16)
    kb = k.astype(BF16)
    v = v_ref[...]

    scores = lax.dot_general(qb, kb, (((1,), (1,)), ((), ())), preferred_element_type=F32)
    o = _dot((scores * dmat_ref[...]).astype(BF16), v)

    pos = lax.broadcasted_iota(jnp.int32, (t, 1), 0).astype(F32)
    if latent:
        o = o + _dot(qb, s0_ref[0].astype(BF16)) * jnp.exp(lgf * (pos + 1.0))
        o = o + _dot(qb, s0_ref[1].astype(BF16)) * jnp.exp(lgb * (t - pos))
    if want_state:
        kf = (k * jnp.exp(lgf * (t - 1.0 - pos))).astype(BF16)
        kr = (k * jnp.exp(lgb * pos)).astype(BF16)
        tn = (((0,), (0,)), ((), ()))
        sfin_ref[0] = lax.dot_general(kf, v, tn, preferred_element_type=F32)
        sfin_ref[1] = lax.dot_general(kr, v, tn, preferred_element_type=F32)

    mu = jnp.mean(o, axis=-1, keepdims=True)
    d = o - mu
    var = jnp.mean(d * d, axis=-1, keepdims=True)
    on = d * lax.rsqrt(var + EPS) * gn_ref[...]
    g = g_ref[...].astype(F32)
    o_ref[...] = (g * _sigmoid(g) * on).astype(o_ref.dtype)


def _ret_call(proj, log_gamma, gn_w, rope_tabs, s0, *, row0, nseq, t, latent, want_state):
    h, dk, dv = RET_HEADS, RET_DK, RET_DV
    rb0 = row0 // t
    q0 = 0
    k0 = q0 + h
    v0 = (2 * h * dk) // dv
    g0 = v0 + h
    in_specs = [pl.BlockSpec(memory_space=pltpu.SMEM),
                pl.BlockSpec((t, dk), lambda hh, b: (rb0 + b, q0 + hh)),
                pl.BlockSpec((t, dk), lambda hh, b: (rb0 + b, k0 + hh)),
                pl.BlockSpec((t, dv), lambda hh, b: (rb0 + b, v0 + hh)),
                pl.BlockSpec((t, dv), lambda hh, b: (rb0 + b, g0 + hh)),
                pl.BlockSpec((1, dv), lambda hh, b: (0, hh))]
    args = [log_gamma, proj, proj, proj, proj, gn_w]
    if latent:
        in_specs += [pl.BlockSpec((t, dk), lambda hh, b: (0, 0)),
                     pl.BlockSpec((t, dk), lambda hh, b: (0, 0)),
                     pl.BlockSpec((None, 2, None, dk, dv), lambda hh, b: (b, 0, hh, 0, 0))]
        args += [rope_tabs[0], rope_tabs[1], s0]
    out_specs = [pl.BlockSpec((t, dv), lambda hh, b: (b, hh))]
    out_shape = [jax.ShapeDtypeStruct((nseq * t, h * dv), BF16)]
    if want_state:
        out_specs.append(pl.BlockSpec((None, 2, None, dk, dv), lambda hh, b: (b, 0, hh, 0, 0)))
        out_shape.append(jax.ShapeDtypeStruct((nseq, 2, h, dk, dv), F32))
    return pl.pallas_call(
        functools.partial(_ret_kernel, t=t, latent=latent, want_state=want_state),
        grid=(h, nseq),
        in_specs=in_specs,
        out_specs=out_specs,
        out_shape=out_shape,
        scratch_shapes=[pltpu.VMEM((t, t), F32)],
        compiler_params=_params("arbitrary", "arbitrary"),
        name="retention_latent" if latent else "retention_ctx",
    )(*args)


def _grid_rope_tables(t):
    quarter = RET_DK // 4
    freqs = ROPE_BASE ** (-jnp.arange(quarter, dtype=F32) / quarter)
    pos = jnp.arange(t)
    row = (pos // GRID_W).astype(F32)[:, None] * freqs[None, :]
    col = (pos % GRID_W).astype(F32)[:, None] * freqs[None, :]
    cos = jnp.concatenate([jnp.cos(row), jnp.cos(row), jnp.cos(col), jnp.cos(col)], axis=-1)
    sin = jnp.concatenate([-jnp.sin(row), jnp.sin(row), -jnp.sin(col), jnp.sin(col)], axis=-1)
    return cos, sin


def _merge_kernel(yg_ref, op_ref, os_ref, ga_ref, gb_ref, wga_ref, wgb_ref, wr_ref, out_ref, wga_b, wgb_b, wr_b,
                  *, n_prompt_tiles):
    i = pl.program_id(1)

    @pl.when(i == 0)
    def _():
        wga_b[...] = wga_ref[...].astype(BF16)
        wgb_b[...] = wgb_ref[...].astype(BF16)
        wr_b[...] = wr_ref[...].astype(BF16)

    def body(o_ref):
        yg = yg_ref[...]
        za = _dot(yg, wga_b[...])
        zb = _dot(yg, wgb_b[...])
        ob = _dot(o_ref[...], wr_b[...])
        out_a = za * _sigmoid(zb)
        ga = ga_ref[...].astype(F32)
        gb = gb_ref[...].astype(F32)
        out_ref[...] = (_sigmoid(ga) * out_a + _sigmoid(gb) * ob).astype(out_ref.dtype)

    @pl.when(i < n_prompt_tiles)
    def _():
        body(op_ref)

    @pl.when(i >= n_prompt_tiles)
    def _():
        body(os_ref)


def _merge_call(yg, o_p, o_s, proj, w_glu, w_ret_out):
    m, ks = yg.shape
    kr = o_p.shape[1]
    d = w_ret_out.shape[1]
    tm = 1024 if m % 1024 == 0 else m
    tn = 512 if d % 512 == 0 else d
    ga0 = (proj.shape[1] - 2 * d) // tn
    gb0 = (proj.shape[1] - d) // tn
    nb = d // tn
    npt = o_p.shape[0] // tm
    return pl.pallas_call(
        functools.partial(_merge_kernel, n_prompt_tiles=npt),
        grid=(nb, m // tm),
        in_specs=[pl.BlockSpec((tm, ks), lambda j, i: (i, 0)),
                  pl.BlockSpec((tm, kr), lambda j, i: (jnp.minimum(i, npt - 1), 0)),
                  pl.BlockSpec((tm, kr), lambda j, i: (jnp.maximum(i - npt, 0), 0)),
                  pl.BlockSpec((tm, tn), lambda j, i: (i, ga0 + j)),
                  pl.BlockSpec((tm, tn), lambda j, i: (i, gb0 + j)),
                  pl.BlockSpec((ks, tn), lambda j, i: (0, j)),
                  pl.BlockSpec((ks, tn), lambda j, i: (0, nb + j)),
                  pl.BlockSpec((kr, tn), lambda j, i: (0, j))],
        out_specs=pl.BlockSpec((tm, tn), lambda j, i: (i, j)),
        out_shape=jax.ShapeDtypeStruct((m, d), BF16),
        scratch_shapes=[pltpu.VMEM((ks, tn), BF16), pltpu.VMEM((ks, tn), BF16), pltpu.VMEM((kr, tn), BF16)],
        compiler_params=_params("arbitrary", "arbitrary"),
        name="merge",
    )(yg, o_p, o_s, proj, proj, w_glu, w_glu, w_ret_out)


def _outproj_kernel(mg_ref, xp_ref, xs_ref, w_ref, g1_ref, sc_ref, sh_ref, n2_ref, wr_ref,
                    x1_ref, h2_ref, aff_ref, *, n_prompt_tiles, n_experts):
    i = pl.program_id(0)
    upd = g1_ref[...] * _dot(mg_ref[...], w_ref[...])

    def body(x_ref):
        x1 = x_ref[...] + upd
        x1_ref[...] = x1
        y = x1 * lax.rsqrt(jnp.mean(x1 * x1, axis=-1, keepdims=True) + EPS)
        h2 = (y * n2_ref[...]) * (1.0 + sc_ref[...]) + sh_ref[...]
        h2_ref[...] = h2.astype(h2_ref.dtype)
        wr = wr_ref[...]
        w_hi = wr.astype(BF16)
        w_lo = (wr - w_hi.astype(F32)).astype(BF16)
        h_hi = h2.astype(BF16)
        h_lo = (h2 - h_hi.astype(F32)).astype(BF16)
        logits = _dot(h_hi, w_hi) + (_dot(h_hi, w_lo) + _dot(h_lo, w_hi))
        lane = lax.broadcasted_iota(jnp.int32, logits.shape, 1)
        logits = jnp.where(lane < n_experts, logits, -jnp.inf)
        e = jnp.exp(logits - jnp.max(logits, axis=-1, keepdims=True))
        aff_ref[...] = e / jnp.sum(e, axis=-1, keepdims=True)

    @pl.when(i < n_prompt_tiles)
    def _():
        body(xp_ref)

    @pl.when(i >= n_prompt_tiles)
    def _():
        body(xs_ref)


def _outproj_call(merged, xp, xs, w_out_b, gate1, scale2, shift2, norm2, w_router_pad, t_sample):
    m, d = merged.shape
    n_p = xp.shape[0]
    tm = 2 * ROW_TILE
    npt = n_p // tm
    tps = t_sample // tm
    row = functools.partial(_mod_row, n_prompt_tiles=npt, tiles_per_sample_seq=tps)
    mod = pl.BlockSpec((None, 1, d), lambda i: (row(i), 0, 0))
    return pl.pallas_call(
        functools.partial(_outproj_kernel, n_prompt_tiles=npt, n_experts=N_EXPERTS),
        grid=(m // tm,),
        in_specs=[pl.BlockSpec((tm, d), lambda i: (i, 0)),
                  pl.BlockSpec((tm, d), lambda i: (jnp.minimum(i, npt - 1), 0)),
                  pl.BlockSpec((tm, d), lambda i: (jnp.maximum(i - npt, 0), 0)),
                  pl.BlockSpec((d, d), lambda i: (0, 0)),
                  mod, mod, mod,
                  pl.BlockSpec((1, d), lambda i: (0, 0)),
                  pl.BlockSpec((d, LANES), lambda i: (0, 0))],
        out_specs=[pl.BlockSpec((tm, d), lambda i: (i, 0)),
                   pl.BlockSpec((tm, d), lambda i: (i, 0)),
                   pl.BlockSpec((tm, LANES), lambda i: (i, 0))],
        out_shape=[jax.ShapeDtypeStruct((m, d), F32),
                   jax.ShapeDtypeStruct((m, d), BF16),
                   jax.ShapeDtypeStruct((m, LANES), F32)],
        compiler_params=_params("arbitrary"),
        name="out_proj_router",
    )(merged, xp, xs, w_out_b, gate1, scale2, shift2, norm2, w_router_pad)


def _select_kernel(a_ref, slot_ref, *, cap):
    a = a_ref[...]
    r, t = a.shape

    def as_float(bits):
        return pltpu.bitcast(bits, F32)

    def bisect(_, carry):
        lo, hi = carry
        mid = lo + jnp.right_shift(hi - lo + 1, 1)
        cnt = jnp.sum(jnp.where(a >= as_float(mid), 1.0, 0.0), axis=-1, keepdims=True)
        ok = cnt >= cap
        return jnp.where(ok, mid, lo), jnp.where(ok, hi, mid - 1)

    lo0 = jnp.zeros((r, 1), jnp.int32)
    hi0 = jnp.full((r, 1), 0x3F800000, jnp.int32)
    thr, _ = lax.fori_loop(0, 31, bisect, (lo0, hi0))

    gt = jnp.where(a >= as_float(thr + 1), 1.0, 0.0)
    eq = jnp.where(a >= as_float(thr), 1.0, 0.0) - gt
    need = cap - jnp.sum(gt, axis=-1, keepdims=True)
    before = lax.broadcasted_iota(jnp.int32, (t, t), 0) < lax.broadcasted_iota(jnp.int32, (t, t), 1)
    tri = jnp.where(before, 1.0, 0.0).astype(BF16)
    eq_rank = _dot(eq.astype(BF16), tri)
    sel = gt + eq * jnp.where(eq_rank < need, 1.0, 0.0)
    pos = _dot(sel.astype(BF16), tri)
    slot_ref[...] = jnp.where(sel > 0.5, pos, -1.0).astype(jnp.int32)


def _select_call(aff_t, cap):
    r, t = aff_t.shape
    return pl.pallas_call(
        functools.partial(_select_kernel, cap=cap),
        grid=(1,),
        in_specs=[pl.BlockSpec((r, t), lambda i: (0, 0))],
        out_specs=pl.BlockSpec((r, t), lambda i: (0, 0)),
        out_shape=jax.ShapeDtypeStruct((r, t), jnp.int32),
        compiler_params=_params("arbitrary"),
        name="select",
    )(aff_t)


def _gather_kernel(slot_ref, aff_ref, h_ref, xs_ref, gate_ref, *, cap, group):
    e_total, t = slot_ref.shape
    h = h_ref[...]
    ci = lax.broadcasted_iota(jnp.int32, (cap, t), 0)
    for e0 in range(0, e_total, group):
        hots = []
        for e in range(e0, e0 + group):
            hit = ci == slot_ref[e:e + 1, :]
            hots.append(jnp.where(hit, 1.0, 0.0).astype(BF16))
            gate_ref[e] = jnp.sum(jnp.where(hit, aff_ref[e:e + 1, :], 0.0), axis=-1, keepdims=True)
        onehot = hots[0] if group == 1 else jnp.concatenate(hots, axis=0)
        xs = _dot(onehot, h).astype(xs_ref.dtype)
        xs_ref[e0:e0 + group] = xs.reshape(group, cap, xs.shape[-1])


def _gather_call(slot_t, aff_t, h2, *, row0, nseq, t, cap):
    e = slot_t.shape[1]
    d = h2.shape[1]
    rb0 = row0 // t
    group = max(1, min(e, 512 // cap))
    return pl.pallas_call(
        functools.partial(_gather_kernel, cap=cap, group=group),
        grid=(nseq,),
        in_specs=[pl.BlockSpec((None, e, t), lambda b: (b, 0, 0)),
                  pl.BlockSpec((None, e, t), lambda b: (b, 0, 0)),
                  pl.BlockSpec((t, d), lambda b: (rb0 + b, 0))],
        out_specs=[pl.BlockSpec((e, cap, d), lambda b: (0, b, 0)),
                   pl.BlockSpec((e, cap, 1), lambda b: (0, b, 0))],
        out_shape=[jax.ShapeDtypeStruct((e, nseq * cap, d), BF16),
                   jax.ShapeDtypeStruct((e, nseq * cap, 1), F32)],
        compiler_params=_params("arbitrary"),
        name="gather",
    )(slot_t, aff_t, h2)


def _ffn_kernel(xp_ref, xs_ref, gp_ref, gs_ref, wg_ref, wu_ref, wd_ref, yp_ref, ys_ref,
                accp, accs, wgb, wub, wdb, *, chunk):
    f = pl.program_id(1)
    wgb[...] = wg_ref[...].astype(BF16)
    wub[...] = wu_ref[...].astype(BF16)
    wdb[...] = wd_ref[...].astype(BF16)

    @pl.when(f == 0)
    def _():
        accp[...] = jnp.zeros_like(accp)
        accs[...] = jnp.zeros_like(accs)

    def part(x_ref, acc):
        m = x_ref.shape[0]
        mc = min(chunk, m)
        for m0 in range(0, m, mc):
            x = x_ref[m0:m0 + mc, :]
            hg = _dot(x, wgb[...])
            hu = _dot(x, wub[...])
            hid = (hg * _sigmoid(hg) * hu).astype(BF16)
            acc[m0:m0 + mc, :] += _dot(hid, wdb[...])

    part(xp_ref, accp)
    part(xs_ref, accs)

    @pl.when(f == pl.num_programs(1) - 1)
    def _():
        yp_ref[...] = (accp[...] * gp_ref[...]).astype(yp_ref.dtype)
        ys_ref[...] = (accs[...] * gs_ref[...]).astype(ys_ref.dtype)


def _ffn_call(xs_p, xs_s, gate_p, gate_s, w_gate, w_up, w_down):
    e, mp, d = xs_p.shape
    ms = xs_s.shape[1]
    ff = w_gate.shape[2]
    tf = 256 if ff % 256 == 0 else ff
    tok = lambda m, w: pl.BlockSpec((None, m, w), lambda ee, f: (ee, 0, 0))
    return pl.pallas_call(
        functools.partial(_ffn_kernel, chunk=512),
        grid=(e, ff // tf),
        in_specs=[tok(mp, d), tok(ms, d), tok(mp, 1), tok(ms, 1),
                  pl.BlockSpec((None, d, tf), lambda ee, f: (ee, 0, f)),
                  pl.BlockSpec((None, d, tf), lambda ee, f: (ee, 0, f)),
                  pl.BlockSpec((None, tf, d), lambda ee, f: (ee, f, 0))],
        out_specs=[pl.BlockSpec((None, mp, d), lambda ee, f: (ee, 0, 0), pipeline_mode=pl.Buffered(1)),
                   pl.BlockSpec((None, ms, d), lambda ee, f: (ee, 0, 0), pipeline_mode=pl.Buffered(1))],
        out_shape=[jax.ShapeDtypeStruct((e, mp, d), BF16), jax.ShapeDtypeStruct((e, ms, d), BF16)],
        scratch_shapes=[pltpu.VMEM((mp, d), F32), pltpu.VMEM((ms, d), F32),
                        pltpu.VMEM((d, tf), BF16), pltpu.VMEM((d, tf), BF16), pltpu.VMEM((tf, d), BF16)],
        compiler_params=_params("arbitrary", "arbitrary"),
        name="expert_ffn",
    )(xs_p, xs_s, gate_p, gate_s, w_gate, w_up, w_down)


def _scatter_kernel(slot_ref, y_ref, x1_ref, g2_ref, wn_ref, o_ref, *, cap):
    e_total = y_ref.shape[0]
    tm = slot_ref.shape[0]
    slot = slot_ref[...]
    lane = lax.broadcasted_iota(jnp.int32, (tm, LANES), 1)
    per_block = max(1, LANES // cap)
    blocks = []
    for b0 in range(0, e_total, per_block):
        acc = jnp.zeros((tm, LANES), F32)
        for j in range(per_block):
            s = slot[:, b0 + j:b0 + j + 1]
            key = jnp.where(s >= 0, s + j * cap, -1)
            acc = acc + jnp.where(lane == key, 1.0, 0.0)
        blocks.append(acc.astype(BF16))
    onehot = jnp.concatenate(blocks, axis=1)
    y = y_ref[...].reshape(e_total * cap, y_ref.shape[-1])
    moe = _dot(onehot, y)
    x2 = x1_ref[...] + g2_ref[...] * moe
    o_ref[...] = x2 * lax.rsqrt(jnp.mean(x2 * x2, axis=-1, keepdims=True) + EPS) * wn_ref[...]


def _scatter_call(slot, y, x1, gate2, final_norm, *, row0, nseq, t, cap, mod_row0, mod_per_seq):
    e, _, d = y.shape
    tm = ROW_TILE
    nt = t // tm
    rb0 = row0 // tm
    assert cap == LANES or LANES % cap == 0
    return pl.pallas_call(
        functools.partial(_scatter_kernel, cap=cap),
        grid=(nseq, nt),
        in_specs=[pl.BlockSpec((None, tm, e), lambda b, i: (b, i, 0)),
                  pl.BlockSpec((e, cap, d), lambda b, i: (0, b, 0)),
                  pl.BlockSpec((tm, d), lambda b, i: (rb0 + b * nt + i, 0)),
                  pl.BlockSpec((None, 1, d), lambda b, i: (mod_row0 + b * mod_per_seq, 0, 0)),
                  pl.BlockSpec((1, d), lambda b, i: (0, 0))],
        out_specs=pl.BlockSpec((tm, d), lambda b, i: (b * nt + i, 0)),
        out_shape=jax.ShapeDtypeStruct((nseq * t, d), F32),
        compiler_params=_params("arbitrary", "arbitrary"),
        name="scatter_final",
    )(slot, y, x1, gate2, final_norm)


def kernel(x_prompt, x_sample, state_s5_re, state_s5_im, state_ret, c, c_ctx, final_norm, w_ada, b_ada, norm1, norm2, w_in, s5_a_re, s5_a_im, s5_log_dt, s5_b_re, s5_b_im, s5_c_re, s5_c_im, s5_d, w_s5_glu, ret_decay_logit, ret_gn_w, w_ret_out, w_out, w_router, w_exp_gate, w_exp_up, w_exp_down):
    bp, tp, d = x_prompt.shape
    bs, ts, _ = x_sample.shape
    depth = w_ada.shape[0]
    n_p, n_s = bp * tp, bs * ts
    xp = x_prompt.reshape(n_p, d)
    xs = x_sample.reshape(n_s, d)

    mod_rows = 16
    cvec = jnp.zeros((mod_rows, d), F32).at[0].set(c_ctx).at[1:1 + bs].set(c)
    rope_tabs = _grid_rope_tables(ts)
    cap_p = CAPACITY_FACTOR * tp // N_EXPERTS
    cap_s = CAPACITY_FACTOR * ts // N_EXPERTS

    new_re, new_im, new_ret = [], [], []
    for l in range(depth):
        mods = _ada_call(cvec, w_ada[l], b_ada[l])
        shift1, scale1, gate1, shift2, scale2, gate2 = [m.reshape(mod_rows, 1, d) for m in jnp.split(mods, 6, axis=-1)]

        h = _norm_mod_call(xp, xs, scale1, shift1, norm1[l].reshape(1, d), ts)
        u_rows = _uproj_call(h, w_in[l])
        proj = _proj_call(h, w_in[l], S5_WIDTH)

        ws_cmp, wy_cmp, a_step, d_rows = _s5_chunk_weights(
            s5_a_re[l], s5_a_im[l], s5_log_dt[l], s5_b_re[l], s5_b_im[l], s5_c_re[l], s5_c_im[l], s5_d[l])
        h0_s = _state_to_lanes(state_s5_re[:, l].astype(F32), state_s5_im[:, l].astype(F32))
        s5_paths = ((0, bp, tp // S5_CHUNK, False), (n_p // S5_CHUNK, bs, ts // S5_CHUNK, True))
        yg, hfin_p = _s5_call(u_rows, ws_cmp, wy_cmp, a_step, d_rows, h0_s, s5_paths)
        s5_re, s5_im = _lanes_to_state(hfin_p)
        new_re.append(s5_re)
        new_im.append(s5_im)

        log_gamma = jax.nn.log_sigmoid(ret_decay_logit[l].astype(F32))
        gn_w = ret_gn_w[l].reshape(1, -1).astype(F32)
        o_p, sfin = _ret_call(proj, log_gamma, gn_w, None, None, row0=0, nseq=bp, t=tp, latent=False, want_state=True)
        o_s, = _ret_call(proj, log_gamma, gn_w, rope_tabs, state_ret[:, l].astype(F32), row0=n_p, nseq=bs, t=ts,
                         latent=True, want_state=False)
        new_ret.append(sfin)
        merged = _merge_call(yg, o_p, o_s, proj, w_s5_glu[l], w_ret_out[l])
        w_router_pad = jnp.zeros((d, LANES), F32).at[:, :N_EXPERTS].set(w_router[l].astype(F32))
        x1, h2, aff = _outproj_call(merged, xp, xs, w_out[l].astype(BF16), gate1, scale2, shift2,
                                    norm2[l].reshape(1, d), w_router_pad, ts)

        aff = aff[:, :N_EXPERTS]
        aff_p = jnp.transpose(aff[:n_p].reshape(bp, tp, N_EXPERTS), (0, 2, 1))
        aff_s = jnp.transpose(aff[n_p:].reshape(bs, ts, N_EXPERTS), (0, 2, 1))
        slot_p = _select_call(aff_p.reshape(bp * N_EXPERTS, tp), cap_p).reshape(bp, N_EXPERTS, tp)
        slot_s = _select_call(aff_s.reshape(bs * N_EXPERTS, ts), cap_s).reshape(bs, N_EXPERTS, ts)
        xe_p, ge_p = _gather_call(slot_p, aff_p, h2, row0=0, nseq=bp, t=tp, cap=cap_p)
        xe_s, ge_s = _gather_call(slot_s, aff_s, h2, row0=n_p, nseq=bs, t=ts, cap=cap_s)
        ye_p, ye_s = _ffn_call(xe_p, xe_s, ge_p, ge_s, w_exp_gate[l], w_exp_up[l], w_exp_down[l])

        last = l == depth - 1
        wn = final_norm.reshape(1, d).astype(F32) if last else None
        assert last, "only the final layer applies the output norm in the scatter kernel"
        yp = _scatter_call(jnp.transpose(slot_p, (0, 2, 1)), ye_p, x1, gate2, wn, row0=0, nseq=bp, t=tp, cap=cap_p,
                           mod_row0=0, mod_per_seq=0)
        ysm = _scatter_call(jnp.transpose(slot_s, (0, 2, 1)), ye_s, x1, gate2, wn, row0=n_p, nseq=bs, t=ts, cap=cap_s,
                            mod_row0=1, mod_per_seq=1)

    y_prompt = yp.reshape(bp, tp, d)
    y_sample = ysm.reshape(bs, ts, d)
    return (y_prompt, y_sample, jnp.stack(new_re, axis=1), jnp.stack(new_im, axis=1), jnp.stack(new_ret, axis=1))
```

```python
import functools
import math

import jax
import jax.numpy as jnp
from jax import lax
from jax.experimental import pallas as pl
from jax.experimental.pallas import tpu as pltpu

F32 = jnp.float32
BF16 = jnp.bfloat16
HIGHEST = lax.Precision.HIGHEST

EPS = 1e-6
GRID_W = 64
S5_WIDTH = 1024
S5_GROUP = 16
S5_STATE = 64
RET_HEADS = 8
RET_DK = 128
RET_DV = 256
ROPE_BASE = 10000.0
N_EXPERTS = 16
CAPACITY_FACTOR = 2

LANES = 128
S5_CHUNK = 8
S5_GB = LANES // S5_GROUP
S5_NGB = S5_WIDTH // LANES
S5_SCOLS = 4 * S5_GB * S5_STATE
ROW_TILE = 256
VMEM_LIMIT = 56 * 1024 * 1024


def _params(*sem):
    return pltpu.CompilerParams(dimension_semantics=sem, vmem_limit_bytes=VMEM_LIMIT)


def _sigmoid(x):
    return 1.0 / (1.0 + jnp.exp(-x))


def _gelu_tanh(x):
    return 0.5 * x * (1.0 + jnp.tanh(math.sqrt(2.0 / math.pi) * (x + 0.044715 * (x * x * x))))


def _dot(a, b):
    return jnp.dot(a, b, preferred_element_type=F32)


def _ada_kernel(c_ref, w_ref, b_ref, o_ref):
    c = c_ref[...]
    s = c * _sigmoid(c)
    o_ref[...] = jnp.dot(s, w_ref[...], precision=HIGHEST, preferred_element_type=F32) + b_ref[...]


def _ada_call(cvec, w_ada, b_ada):
    r, d = cvec.shape
    n = w_ada.shape[1]
    tn = 1024 if n % 1024 == 0 else 512
    return pl.pallas_call(
        _ada_kernel,
        grid=(n // tn,),
        in_specs=[pl.BlockSpec((r, d), lambda j: (0, 0)),
                  pl.BlockSpec((d, tn), lambda j: (0, j)),
                  pl.BlockSpec((1, tn), lambda j: (0, j))],
        out_specs=pl.BlockSpec((r, tn), lambda j: (0, j)),
        out_shape=jax.ShapeDtypeStruct((r, n), F32),
        compiler_params=_params("arbitrary"),
        name="ada",
    )(cvec, w_ada, b_ada.reshape(1, n))


def _norm_mod_kernel(xp_ref, xs_ref, sc_ref, sh_ref, w_ref, o_ref, *, n_prompt_tiles):
    i = pl.program_id(0)

    def body(x_ref):
        x = x_ref[...]
        y = x * lax.rsqrt(jnp.mean(x * x, axis=-1, keepdims=True) + EPS)
        h = (y * w_ref[...]) * (1.0 + sc_ref[...]) + sh_ref[...]
        o_ref[...] = h.astype(o_ref.dtype)

    @pl.when(i < n_prompt_tiles)
    def _():
        body(xp_ref)

    @pl.when(i >= n_prompt_tiles)
    def _():
        body(xs_ref)


def _mod_row(i, n_prompt_tiles, tiles_per_sample_seq):
    return jnp.where(i < n_prompt_tiles, 0, 1 + (i - n_prompt_tiles) // tiles_per_sample_seq)


def _norm_mod_call(xp, xs, scale, shift, w, t_sample):
    n_p, d = xp.shape
    n_s = xs.shape[0]
    tm = ROW_TILE
    npt, nst = n_p // tm, n_s // tm
    tps = t_sample // tm
    row = functools.partial(_mod_row, n_prompt_tiles=npt, tiles_per_sample_seq=tps)
    return pl.pallas_call(
        functools.partial(_norm_mod_kernel, n_prompt_tiles=npt),
        grid=(npt + nst,),
        in_specs=[pl.BlockSpec((tm, d), lambda i: (jnp.minimum(i, npt - 1), 0)),
                  pl.BlockSpec((tm, d), lambda i: (jnp.maximum(i - npt, 0), 0)),
                  pl.BlockSpec((None, 1, d), lambda i: (row(i), 0, 0)),
                  pl.BlockSpec((None, 1, d), lambda i: (row(i), 0, 0)),
                  pl.BlockSpec((1, d), lambda i: (0, 0))],
        out_specs=pl.BlockSpec((tm, d), lambda i: (i, 0)),
        out_shape=jax.ShapeDtypeStruct((n_p + n_s, d), BF16),
        compiler_params=_params("arbitrary"),
        name="norm_mod",
    )(xp, xs, scale, shift, w)


def _proj_kernel(a_ref, w_ref, o_ref, wb_ref):
    @pl.when(pl.program_id(1) == 0)
    def _():
        wb_ref[...] = w_ref[...].astype(BF16)

    o_ref[...] = _dot(a_ref[...], wb_ref[...]).astype(o_ref.dtype)


def _proj_call(a, w, col0):
    m, k = a.shape
    n = w.shape[1] - col0
    tm = 1024 if m % 1024 == 0 else m
    tn = 1024 if n % 1024 == 0 else 512
    cb0 = col0 // tn
    return pl.pallas_call(
        _proj_kernel,
        grid=(n // tn, m // tm),
        in_specs=[pl.BlockSpec((tm, k), lambda j, i: (i, 0)),
                  pl.BlockSpec((k, tn), lambda j, i: (0, cb0 + j))],
        out_specs=pl.BlockSpec((tm, tn), lambda j, i: (i, j)),
        out_shape=jax.ShapeDtypeStruct((m, n), BF16),
        scratch_shapes=[pltpu.VMEM((k, tn), BF16)],
        compiler_params=_params("arbitrary", "arbitrary"),
        name="in_proj",
    )(a, w)


def _cmul(ar, ai, br, bi):
    return ar * br - ai * bi, ar * bi + ai * br


def _uproj_kernel(a_ref, w_ref, o_ref, wb_ref, r_scr):
    @pl.when(pl.program_id(0) == 0)
    def _():
        wb_ref[...] = w_ref[...].astype(BF16)

    res = _dot(a_ref[...], wb_ref[...])
    for g in range(S5_NGB):
        r_scr[g] = res[:, g * LANES:(g + 1) * LANES]
    rows = a_ref.shape[0] // S5_CHUNK
    for g in range(S5_NGB):
        for tau in range(S5_CHUNK):
            c0 = (g * S5_CHUNK + tau) * LANES
            o_ref[:, c0:c0 + LANES] = r_scr[g, pl.ds(tau, rows, stride=S5_CHUNK), :].astype(o_ref.dtype)


def _uproj_call(a, w):
    m, k = a.shape
    tm = 1024
    return pl.pallas_call(
        _uproj_kernel,
        grid=(m // tm,),
        in_specs=[pl.BlockSpec((tm, k), lambda i: (i, 0)),
                  pl.BlockSpec((k, S5_WIDTH), lambda i: (0, 0))],
        out_specs=pl.BlockSpec((tm // S5_CHUNK, S5_CHUNK * S5_WIDTH), lambda i: (i, 0)),
        out_shape=jax.ShapeDtypeStruct((m // S5_CHUNK, S5_CHUNK * S5_WIDTH), BF16),
        scratch_shapes=[pltpu.VMEM((k, S5_WIDTH), BF16), pltpu.VMEM((S5_NGB, tm, LANES), F32)],
        compiler_params=_params("arbitrary"),
        name="u_proj",
    )(a, w)


def _s5_chunk_weights(a_re, a_im, log_dt, b_re, b_im, c_re, c_im, d_skip):
    L, G, P, HG, GB, NGB = S5_CHUNK, S5_WIDTH // S5_GROUP, S5_STATE, S5_GROUP, S5_GB, S5_NGB
    a_re, a_im = a_re.astype(F32), a_im.astype(F32)
    dt = jnp.exp(log_dt.astype(F32))[..., None]
    adt_re, adt_im = a_re * dt, a_im * dt
    mag = jnp.exp(adt_re)
    lam_re, lam_im = mag * jnp.cos(adt_im), mag * jnp.sin(adt_im)
    den = a_re * a_re + a_im * a_im
    q_re = ((lam_re - 1.0) * a_re + lam_im * a_im) / den
    q_im = (lam_im * a_re - (lam_re - 1.0) * a_im) / den
    bb_re, bb_im = _cmul(q_re[..., None], q_im[..., None], b_re.astype(F32), b_im.astype(F32))
    n = jnp.arange(L + 1, dtype=F32)
    pmag = jnp.exp(adt_re[..., None] * n)
    pw_re, pw_im = pmag * jnp.cos(adt_im[..., None] * n), pmag * jnp.sin(adt_im[..., None] * n)

    def ws_exponents(pw, d):
        e = pw[0][..., :L][..., ::-1] if d == 0 else pw[1][..., :L]
        return jnp.transpose(e, (2, 0, 1))[:, :, None]

    def ws_dir(d):
        bt_re, bt_im = jnp.transpose(bb_re[d], (0, 2, 1)), jnp.transpose(bb_im[d], (0, 2, 1))
        return _cmul(ws_exponents(pw_re, d), ws_exponents(pw_im, d), bt_re[None], bt_im[None])

    (wf_re, wf_im), (wb_re, wb_im) = ws_dir(0), ws_dir(1)
    ws = jnp.concatenate([wf_re, wb_re, wf_im, wb_im], axis=-1).reshape(L, NGB, GB * HG, 4 * P)

    def wy_exponents(pw):
        return jnp.stack([pw[0], pw[1][..., ::-1]])

    ct_re = jnp.transpose(c_re.astype(F32), (0, 1, 3, 2))[:, :, :, None]
    ct_im = jnp.transpose(c_im.astype(F32), (0, 1, 3, 2))[:, :, :, None]
    y_re, y_im = _cmul(wy_exponents(pw_re)[..., None], wy_exponents(pw_im)[..., None], ct_re, ct_im)
    wy = jnp.stack([y_re, -y_im]).reshape(2, 2, NGB, GB * P, (L + 1) * HG)

    def lanes(x):
        return jnp.transpose(x.reshape(2, NGB, GB * P), (1, 0, 2)).reshape(NGB, 1, 2 * GB * P)

    a_step = jnp.concatenate([lanes(pw_re[..., L]), lanes(pw_im[..., L])], axis=1)
    return ws.astype(BF16), wy.astype(BF16), a_step, d_skip.astype(F32).reshape(NGB, 1, LANES)


def _s5_expand(xws_ref, zwy_ref, d_ref, wm, wws, wwy):
    L, hg, p = S5_CHUNK, S5_GROUP, S5_STATE
    lg_hg, lg_p, lg_lanes = hg.bit_length() - 1, p.bit_length() - 1, LANES.bit_length() - 1
    gmask = S5_GB - 1
    rblk = 512

    def expand(x, src_of_col, row_group, col_group, o_ref, cblk):
        rows, k = x.shape
        for c0 in range(0, o_ref.shape[1], cblk):
            kk = lax.broadcasted_iota(jnp.int32, (k, cblk), 0)
            cc = lax.broadcasted_iota(jnp.int32, (k, cblk), 1) + c0
            spread = jnp.where(kk == src_of_col(cc), 1.0, 0.0).astype(BF16)
            rr = lax.broadcasted_iota(jnp.int32, (rblk, cblk), 0)
            c2 = lax.broadcasted_iota(jnp.int32, (rblk, cblk), 1) + c0
            keep = row_group(rr) == col_group(c2)
            for r0 in range(0, rows, rblk):
                full = _dot(x[r0:r0 + rblk, :], spread)
                o_ref[r0:r0 + rblk, c0:c0 + cblk] = jnp.where(keep, full, 0.0).astype(o_ref.dtype)

    lane_group = lambda i: jnp.right_shift(i, lg_hg) & gmask
    state_group = lambda i: jnp.right_shift(i, lg_p) & gmask
    chunk_src = lambda c: jnp.right_shift(c, lg_lanes) * hg + (c & (hg - 1))
    state_src = lambda c: jnp.right_shift(c, lg_p + 3) * p + (c & (p - 1))
    xws = xws_ref[...]
    expand(xws.reshape(L * LANES, xws.shape[-1]), state_src, lane_group, state_group, wws, 1024)
    zwy = zwy_ref[...]
    expand(zwy.reshape(S5_SCOLS, zwy.shape[-1]), chunk_src, state_group, lane_group, wwy, wwy.shape[1])

    h = S5_SCOLS // 4
    last, first = wws[(L - 1) * LANES:L * LANES, :], wws[0:LANES, :]
    pf = _dot(last[:, 0:h], wwy[0:h, :]) + _dot(last[:, 2 * h:3 * h], wwy[2 * h:3 * h, :])
    pb = _dot(first[:, h:2 * h], wwy[h:2 * h, :]) + _dot(first[:, 3 * h:4 * h], wwy[3 * h:4 * h, :])
    eye = lax.broadcasted_iota(jnp.int32, (LANES, LANES), 0) == lax.broadcasted_iota(jnp.int32, (LANES, LANES), 1)
    p0 = pf[:, 0:LANES] + pb[:, L * LANES:(L + 1) * LANES] + jnp.where(eye, d_ref[...], 0.0)
    for s in range(L):
        for t in range(L):
            if t > s:
                blk = pf[:, (t - s) * LANES:(t - s + 1) * LANES]
            elif t < s:
                blk = pb[:, (L - (s - t)) * LANES:(L - (s - t) + 1) * LANES]
            else:
                blk = p0
            wm[s * LANES:(s + 1) * LANES, t * LANES:(t + 1) * LANES] = blk.astype(wm.dtype)


def _s5_kernel(u_ref, xws_ref, zwy_ref, d_ref, a_ref, h0_ref, y_ref, hfin_ref,
               wm, wws, wwy, s_scr, hb_scr, y_scr, *, paths):
    nu = S5_CHUNK
    half = S5_SCOLS // 4
    nsl = half // LANES
    cw = 2 * LANES

    _s5_expand(xws_ref, zwy_ref, d_ref, wm, wws, wwy)
    a = a_ref[...]

    for row0, nseq, nchunk, has_h0 in paths:
        r = nseq * nchunk

        for q in range(4):
            sq = _dot(u_ref[row0:row0 + r, :], wws[:, q * half:(q + 1) * half])
            for k in range(nsl):
                for s in range(nseq):
                    s_scr[q * nsl + k, pl.ds(s, nchunk, stride=nseq), :] = (
                        sq[s * nchunk:(s + 1) * nchunk, k * LANES:(k + 1) * LANES])

        if has_h0:
            h0 = h0_ref[...]
            carry0 = tuple(h0[:, j * LANES:(j + 1) * LANES] for j in range(4 * nsl))
        else:
            carry0 = tuple(jnp.zeros((nseq, LANES), F32) for _ in range(4 * nsl))

        def step(i, carry, nseq=nseq, nchunk=nchunk):
            rf = pl.multiple_of(i * nseq, nseq)
            rb = pl.multiple_of((nchunk - 1 - i) * nseq, nseq)
            new = list(carry)
            for k in range(nsl):
                for re, im, rows, acol in ((k, 2 * nsl + k, rf, k), (nsl + k, 3 * nsl + k, rb, nsl + k)):
                    s_re = s_scr[re, pl.ds(rows, nseq), :]
                    s_im = s_scr[im, pl.ds(rows, nseq), :]
                    h_re, h_im = carry[re], carry[im]
                    s_scr[re, pl.ds(rows, nseq), :] = h_re
                    s_scr[im, pl.ds(rows, nseq), :] = h_im
                    a_re = a[0:1, acol * LANES:(acol + 1) * LANES]
                    a_im = a[1:2, acol * LANES:(acol + 1) * LANES]
                    new[re] = a_re * h_re - a_im * h_im + s_re
                    new[im] = a_re * h_im + a_im * h_re + s_im
            return tuple(new)

        fin = lax.fori_loop(0, nchunk, step, carry0)
        if not has_h0:
            for j in range(4 * nsl):
                hfin_ref[:, j * LANES:(j + 1) * LANES] = fin[j]

        for j in range(4 * nsl):
            for s in range(nseq):
                hb_scr[s * nchunk:(s + 1) * nchunk, j * LANES:(j + 1) * LANES] = (
                    s_scr[j, pl.ds(s, nchunk, stride=nseq), :].astype(BF16))

        for j in range(nu * LANES // cw):
            c0 = j * cw
            y = _dot(u_ref[row0:row0 + r, :], wm[:, c0:c0 + cw])
            for q in range(4):
                off = c0 + (LANES if q % 2 == 0 else 0)
                y = y + _dot(hb_scr[0:r, q * half:(q + 1) * half], wwy[q * half:(q + 1) * half, off:off + cw])
            g = _gelu_tanh(y)
            for t2 in range(cw // LANES):
                tau = j * (cw // LANES) + t2
                y_scr[pl.ds(row0 * nu + tau, r, stride=nu), :] = g[:, t2 * LANES:(t2 + 1) * LANES]

    y_ref[...] = y_scr[...].astype(y_ref.dtype)


def _s5_call(u_rows, ws_cmp, wy_cmp, a_step, d_rows, h0, paths):
    rows = u_rows.shape[0]
    n_tok = rows * S5_CHUNK
    sc, k = S5_SCOLS, S5_CHUNK * LANES
    rmax = max(p[1] * p[2] for p in paths)
    nseq_fin = [p[1] for p in paths if not p[3]][0]
    blk = lambda a, b: pl.BlockSpec((None, a, b), lambda g: (g, 0, 0))
    return pl.pallas_call(
        functools.partial(_s5_kernel, paths=paths),
        grid=(S5_NGB,),
        in_specs=[pl.BlockSpec((rows, k), lambda g: (0, g)),
                  pl.BlockSpec((S5_CHUNK, None, LANES, ws_cmp.shape[-1]), lambda g: (0, g, 0, 0)),
                  pl.BlockSpec((2, 2, None, sc // 4, wy_cmp.shape[-1]), lambda g: (0, 0, g, 0, 0)),
                  blk(1, LANES), blk(2, sc // 2), blk(h0.shape[1], sc)],
        out_specs=[pl.BlockSpec((n_tok, LANES), lambda g: (0, g)), blk(nseq_fin, sc)],
        out_shape=[jax.ShapeDtypeStruct((n_tok, S5_WIDTH), BF16),
                   jax.ShapeDtypeStruct((S5_NGB, nseq_fin, sc), F32)],
        scratch_shapes=[pltpu.VMEM((k, k), BF16), pltpu.VMEM((k, sc), BF16),
                        pltpu.VMEM((sc, (S5_CHUNK + 1) * LANES), BF16),
                        pltpu.VMEM((4 * (sc // 4 // LANES), rmax, LANES), F32), pltpu.VMEM((rmax, sc), BF16),
                        pltpu.VMEM((n_tok, LANES), F32)],
        compiler_params=_params("arbitrary"),
        name="s5",
    )(u_rows, ws_cmp, wy_cmp, d_rows, a_step, h0)


def _state_to_lanes(s_re, s_im):
    b = s_re.shape[0]

    def one(x):
        return jnp.transpose(x.reshape(b, 2, S5_NGB, S5_GB * S5_STATE), (2, 0, 1, 3)).reshape(S5_NGB, b, -1)

    return jnp.concatenate([one(s_re), one(s_im)], axis=-1)


def _lanes_to_state(h):
    b = h.shape[1]
    x = h.reshape(S5_NGB, b, 2, 2, S5_GB, S5_STATE)
    x = jnp.transpose(x, (2, 1, 3, 0, 4, 5)).reshape(2, b, 2, S5_NGB * S5_GB, S5_STATE)
    return x[0], x[1]


def _ret_kernel(lg_ref, q_ref, k_ref, v_ref, g_ref, gn_ref, *rest, t, group, latent, want_state):
    rest = list(rest)
    if latent:
        cos_ref, sin_ref, s0_ref = rest[:3]
        rest = rest[3:]
    o_ref = rest.pop(0)
    sfin_ref = rest.pop(0) if want_state else None
    dmat_ref = rest.pop(0)

    h = pl.program_id(0)
    lgf = lg_ref[0, h]
    lgb = lg_ref[1, h]

    @pl.when(pl.program_id(1) == 0)
    def _():
        ti = lax.broadcasted_iota(jnp.int32, (t, t), 0)
        si = lax.broadcasted_iota(jnp.int32, (t, t), 1)
        diff = (ti - si).astype(F32)
        dmat_ref[...] = jnp.exp(jnp.where(diff >= 0, lgf * diff, -lgb * diff))

    pos = lax.broadcasted_iota(jnp.int32, (t, 1), 0).astype(F32)
    if latent:
        lane = lax.broadcasted_iota(jnp.int32, (t, RET_DK), 1)
        first = (lane % (RET_DK // 2)) < (RET_DK // 4)

        def rope(x):
            swapped = jnp.where(first, pltpu.roll(x, RET_DK - RET_DK // 4, 1), pltpu.roll(x, RET_DK // 4, 1))
            return x * cos_ref[...] + swapped * sin_ref[...]

    for j in range(group):
        rows = slice(j * t, (j + 1) * t)
        q = q_ref[rows, :].astype(F32) * (RET_DK ** -0.5)
        k = k_ref[rows, :].astype(F32)
        if latent:
            q = rope(q)
            k = rope(k)
        qb = q.astype(BF16)
        kb = k.astype(BF16)
        v = v_ref[rows, :]

        scores = lax.dot_general(qb, kb, (((1,), (1,)), ((), ())), preferred_element_type=F32)
        o = _dot((scores * dmat_ref[...]).astype(BF16), v)

        if latent:
            o = o + _dot(qb, s0_ref[j, 0].astype(BF16)) * jnp.exp(lgf * (pos + 1.0))
            o = o + _dot(qb, s0_ref[j, 1].astype(BF16)) * jnp.exp(lgb * (t - pos))
        if want_state:
            kf = (k * jnp.exp(lgf * (t - 1.0 - pos))).astype(BF16)
            kr = (k * jnp.exp(lgb * pos)).astype(BF16)
            tn = (((0,), (0,)), ((), ()))
            sfin_ref[j, 0] = lax.dot_general(kf, v, tn, preferred_element_type=F32)
            sfin_ref[j, 1] = lax.dot_general(kr, v, tn, preferred_element_type=F32)

        mu = jnp.mean(o, axis=-1, keepdims=True)
        d = o - mu
        var = jnp.mean(d * d, axis=-1, keepdims=True)
        on = d * lax.rsqrt(var + EPS) * gn_ref[...]
        g = g_ref[rows, :].astype(F32)
        o_ref[rows, :] = (g * _sigmoid(g) * on).astype(o_ref.dtype)


def _ret_call(proj, log_gamma, gn_w, rope_tabs, s0, *, row0, nseq, t, group, latent, want_state):
    h, dk, dv = RET_HEADS, RET_DK, RET_DV
    gt = group * t
    rb0 = row0 // gt
    q0 = 0
    k0 = q0 + h
    v0 = (2 * h * dk) // dv
    g0 = v0 + h
    in_specs = [pl.BlockSpec(memory_space=pltpu.SMEM),
                pl.BlockSpec((gt, dk), lambda hh, b: (rb0 + b, q0 + hh)),
                pl.BlockSpec((gt, dk), lambda hh, b: (rb0 + b, k0 + hh)),
                pl.BlockSpec((gt, dv), lambda hh, b: (rb0 + b, v0 + hh)),
                pl.BlockSpec((gt, dv), lambda hh, b: (rb0 + b, g0 + hh)),
                pl.BlockSpec((1, dv), lambda hh, b: (0, hh))]
    args = [log_gamma, proj, proj, proj, proj, gn_w]
    if latent:
        in_specs += [pl.BlockSpec((t, dk), lambda hh, b: (0, 0)),
                     pl.BlockSpec((t, dk), lambda hh, b: (0, 0)),
                     pl.BlockSpec((group, 2, None, dk, dv), lambda hh, b: (b, 0, hh, 0, 0))]
        args += [rope_tabs[0], rope_tabs[1], s0]
    out_specs = [pl.BlockSpec((gt, dv), lambda hh, b: (b, hh))]
    out_shape = [jax.ShapeDtypeStruct((nseq * t, h * dv), BF16)]
    if want_state:
        out_specs.append(pl.BlockSpec((group, 2, None, dk, dv), lambda hh, b: (b, 0, hh, 0, 0)))
        out_shape.append(jax.ShapeDtypeStruct((nseq, 2, h, dk, dv), F32))
    return pl.pallas_call(
        functools.partial(_ret_kernel, t=t, group=group, latent=latent, want_state=want_state),
        grid=(h, nseq // group),
        in_specs=in_specs,
        out_specs=out_specs,
        out_shape=out_shape,
        scratch_shapes=[pltpu.VMEM((t, t), F32)],
        compiler_params=_params("arbitrary", "arbitrary"),
        name="retention_latent" if latent else "retention_ctx",
    )(*args)


def _grid_rope_tables(t):
    quarter = RET_DK // 4
    freqs = ROPE_BASE ** (-jnp.arange(quarter, dtype=F32) / quarter)
    pos = jnp.arange(t)
    row = (pos // GRID_W).astype(F32)[:, None] * freqs[None, :]
    col = (pos % GRID_W).astype(F32)[:, None] * freqs[None, :]
    cos = jnp.concatenate([jnp.cos(row), jnp.cos(row), jnp.cos(col), jnp.cos(col)], axis=-1)
    sin = jnp.concatenate([-jnp.sin(row), jnp.sin(row), -jnp.sin(col), jnp.sin(col)], axis=-1)
    return cos, sin


def _merge_kernel(yg_ref, op_ref, os_ref, ga_ref, gb_ref, wga_ref, wgb_ref, wr_ref, out_ref, wga_b, wgb_b, wr_b,
                  *, n_prompt_tiles):
    i = pl.program_id(1)

    @pl.when(i == 0)
    def _():
        wga_b[...] = wga_ref[...].astype(BF16)
        wgb_b[...] = wgb_ref[...].astype(BF16)
        wr_b[...] = wr_ref[...].astype(BF16)

    def body(o_ref):
        yg = yg_ref[...]
        za = _dot(yg, wga_b[...])
        zb = _dot(yg, wgb_b[...])
        ob = _dot(o_ref[...], wr_b[...])
        out_a = za * _sigmoid(zb)
        ga = ga_ref[...].astype(F32)
        gb = gb_ref[...].astype(F32)
        out_ref[...] = (_sigmoid(ga) * out_a + _sigmoid(gb) * ob).astype(out_ref.dtype)

    @pl.when(i < n_prompt_tiles)
    def _():
        body(op_ref)

    @pl.when(i >= n_prompt_tiles)
    def _():
        body(os_ref)


def _merge_call(yg, o_p, o_s, proj, w_glu, w_ret_out):
    m, ks = yg.shape
    kr = o_p.shape[1]
    d = w_ret_out.shape[1]
    tm = 1024 if m % 1024 == 0 else m
    tn = 512 if d % 512 == 0 else d
    ga0 = (proj.shape[1] - 2 * d) // tn
    gb0 = (proj.shape[1] - d) // tn
    nb = d // tn
    npt = o_p.shape[0] // tm
    return pl.pallas_call(
        functools.partial(_merge_kernel, n_prompt_tiles=npt),
        grid=(nb, m // tm),
        in_specs=[pl.BlockSpec((tm, ks), lambda j, i: (i, 0)),
                  pl.BlockSpec((tm, kr), lambda j, i: (jnp.minimum(i, npt - 1), 0)),
                  pl.BlockSpec((tm, kr), lambda j, i: (jnp.maximum(i - npt, 0), 0)),
                  pl.BlockSpec((tm, tn), lambda j, i: (i, ga0 + j)),
                  pl.BlockSpec((tm, tn), lambda j, i: (i, gb0 + j)),
                  pl.BlockSpec((ks, tn), lambda j, i: (0, j)),
                  pl.BlockSpec((ks, tn), lambda j, i: (0, nb + j)),
                  pl.BlockSpec((kr, tn), lambda j, i: (0, j))],
        out_specs=pl.BlockSpec((tm, tn), lambda j, i: (i, j)),
        out_shape=jax.ShapeDtypeStruct((m, d), BF16),
        scratch_shapes=[pltpu.VMEM((ks, tn), BF16), pltpu.VMEM((ks, tn), BF16), pltpu.VMEM((kr, tn), BF16)],
        compiler_params=_params("arbitrary", "arbitrary"),
        name="merge",
    )(yg, o_p, o_s, proj, proj, w_glu, w_glu, w_ret_out)


def _outproj_kernel(mg_ref, xp_ref, xs_ref, w_ref, g1_ref, sc_ref, sh_ref, n2_ref, wr_ref,
                    x1_ref, h2_ref, aff_ref, *, n_prompt_tiles, n_experts):
    i = pl.program_id(0)
    upd = g1_ref[...] * _dot(mg_ref[...], w_ref[...])

    def body(x_ref):
        x1 = x_ref[...] + upd
        x1_ref[...] = x1
        y = x1 * lax.rsqrt(jnp.mean(x1 * x1, axis=-1, keepdims=True) + EPS)
        h2 = (y * n2_ref[...]) * (1.0 + sc_ref[...]) + sh_ref[...]
        h2_ref[...] = h2.astype(h2_ref.dtype)
        wr = wr_ref[...]
        w_hi = wr.astype(BF16)
        w_lo = (wr - w_hi.astype(F32)).astype(BF16)
        h_hi = h2.astype(BF16)
        h_lo = (h2 - h_hi.astype(F32)).astype(BF16)
        hw = _dot(h_hi, jnp.concatenate([w_hi, w_lo], axis=1))
        logits = hw[:, :LANES] + (hw[:, LANES:] + _dot(h_lo, w_hi))
        lane = lax.broadcasted_iota(jnp.int32, logits.shape, 1)
        logits = jnp.where(lane < n_experts, logits, -jnp.inf)
        e = jnp.exp(logits - jnp.max(logits, axis=-1, keepdims=True))
        aff_ref[...] = e / jnp.sum(e, axis=-1, keepdims=True)

    @pl.when(i < n_prompt_tiles)
    def _():
        body(xp_ref)

    @pl.when(i >= n_prompt_tiles)
    def _():
        body(xs_ref)


def _outproj_call(merged, xp, xs, w_out_b, gate1, scale2, shift2, norm2, w_router_pad, t_sample):
    m, d = merged.shape
    n_p = xp.shape[0]
    tm = 2 * ROW_TILE
    npt = n_p // tm
    tps = t_sample // tm
    row = functools.partial(_mod_row, n_prompt_tiles=npt, tiles_per_sample_seq=tps)
    mod = pl.BlockSpec((None, 1, d), lambda i: (row(i), 0, 0))
    return pl.pallas_call(
        functools.partial(_outproj_kernel, n_prompt_tiles=npt, n_experts=N_EXPERTS),
        grid=(m // tm,),
        in_specs=[pl.BlockSpec((tm, d), lambda i: (i, 0)),
                  pl.BlockSpec((tm, d), lambda i: (jnp.minimum(i, npt - 1), 0)),
                  pl.BlockSpec((tm, d), lambda i: (jnp.maximum(i - npt, 0), 0)),
                  pl.BlockSpec((d, d), lambda i: (0, 0)),
                  mod, mod, mod,
                  pl.BlockSpec((1, d), lambda i: (0, 0)),
                  pl.BlockSpec((d, LANES), lambda i: (0, 0))],
        out_specs=[pl.BlockSpec((tm, d), lambda i: (i, 0)),
                   pl.BlockSpec((tm, d), lambda i: (i, 0)),
                   pl.BlockSpec((tm, LANES), lambda i: (i, 0))],
        out_shape=[jax.ShapeDtypeStruct((m, d), F32),
                   jax.ShapeDtypeStruct((m, d), BF16),
                   jax.ShapeDtypeStruct((m, LANES), F32)],
        compiler_params=_params("arbitrary"),
        name="out_proj_router",
    )(merged, xp, xs, w_out_b, gate1, scale2, shift2, norm2, w_router_pad)


def _select_kernel(a_ref, slot_ref, *, cap):
    a = a_ref[...]
    r, t = a.shape

    def as_float(bits):
        return pltpu.bitcast(bits, F32)

    def bisect(_, carry):
        lo, hi = carry
        mid = lo + jnp.right_shift(hi - lo + 1, 1)
        cnt = jnp.sum(jnp.where(a >= as_float(mid), 1.0, 0.0), axis=-1, keepdims=True)
        ok = cnt >= cap
        return jnp.where(ok, mid, lo), jnp.where(ok, hi, mid - 1)

    lo0 = jnp.zeros((r, 1), jnp.int32)
    hi0 = jnp.full((r, 1), 0x3F800000, jnp.int32)
    thr, _ = lax.fori_loop(0, 31, bisect, (lo0, hi0))

    gt = jnp.where(a >= as_float(thr + 1), 1.0, 0.0)
    eq = jnp.where(a >= as_float(thr), 1.0, 0.0) - gt
    need = cap - jnp.sum(gt, axis=-1, keepdims=True)
    before = lax.broadcasted_iota(jnp.int32, (t, t), 0) < lax.broadcasted_iota(jnp.int32, (t, t), 1)
    tri = jnp.where(before, 1.0, 0.0).astype(BF16)
    eq_rank = _dot(eq.astype(BF16), tri)
    sel = gt + eq * jnp.where(eq_rank < need, 1.0, 0.0)
    pos = _dot(sel.astype(BF16), tri)
    slot_ref[...] = jnp.where(sel > 0.5, pos, -1.0).astype(jnp.int32)


def _select_call(aff_t, cap):
    r, t = aff_t.shape
    return pl.pallas_call(
        functools.partial(_select_kernel, cap=cap),
        grid=(1,),
        in_specs=[pl.BlockSpec((r, t), lambda i: (0, 0))],
        out_specs=pl.BlockSpec((r, t), lambda i: (0, 0)),
        out_shape=jax.ShapeDtypeStruct((r, t), jnp.int32),
        compiler_params=_params("arbitrary"),
        name="select",
    )(aff_t)


def _gather_kernel(slot_ref, aff_ref, h_ref, xs_ref, gate_ref, *, cap, group):
    e_total, t = slot_ref.shape
    h = h_ref[...]
    ci = lax.broadcasted_iota(jnp.int32, (cap, t), 0)
    for e0 in range(0, e_total, group):
        hots = []
        for e in range(e0, e0 + group):
            hit = ci == slot_ref[e:e + 1, :]
            hots.append(jnp.where(hit, 1.0, 0.0).astype(BF16))
            gate_ref[e] = jnp.sum(jnp.where(hit, aff_ref[e:e + 1, :], 0.0), axis=-1, keepdims=True)
        onehot = hots[0] if group == 1 else jnp.concatenate(hots, axis=0)
        xs = _dot(onehot, h).astype(xs_ref.dtype)
        xs_ref[e0:e0 + group] = xs.reshape(group, cap, xs.shape[-1])


def _gather_call(slot_t, aff_t, h2, *, row0, nseq, t, cap):
    e = slot_t.shape[1]
    d = h2.shape[1]
    rb0 = row0 // t
    group = max(1, min(e, 512 // cap))
    return pl.pallas_call(
        functools.partial(_gather_kernel, cap=cap, group=group),
        grid=(nseq,),
        in_specs=[pl.BlockSpec((None, e, t), lambda b: (b, 0, 0)),
                  pl.BlockSpec((None, e, t), lambda b: (b, 0, 0)),
                  pl.BlockSpec((t, d), lambda b: (rb0 + b, 0))],
        out_specs=[pl.BlockSpec((e, cap, d), lambda b: (0, b, 0)),
                   pl.BlockSpec((e, cap, 1), lambda b: (0, b, 0))],
        out_shape=[jax.ShapeDtypeStruct((e, nseq * cap, d), BF16),
                   jax.ShapeDtypeStruct((e, nseq * cap, 1), F32)],
        compiler_params=_params("arbitrary"),
        name="gather",
    )(slot_t, aff_t, h2)


def _ffn_kernel(xp_ref, xs_ref, gp_ref, gs_ref, wg_ref, wu_ref, wd_ref, yp_ref, ys_ref,
                accp, accs, wgb, wub, wdb, *, chunk):
    f = pl.program_id(1)
    wgb[...] = wg_ref[...].astype(BF16)
    wub[...] = wu_ref[...].astype(BF16)
    wdb[...] = wd_ref[...].astype(BF16)

    @pl.when(f == 0)
    def _():
        accp[...] = jnp.zeros_like(accp)
        accs[...] = jnp.zeros_like(accs)

    def part(x_ref, acc):
        m = x_ref.shape[0]
        mc = min(chunk, m)
        for m0 in range(0, m, mc):
            x = x_ref[m0:m0 + mc, :]
            hg = _dot(x, wgb[...])
            hu = _dot(x, wub[...])
            hid = (hg * _sigmoid(hg) * hu).astype(BF16)
            acc[m0:m0 + mc, :] += _dot(hid, wdb[...])

    part(xp_ref, accp)
    part(xs_ref, accs)

    @pl.when(f == pl.num_programs(1) - 1)
    def _():
        yp_ref[...] = (accp[...] * gp_ref[...]).astype(yp_ref.dtype)
        ys_ref[...] = (accs[...] * gs_ref[...]).astype(ys_ref.dtype)


def _ffn_call(xs_p, xs_s, gate_p, gate_s, w_gate, w_up, w_down):
    e, mp, d = xs_p.shape
    ms = xs_s.shape[1]
    ff = w_gate.shape[2]
    tf = 256 if ff % 256 == 0 else ff
    tok = lambda m, w: pl.BlockSpec((None, m, w), lambda ee, f: (ee, 0, 0))
    return pl.pallas_call(
        functools.partial(_ffn_kernel, chunk=512),
        grid=(e, ff // tf),
        in_specs=[tok(mp, d), tok(ms, d), tok(mp, 1), tok(ms, 1),
                  pl.BlockSpec((None, d, tf), lambda ee, f: (ee, 0, f)),
                  pl.BlockSpec((None, d, tf), lambda ee, f: (ee, 0, f)),
                  pl.BlockSpec((None, tf, d), lambda ee, f: (ee, f, 0))],
        out_specs=[pl.BlockSpec((None, mp, d), lambda ee, f: (ee, 0, 0), pipeline_mode=pl.Buffered(1)),
                   pl.BlockSpec((None, ms, d), lambda ee, f: (ee, 0, 0), pipeline_mode=pl.Buffered(1))],
        out_shape=[jax.ShapeDtypeStruct((e, mp, d), BF16), jax.ShapeDtypeStruct((e, ms, d), BF16)],
        scratch_shapes=[pltpu.VMEM((mp, d), F32), pltpu.VMEM((ms, d), F32),
                        pltpu.VMEM((d, tf), BF16), pltpu.VMEM((d, tf), BF16), pltpu.VMEM((tf, d), BF16)],
        compiler_params=_params("arbitrary", "arbitrary"),
        name="expert_ffn",
    )(xs_p, xs_s, gate_p, gate_s, w_gate, w_up, w_down)


def _scatter_kernel(slot_ref, y_ref, x1_ref, g2_ref, wn_ref, o_ref, *, cap):
    e_total = y_ref.shape[0]
    tm = slot_ref.shape[0]
    slot = slot_ref[...]
    lane = lax.broadcasted_iota(jnp.int32, (tm, LANES), 1)
    per_block = max(1, LANES // cap)
    blocks = []
    for b0 in range(0, e_total, per_block):
        acc = jnp.zeros((tm, LANES), F32)
        for j in range(per_block):
            s = slot[:, b0 + j:b0 + j + 1]
            key = jnp.where(s >= 0, s + j * cap, -1)
            acc = acc + jnp.where(lane == key, 1.0, 0.0)
        blocks.append(acc.astype(BF16))
    onehot = jnp.concatenate(blocks, axis=1)
    y = y_ref[...].reshape(e_total * cap, y_ref.shape[-1])
    moe = _dot(onehot, y)
    x2 = x1_ref[...] + g2_ref[...] * moe
    o_ref[...] = x2 * lax.rsqrt(jnp.mean(x2 * x2, axis=-1, keepdims=True) + EPS) * wn_ref[...]


def _scatter_call(slot, y, x1, gate2, final_norm, *, row0, nseq, t, cap, mod_row0, mod_per_seq):
    e, _, d = y.shape
    tm = ROW_TILE
    nt = t // tm
    rb0 = row0 // tm
    assert cap == LANES or LANES % cap == 0
    return pl.pallas_call(
        functools.partial(_scatter_kernel, cap=cap),
        grid=(nseq, nt),
        in_specs=[pl.BlockSpec((None, tm, e), lambda b, i: (b, i, 0)),
                  pl.BlockSpec((e, cap, d), lambda b, i: (0, b, 0)),
                  pl.BlockSpec((tm, d), lambda b, i: (rb0 + b * nt + i, 0)),
                  pl.BlockSpec((None, 1, d), lambda b, i: (mod_row0 + b * mod_per_seq, 0, 0)),
                  pl.BlockSpec((1, d), lambda b, i: (0, 0))],
        out_specs=pl.BlockSpec((tm, d), lambda b, i: (b * nt + i, 0)),
        out_shape=jax.ShapeDtypeStruct((nseq * t, d), F32),
        compiler_params=_params("arbitrary", "arbitrary"),
        name="scatter_final",
    )(slot, y, x1, gate2, final_norm)


def kernel(x_prompt, x_sample, state_s5_re, state_s5_im, state_ret, c, c_ctx, final_norm, w_ada, b_ada, norm1, norm2, w_in, s5_a_re, s5_a_im, s5_log_dt, s5_b_re, s5_b_im, s5_c_re, s5_c_im, s5_d, w_s5_glu, ret_decay_logit, ret_gn_w, w_ret_out, w_out, w_router, w_exp_gate, w_exp_up, w_exp_down):
    bp, tp, d = x_prompt.shape
    bs, ts, _ = x_sample.shape
    depth = w_ada.shape[0]
    n_p, n_s = bp * tp, bs * ts
    xp = x_prompt.reshape(n_p, d)
    xs = x_sample.reshape(n_s, d)

    mod_rows = 16
    cvec = jnp.zeros((mod_rows, d), F32).at[0].set(c_ctx).at[1:1 + bs].set(c)
    rope_tabs = _grid_rope_tables(ts)
    cap_p = CAPACITY_FACTOR * tp // N_EXPERTS
    cap_s = CAPACITY_FACTOR * ts // N_EXPERTS

    new_re, new_im, new_ret = [], [], []
    for l in range(depth):
        mods = _ada_call(cvec, w_ada[l], b_ada[l])
        shift1, scale1, gate1, shift2, scale2, gate2 = [m.reshape(mod_rows, 1, d) for m in jnp.split(mods, 6, axis=-1)]

        h = _norm_mod_call(xp, xs, scale1, shift1, norm1[l].reshape(1, d), ts)
        u_rows = _uproj_call(h, w_in[l])
        proj = _proj_call(h, w_in[l], S5_WIDTH)

        ws_cmp, wy_cmp, a_step, d_rows = _s5_chunk_weights(
            s5_a_re[l], s5_a_im[l], s5_log_dt[l], s5_b_re[l], s5_b_im[l], s5_c_re[l], s5_c_im[l], s5_d[l])
        h0_s = _state_to_lanes(state_s5_re[:, l].astype(F32), state_s5_im[:, l].astype(F32))
        s5_paths = ((0, bp, tp // S5_CHUNK, False), (n_p // S5_CHUNK, bs, ts // S5_CHUNK, True))
        yg, hfin_p = _s5_call(u_rows, ws_cmp, wy_cmp, a_step, d_rows, h0_s, s5_paths)
        s5_re, s5_im = _lanes_to_state(hfin_p)
        new_re.append(s5_re)
        new_im.append(s5_im)

        log_gamma = jax.nn.log_sigmoid(ret_decay_logit[l].astype(F32))
        gn_w = ret_gn_w[l].reshape(1, -1).astype(F32)
        o_p, sfin = _ret_call(proj, log_gamma, gn_w, None, None, row0=0, nseq=bp, t=tp, group=4, latent=False,
                              want_state=True)
        o_s, = _ret_call(proj, log_gamma, gn_w, rope_tabs, state_ret[:, l].astype(F32), row0=n_p, nseq=bs, t=ts,
                         group=2, latent=True, want_state=False)
        new_ret.append(sfin)
        merged = _merge_call(yg, o_p, o_s, proj, w_s5_glu[l], w_ret_out[l])
        w_router_pad = jnp.zeros((d, LANES), F32).at[:, :N_EXPERTS].set(w_router[l].astype(F32))
        x1, h2, aff = _outproj_call(merged, xp, xs, w_out[l].astype(BF16), gate1, scale2, shift2,
                                    norm2[l].reshape(1, d), w_router_pad, ts)

        aff = aff[:, :N_EXPERTS]
        aff_p = jnp.transpose(aff[:n_p].reshape(bp, tp, N_EXPERTS), (0, 2, 1))
        aff_s = jnp.transpose(aff[n_p:].reshape(bs, ts, N_EXPERTS), (0, 2, 1))
        slot_p = _select_call(aff_p.reshape(bp * N_EXPERTS, tp), cap_p).reshape(bp, N_EXPERTS, tp)
        slot_s = _select_call(aff_s.reshape(bs * N_EXPERTS, ts), cap_s).reshape(bs, N_EXPERTS, ts)
        xe_p, ge_p = _gather_call(slot_p, aff_p, h2, row0=0, nseq=bp, t=tp, cap=cap_p)
        xe_s, ge_s = _gather_call(slot_s, aff_s, h2, row0=n_p, nseq=bs, t=ts, cap=cap_s)
        ye_p, ye_s = _ffn_call(xe_p, xe_s, ge_p, ge_s, w_exp_gate[l], w_exp_up[l], w_exp_down[l])

        last = l == depth - 1
        wn = final_norm.reshape(1, d).astype(F32) if last else None
        assert last, "only the final layer applies the output norm in the scatter kernel"
        yp = _scatter_call(jnp.transpose(slot_p, (0, 2, 1)), ye_p, x1, gate2, wn, row0=0, nseq=bp, t=tp, cap=cap_p,
                           mod_row0=0, mod_per_seq=0)
        ysm = _scatter_call(jnp.transpose(slot_s, (0, 2, 1)), ye_s, x1, gate2, wn, row0=n_p, nseq=bs, t=ts, cap=cap_s,
                            mod_row0=1, mod_per_seq=1)

    y_prompt = yp.reshape(bp, tp, d)
    y_sample = ysm.reshape(bs, ts, d)
    return (y_prompt, y_sample, jnp.stack(new_re, axis=1), jnp.stack(new_im, axis=1), jnp.stack(new_ret, axis=1))
```

```python
import functools
import math

import jax
import jax.numpy as jnp
from jax import lax
from jax.experimental import pallas as pl
from jax.experimental.pallas import tpu as pltpu

F32 = jnp.float32
BF16 = jnp.bfloat16
HIGHEST = lax.Precision.HIGHEST

EPS = 1e-6
GRID_W = 64
S5_WIDTH = 1024
S5_GROUP = 16
S5_STATE = 64
RET_HEADS = 8
RET_DK = 128
RET_DV = 256
ROPE_BASE = 10000.0
N_EXPERTS = 16
CAPACITY_FACTOR = 2

LANES = 128
S5_CHUNK = 8
S5_GB = LANES // S5_GROUP
S5_NGB = S5_WIDTH // LANES
S5_SCOLS = 4 * S5_GB * S5_STATE
ROW_TILE = 256
VMEM_LIMIT = 56 * 1024 * 1024


def _params(*sem):
    return pltpu.CompilerParams(dimension_semantics=sem, vmem_limit_bytes=VMEM_LIMIT)


def _sigmoid(x):
    return 1.0 / (1.0 + jnp.exp(-x))


def _gelu_tanh(x):
    return 0.5 * x * (1.0 + jnp.tanh(math.sqrt(2.0 / math.pi) * (x + 0.044715 * (x * x * x))))


def _dot(a, b):
    return jnp.dot(a, b, preferred_element_type=F32)


def _ada_kernel(c_ref, w_ref, b_ref, o_ref):
    c = c_ref[...]
    s = c * _sigmoid(c)
    o_ref[...] = jnp.dot(s, w_ref[...], precision=HIGHEST, preferred_element_type=F32) + b_ref[...]


def _ada_call(cvec, w_ada, b_ada):
    r, d = cvec.shape
    n = w_ada.shape[1]
    tn = 1024 if n % 1024 == 0 else 512
    return pl.pallas_call(
        _ada_kernel,
        grid=(n // tn,),
        in_specs=[pl.BlockSpec((r, d), lambda j: (0, 0)),
                  pl.BlockSpec((d, tn), lambda j: (0, j)),
                  pl.BlockSpec((1, tn), lambda j: (0, j))],
        out_specs=pl.BlockSpec((r, tn), lambda j: (0, j)),
        out_shape=jax.ShapeDtypeStruct((r, n), F32),
        compiler_params=_params("arbitrary"),
        name="ada",
    )(cvec, w_ada, b_ada.reshape(1, n))


def _norm_mod_kernel(xp_ref, xs_ref, sc_ref, sh_ref, w_ref, o_ref, *, n_prompt_tiles):
    i = pl.program_id(0)

    def body(x_ref):
        x = x_ref[...]
        y = x * lax.rsqrt(jnp.mean(x * x, axis=-1, keepdims=True) + EPS)
        h = (y * w_ref[...]) * (1.0 + sc_ref[...]) + sh_ref[...]
        o_ref[...] = h.astype(o_ref.dtype)

    @pl.when(i < n_prompt_tiles)
    def _():
        body(xp_ref)

    @pl.when(i >= n_prompt_tiles)
    def _():
        body(xs_ref)


def _mod_row(i, n_prompt_tiles, tiles_per_sample_seq):
    return jnp.where(i < n_prompt_tiles, 0, 1 + (i - n_prompt_tiles) // tiles_per_sample_seq)


def _norm_mod_call(xp, xs, scale, shift, w, t_sample):
    n_p, d = xp.shape
    n_s = xs.shape[0]
    tm = 2 * ROW_TILE
    npt, nst = n_p // tm, n_s // tm
    tps = t_sample // tm
    row = functools.partial(_mod_row, n_prompt_tiles=npt, tiles_per_sample_seq=tps)
    return pl.pallas_call(
        functools.partial(_norm_mod_kernel, n_prompt_tiles=npt),
        grid=(npt + nst,),
        in_specs=[pl.BlockSpec((tm, d), lambda i: (jnp.minimum(i, npt - 1), 0)),
                  pl.BlockSpec((tm, d), lambda i: (jnp.maximum(i - npt, 0), 0)),
                  pl.BlockSpec((None, 1, d), lambda i: (row(i), 0, 0)),
                  pl.BlockSpec((None, 1, d), lambda i: (row(i), 0, 0)),
                  pl.BlockSpec((1, d), lambda i: (0, 0))],
        out_specs=pl.BlockSpec((tm, d), lambda i: (i, 0)),
        out_shape=jax.ShapeDtypeStruct((n_p + n_s, d), BF16),
        compiler_params=_params("arbitrary"),
        name="norm_mod",
    )(xp, xs, scale, shift, w)


def _proj_kernel(a_ref, w_ref, o_ref, wb_ref):
    @pl.when(pl.program_id(1) == 0)
    def _():
        wb_ref[...] = w_ref[...].astype(BF16)

    o_ref[...] = _dot(a_ref[...], wb_ref[...]).astype(o_ref.dtype)


def _proj_call(a, w, col0):
    m, k = a.shape
    n = w.shape[1] - col0
    tm = 1024 if m % 1024 == 0 else m
    tn = 1024 if n % 1024 == 0 else 512
    cb0 = col0 // tn
    return pl.pallas_call(
        _proj_kernel,
        grid=(n // tn, m // tm),
        in_specs=[pl.BlockSpec((tm, k), lambda j, i: (i, 0)),
                  pl.BlockSpec((k, tn), lambda j, i: (0, cb0 + j))],
        out_specs=pl.BlockSpec((tm, tn), lambda j, i: (i, j)),
        out_shape=jax.ShapeDtypeStruct((m, n), BF16),
        scratch_shapes=[pltpu.VMEM((k, tn), BF16)],
        compiler_params=_params("arbitrary", "arbitrary"),
        name="in_proj",
    )(a, w)


def _cmul(ar, ai, br, bi):
    return ar * br - ai * bi, ar * bi + ai * br


def _uproj_kernel(a_ref, w_ref, o_ref, wb_ref, r_scr):
    @pl.when(pl.program_id(0) == 0)
    def _():
        wb_ref[...] = w_ref[...].astype(BF16)

    res = _dot(a_ref[...], wb_ref[...])
    for g in range(S5_NGB):
        r_scr[g] = res[:, g * LANES:(g + 1) * LANES]
    rows = a_ref.shape[0] // S5_CHUNK
    for g in range(S5_NGB):
        for tau in range(S5_CHUNK):
            c0 = (g * S5_CHUNK + tau) * LANES
            o_ref[:, c0:c0 + LANES] = r_scr[g, pl.ds(tau, rows, stride=S5_CHUNK), :].astype(o_ref.dtype)


def _uproj_call(a, w):
    m, k = a.shape
    tm = 1024
    return pl.pallas_call(
        _uproj_kernel,
        grid=(m // tm,),
        in_specs=[pl.BlockSpec((tm, k), lambda i: (i, 0)),
                  pl.BlockSpec((k, S5_WIDTH), lambda i: (0, 0))],
        out_specs=pl.BlockSpec((tm // S5_CHUNK, S5_CHUNK * S5_WIDTH), lambda i: (i, 0)),
        out_shape=jax.ShapeDtypeStruct((m // S5_CHUNK, S5_CHUNK * S5_WIDTH), BF16),
        scratch_shapes=[pltpu.VMEM((k, S5_WIDTH), BF16), pltpu.VMEM((S5_NGB, tm, LANES), F32)],
        compiler_params=_params("arbitrary"),
        name="u_proj",
    )(a, w)


def _s5_chunk_weights(a_re, a_im, log_dt, b_re, b_im, c_re, c_im, d_skip):
    L, G, P, HG, GB, NGB = S5_CHUNK, S5_WIDTH // S5_GROUP, S5_STATE, S5_GROUP, S5_GB, S5_NGB
    a_re, a_im = a_re.astype(F32), a_im.astype(F32)
    dt = jnp.exp(log_dt.astype(F32))[..., None]
    adt_re, adt_im = a_re * dt, a_im * dt
    mag = jnp.exp(adt_re)
    lam_re, lam_im = mag * jnp.cos(adt_im), mag * jnp.sin(adt_im)
    den = a_re * a_re + a_im * a_im
    q_re = ((lam_re - 1.0) * a_re + lam_im * a_im) / den
    q_im = (lam_im * a_re - (lam_re - 1.0) * a_im) / den
    bb_re, bb_im = _cmul(q_re[..., None], q_im[..., None], b_re.astype(F32), b_im.astype(F32))
    n = jnp.arange(L + 1, dtype=F32)
    pmag = jnp.exp(adt_re[..., None] * n)
    pw_re, pw_im = pmag * jnp.cos(adt_im[..., None] * n), pmag * jnp.sin(adt_im[..., None] * n)

    def ws_exponents(pw, d):
        e = pw[0][..., :L][..., ::-1] if d == 0 else pw[1][..., :L]
        return jnp.transpose(e, (2, 0, 1))[:, :, None]

    def ws_dir(d):
        bt_re, bt_im = jnp.transpose(bb_re[d], (0, 2, 1)), jnp.transpose(bb_im[d], (0, 2, 1))
        return _cmul(ws_exponents(pw_re, d), ws_exponents(pw_im, d), bt_re[None], bt_im[None])

    (wf_re, wf_im), (wb_re, wb_im) = ws_dir(0), ws_dir(1)
    ws = jnp.concatenate([wf_re, wb_re, wf_im, wb_im], axis=-1).reshape(L, NGB, GB * HG, 4 * P)

    def wy_exponents(pw):
        return jnp.stack([pw[0], pw[1][..., ::-1]])

    ct_re = jnp.transpose(c_re.astype(F32), (0, 1, 3, 2))[:, :, :, None]
    ct_im = jnp.transpose(c_im.astype(F32), (0, 1, 3, 2))[:, :, :, None]
    y_re, y_im = _cmul(wy_exponents(pw_re)[..., None], wy_exponents(pw_im)[..., None], ct_re, ct_im)
    wy = jnp.stack([y_re, -y_im]).reshape(2, 2, NGB, GB * P, (L + 1) * HG)

    def lanes(x):
        return jnp.transpose(x.reshape(2, NGB, GB * P), (1, 0, 2)).reshape(NGB, 1, 2 * GB * P)

    a_step = jnp.concatenate([lanes(pw_re[..., L]), lanes(pw_im[..., L])], axis=1)
    return ws.astype(BF16), wy.astype(BF16), a_step, d_skip.astype(F32).reshape(NGB, 1, LANES)


def _s5_expand(xws_ref, zwy_ref, d_ref, wm, wws, wwy):
    L, hg, p = S5_CHUNK, S5_GROUP, S5_STATE
    lg_hg, lg_p, lg_lanes = hg.bit_length() - 1, p.bit_length() - 1, LANES.bit_length() - 1
    gmask = S5_GB - 1
    rblk = 512

    def expand(x, src_of_col, row_group, col_group, o_ref, cblk):
        rows, k = x.shape
        for c0 in range(0, o_ref.shape[1], cblk):
            kk = lax.broadcasted_iota(jnp.int32, (k, cblk), 0)
            cc = lax.broadcasted_iota(jnp.int32, (k, cblk), 1) + c0
            spread = jnp.where(kk == src_of_col(cc), 1.0, 0.0).astype(BF16)
            rr = lax.broadcasted_iota(jnp.int32, (rblk, cblk), 0)
            c2 = lax.broadcasted_iota(jnp.int32, (rblk, cblk), 1) + c0
            keep = row_group(rr) == col_group(c2)
            for r0 in range(0, rows, rblk):
                full = _dot(x[r0:r0 + rblk, :], spread)
                o_ref[r0:r0 + rblk, c0:c0 + cblk] = jnp.where(keep, full, 0.0).astype(o_ref.dtype)

    lane_group = lambda i: jnp.right_shift(i, lg_hg) & gmask
    state_group = lambda i: jnp.right_shift(i, lg_p) & gmask
    chunk_src = lambda c: jnp.right_shift(c, lg_lanes) * hg + (c & (hg - 1))
    state_src = lambda c: jnp.right_shift(c, lg_p + 3) * p + (c & (p - 1))
    xws = xws_ref[...]
    expand(xws.reshape(L * LANES, xws.shape[-1]), state_src, lane_group, state_group, wws, 1024)
    zwy = zwy_ref[...]
    expand(zwy.reshape(S5_SCOLS, zwy.shape[-1]), chunk_src, state_group, lane_group, wwy, wwy.shape[1])

    h = S5_SCOLS // 4
    last, first = wws[(L - 1) * LANES:L * LANES, :], wws[0:LANES, :]
    pf = _dot(last[:, 0:h], wwy[0:h, :]) + _dot(last[:, 2 * h:3 * h], wwy[2 * h:3 * h, :])
    pb = _dot(first[:, h:2 * h], wwy[h:2 * h, :]) + _dot(first[:, 3 * h:4 * h], wwy[3 * h:4 * h, :])
    eye = lax.broadcasted_iota(jnp.int32, (LANES, LANES), 0) == lax.broadcasted_iota(jnp.int32, (LANES, LANES), 1)
    p0 = pf[:, 0:LANES] + pb[:, L * LANES:(L + 1) * LANES] + jnp.where(eye, d_ref[...], 0.0)
    for s in range(L):
        for t in range(L):
            if t > s:
                blk = pf[:, (t - s) * LANES:(t - s + 1) * LANES]
            elif t < s:
                blk = pb[:, (L - (s - t)) * LANES:(L - (s - t) + 1) * LANES]
            else:
                blk = p0
            wm[s * LANES:(s + 1) * LANES, t * LANES:(t + 1) * LANES] = blk.astype(wm.dtype)


def _s5_kernel(u_ref, xws_ref, zwy_ref, d_ref, a_ref, h0_ref, y_ref, hfin_ref,
               wm, wws, wwy, s_scr, hb_scr, y_scr, *, paths):
    nu = S5_CHUNK
    half = S5_SCOLS // 4
    nsl = half // LANES
    cw = 2 * LANES

    _s5_expand(xws_ref, zwy_ref, d_ref, wm, wws, wwy)
    a = a_ref[...]

    for row0, nseq, nchunk, has_h0 in paths:
        r = nseq * nchunk

        for q in range(4):
            sq = _dot(u_ref[row0:row0 + r, :], wws[:, q * half:(q + 1) * half])
            for k in range(nsl):
                for s in range(nseq):
                    s_scr[q * nsl + k, pl.ds(s, nchunk, stride=nseq), :] = (
                        sq[s * nchunk:(s + 1) * nchunk, k * LANES:(k + 1) * LANES])

        if has_h0:
            h0 = h0_ref[...]
            carry0 = tuple(h0[:, j * LANES:(j + 1) * LANES] for j in range(4 * nsl))
        else:
            carry0 = tuple(jnp.zeros((nseq, LANES), F32) for _ in range(4 * nsl))

        def step(i, carry, nseq=nseq, nchunk=nchunk):
            rf = pl.multiple_of(i * nseq, nseq)
            rb = pl.multiple_of((nchunk - 1 - i) * nseq, nseq)
            new = list(carry)
            for k in range(nsl):
                for re, im, rows, acol in ((k, 2 * nsl + k, rf, k), (nsl + k, 3 * nsl + k, rb, nsl + k)):
                    s_re = s_scr[re, pl.ds(rows, nseq), :]
                    s_im = s_scr[im, pl.ds(rows, nseq), :]
                    h_re, h_im = carry[re], carry[im]
                    s_scr[re, pl.ds(rows, nseq), :] = h_re
                    s_scr[im, pl.ds(rows, nseq), :] = h_im
                    a_re = a[0:1, acol * LANES:(acol + 1) * LANES]
                    a_im = a[1:2, acol * LANES:(acol + 1) * LANES]
                    new[re] = a_re * h_re - a_im * h_im + s_re
                    new[im] = a_re * h_im + a_im * h_re + s_im
            return tuple(new)

        fin = lax.fori_loop(0, nchunk, step, carry0)
        if not has_h0:
            for j in range(4 * nsl):
                hfin_ref[:, j * LANES:(j + 1) * LANES] = fin[j]

        for j in range(4 * nsl):
            for s in range(nseq):
                hb_scr[s * nchunk:(s + 1) * nchunk, j * LANES:(j + 1) * LANES] = (
                    s_scr[j, pl.ds(s, nchunk, stride=nseq), :].astype(BF16))

        for j in range(nu * LANES // cw):
            c0 = j * cw
            y = _dot(u_ref[row0:row0 + r, :], wm[:, c0:c0 + cw])
            for q in range(4):
                off = c0 + (LANES if q % 2 == 0 else 0)
                y = y + _dot(hb_scr[0:r, q * half:(q + 1) * half], wwy[q * half:(q + 1) * half, off:off + cw])
            g = _gelu_tanh(y)
            for t2 in range(cw // LANES):
                tau = j * (cw // LANES) + t2
                y_scr[pl.ds(row0 * nu + tau, r, stride=nu), :] = g[:, t2 * LANES:(t2 + 1) * LANES]

    y_ref[...] = y_scr[...].astype(y_ref.dtype)


def _s5_call(u_rows, ws_cmp, wy_cmp, a_step, d_rows, h0, paths):
    rows = u_rows.shape[0]
    n_tok = rows * S5_CHUNK
    sc, k = S5_SCOLS, S5_CHUNK * LANES
    rmax = max(p[1] * p[2] for p in paths)
    nseq_fin = [p[1] for p in paths if not p[3]][0]
    blk = lambda a, b: pl.BlockSpec((None, a, b), lambda g: (g, 0, 0))
    return pl.pallas_call(
        functools.partial(_s5_kernel, paths=paths),
        grid=(S5_NGB,),
        in_specs=[pl.BlockSpec((rows, k), lambda g: (0, g)),
                  pl.BlockSpec((S5_CHUNK, None, LANES, ws_cmp.shape[-1]), lambda g: (0, g, 0, 0)),
                  pl.BlockSpec((2, 2, None, sc // 4, wy_cmp.shape[-1]), lambda g: (0, 0, g, 0, 0)),
                  blk(1, LANES), blk(2, sc // 2), blk(h0.shape[1], sc)],
        out_specs=[pl.BlockSpec((n_tok, LANES), lambda g: (0, g)), blk(nseq_fin, sc)],
        out_shape=[jax.ShapeDtypeStruct((n_tok, S5_WIDTH), BF16),
                   jax.ShapeDtypeStruct((S5_NGB, nseq_fin, sc), F32)],
        scratch_shapes=[pltpu.VMEM((k, k), BF16), pltpu.VMEM((k, sc), BF16),
                        pltpu.VMEM((sc, (S5_CHUNK + 1) * LANES), BF16),
                        pltpu.VMEM((4 * (sc // 4 // LANES), rmax, LANES), F32), pltpu.VMEM((rmax, sc), BF16),
                        pltpu.VMEM((n_tok, LANES), F32)],
        compiler_params=_params("arbitrary"),
        name="s5",
    )(u_rows, ws_cmp, wy_cmp, d_rows, a_step, h0)


def _state_to_lanes(s_re, s_im):
    b = s_re.shape[0]

    def one(x):
        return jnp.transpose(x.reshape(b, 2, S5_NGB, S5_GB * S5_STATE), (2, 0, 1, 3)).reshape(S5_NGB, b, -1)

    return jnp.concatenate([one(s_re), one(s_im)], axis=-1)


def _lanes_to_state(h):
    b = h.shape[1]
    x = h.reshape(S5_NGB, b, 2, 2, S5_GB, S5_STATE)
    x = jnp.transpose(x, (2, 1, 3, 0, 4, 5)).reshape(2, b, 2, S5_NGB * S5_GB, S5_STATE)
    return x[0], x[1]


def _ret_kernel(lg_ref, q_ref, k_ref, v_ref, g_ref, gn_ref, *rest, t, group, latent, want_state):
    rest = list(rest)
    if latent:
        cos_ref, sin_ref, s0_ref = rest[:3]
        rest = rest[3:]
    o_ref = rest.pop(0)
    sfin_ref = rest.pop(0) if want_state else None
    dmat_ref = rest.pop(0)

    h = pl.program_id(0)
    lgf = lg_ref[0, h]
    lgb = lg_ref[1, h]

    @pl.when(pl.program_id(1) == 0)
    def _():
        ti = lax.broadcasted_iota(jnp.int32, (t, t), 0)
        si = lax.broadcasted_iota(jnp.int32, (t, t), 1)
        diff = (ti - si).astype(F32)
        dmat_ref[...] = jnp.exp(jnp.where(diff >= 0, lgf * diff, -lgb * diff))

    pos = lax.broadcasted_iota(jnp.int32, (t, 1), 0).astype(F32)
    if latent:
        lane = lax.broadcasted_iota(jnp.int32, (t, RET_DK), 1)
        first = (lane % (RET_DK // 2)) < (RET_DK // 4)

        def rope(x):
            swapped = jnp.where(first, pltpu.roll(x, RET_DK - RET_DK // 4, 1), pltpu.roll(x, RET_DK // 4, 1))
            return x * cos_ref[...] + swapped * sin_ref[...]

    for j in range(group):
        rows = slice(j * t, (j + 1) * t)
        q = q_ref[rows, :].astype(F32) * (RET_DK ** -0.5)
        k = k_ref[rows, :].astype(F32)
        if latent:
            q = rope(q)
            k = rope(k)
        qb = q.astype(BF16)
        kb = k.astype(BF16)
        v = v_ref[rows, :]

        scores = lax.dot_general(qb, kb, (((1,), (1,)), ((), ())), preferred_element_type=F32)
        o = _dot((scores * dmat_ref[...]).astype(BF16), v)

        if latent:
            o = o + _dot(qb, s0_ref[j, 0].astype(BF16)) * jnp.exp(lgf * (pos + 1.0))
            o = o + _dot(qb, s0_ref[j, 1].astype(BF16)) * jnp.exp(lgb * (t - pos))
        if want_state:
            kf = (k * jnp.exp(lgf * (t - 1.0 - pos))).astype(BF16)
            kr = (k * jnp.exp(lgb * pos)).astype(BF16)
            tn = (((0,), (0,)), ((), ()))
            sfin_ref[j, 0] = lax.dot_general(kf, v, tn, preferred_element_type=F32)
            sfin_ref[j, 1] = lax.dot_general(kr, v, tn, preferred_element_type=F32)

        mu = jnp.mean(o, axis=-1, keepdims=True)
        d = o - mu
        var = jnp.mean(d * d, axis=-1, keepdims=True)
        on = d * lax.rsqrt(var + EPS) * gn_ref[...]
        g = g_ref[rows, :].astype(F32)
        o_ref[rows, :] = (g * _sigmoid(g) * on).astype(o_ref.dtype)


def _ret_call(proj, log_gamma, gn_w, rope_tabs, s0, *, row0, nseq, t, group, latent, want_state):
    h, dk, dv = RET_HEADS, RET_DK, RET_DV
    gt = group * t
    rb0 = row0 // gt
    q0 = 0
    k0 = q0 + h
    v0 = (2 * h * dk) // dv
    g0 = v0 + h
    in_specs = [pl.BlockSpec(memory_space=pltpu.SMEM),
                pl.BlockSpec((gt, dk), lambda hh, b: (rb0 + b, q0 + hh)),
                pl.BlockSpec((gt, dk), lambda hh, b: (rb0 + b, k0 + hh)),
                pl.BlockSpec((gt, dv), lambda hh, b: (rb0 + b, v0 + hh)),
                pl.BlockSpec((gt, dv), lambda hh, b: (rb0 + b, g0 + hh)),
                pl.BlockSpec((1, dv), lambda hh, b: (0, hh))]
    args = [log_gamma, proj, proj, proj, proj, gn_w]
    if latent:
        in_specs += [pl.BlockSpec((t, dk), lambda hh, b: (0, 0)),
                     pl.BlockSpec((t, dk), lambda hh, b: (0, 0)),
                     pl.BlockSpec((group, 2, None, dk, dv), lambda hh, b: (b, 0, hh, 0, 0))]
        args += [rope_tabs[0], rope_tabs[1], s0]
    out_specs = [pl.BlockSpec((gt, dv), lambda hh, b: (b, hh))]
    out_shape = [jax.ShapeDtypeStruct((nseq * t, h * dv), BF16)]
    if want_state:
        out_specs.append(pl.BlockSpec((group, 2, None, dk, dv), lambda hh, b: (b, 0, hh, 0, 0)))
        out_shape.append(jax.ShapeDtypeStruct((nseq, 2, h, dk, dv), F32))
    return pl.pallas_call(
        functools.partial(_ret_kernel, t=t, group=group, latent=latent, want_state=want_state),
        grid=(h, nseq // group),
        in_specs=in_specs,
        out_specs=out_specs,
        out_shape=out_shape,
        scratch_shapes=[pltpu.VMEM((t, t), F32)],
        compiler_params=_params("arbitrary", "arbitrary"),
        name="retention_latent" if latent else "retention_ctx",
    )(*args)


def _grid_rope_tables(t):
    quarter = RET_DK // 4
    freqs = ROPE_BASE ** (-jnp.arange(quarter, dtype=F32) / quarter)
    pos = jnp.arange(t)
    row = (pos // GRID_W).astype(F32)[:, None] * freqs[None, :]
    col = (pos % GRID_W).astype(F32)[:, None] * freqs[None, :]
    cos = jnp.concatenate([jnp.cos(row), jnp.cos(row), jnp.cos(col), jnp.cos(col)], axis=-1)
    sin = jnp.concatenate([-jnp.sin(row), jnp.sin(row), -jnp.sin(col), jnp.sin(col)], axis=-1)
    return cos, sin


def _merge_kernel(yg_ref, op_ref, os_ref, ga_ref, gb_ref, wga_ref, wgb_ref, wr_ref, out_ref, wga_b, wgb_b, wr_b,
                  *, n_prompt_tiles):
    i = pl.program_id(1)

    @pl.when(i == 0)
    def _():
        wga_b[...] = wga_ref[...].astype(BF16)
        wgb_b[...] = wgb_ref[...].astype(BF16)
        wr_b[...] = wr_ref[...].astype(BF16)

    def body(o_ref):
        yg = yg_ref[...]
        za = _dot(yg, wga_b[...])
        zb = _dot(yg, wgb_b[...])
        ob = _dot(o_ref[...], wr_b[...])
        out_a = za * _sigmoid(zb)
        ga = ga_ref[...].astype(F32)
        gb = gb_ref[...].astype(F32)
        out_ref[...] = (_sigmoid(ga) * out_a + _sigmoid(gb) * ob).astype(out_ref.dtype)

    @pl.when(i < n_prompt_tiles)
    def _():
        body(op_ref)

    @pl.when(i >= n_prompt_tiles)
    def _():
        body(os_ref)


def _merge_call(yg, o_p, o_s, proj, w_glu, w_ret_out):
    m, ks = yg.shape
    kr = o_p.shape[1]
    d = w_ret_out.shape[1]
    tm = 1024 if m % 1024 == 0 else m
    tn = 512 if d % 512 == 0 else d
    ga0 = (proj.shape[1] - 2 * d) // tn
    gb0 = (proj.shape[1] - d) // tn
    nb = d // tn
    npt = o_p.shape[0] // tm
    return pl.pallas_call(
        functools.partial(_merge_kernel, n_prompt_tiles=npt),
        grid=(nb, m // tm),
        in_specs=[pl.BlockSpec((tm, ks), lambda j, i: (i, 0)),
                  pl.BlockSpec((tm, kr), lambda j, i: (jnp.minimum(i, npt - 1), 0)),
                  pl.BlockSpec((tm, kr), lambda j, i: (jnp.maximum(i - npt, 0), 0)),
                  pl.BlockSpec((tm, tn), lambda j, i: (i, ga0 + j)),
                  pl.BlockSpec((tm, tn), lambda j, i: (i, gb0 + j)),
                  pl.BlockSpec((ks, tn), lambda j, i: (0, j)),
                  pl.BlockSpec((ks, tn), lambda j, i: (0, nb + j)),
                  pl.BlockSpec((kr, tn), lambda j, i: (0, j))],
        out_specs=pl.BlockSpec((tm, tn), lambda j, i: (i, j)),
        out_shape=jax.ShapeDtypeStruct((m, d), BF16),
        scratch_shapes=[pltpu.VMEM((ks, tn), BF16), pltpu.VMEM((ks, tn), BF16), pltpu.VMEM((kr, tn), BF16)],
        compiler_params=_params("arbitrary", "arbitrary"),
        name="merge",
    )(yg, o_p, o_s, proj, proj, w_glu, w_glu, w_ret_out)


def _outproj_kernel(mg_ref, xp_ref, xs_ref, w_ref, g1_ref, sc_ref, sh_ref, n2_ref, wr_ref,
                    x1_ref, h2_ref, aff_ref, *, n_prompt_tiles, n_experts):
    i = pl.program_id(0)
    upd = g1_ref[...] * _dot(mg_ref[...], w_ref[...])

    def body(x_ref):
        x1 = x_ref[...] + upd
        x1_ref[...] = x1
        y = x1 * lax.rsqrt(jnp.mean(x1 * x1, axis=-1, keepdims=True) + EPS)
        h2 = (y * n2_ref[...]) * (1.0 + sc_ref[...]) + sh_ref[...]
        h2_ref[...] = h2.astype(h2_ref.dtype)
        wr = wr_ref[...]
        w_hi = wr.astype(BF16)
        w_lo = (wr - w_hi.astype(F32)).astype(BF16)
        h_hi = h2.astype(BF16)
        h_lo = (h2 - h_hi.astype(F32)).astype(BF16)
        hw = _dot(h_hi, jnp.concatenate([w_hi, w_lo], axis=1))
        logits = hw[:, :LANES] + (hw[:, LANES:] + _dot(h_lo, w_hi))
        lane = lax.broadcasted_iota(jnp.int32, logits.shape, 1)
        logits = jnp.where(lane < n_experts, logits, -jnp.inf)
        e = jnp.exp(logits - jnp.max(logits, axis=-1, keepdims=True))
        aff_ref[...] = e / jnp.sum(e, axis=-1, keepdims=True)

    @pl.when(i < n_prompt_tiles)
    def _():
        body(xp_ref)

    @pl.when(i >= n_prompt_tiles)
    def _():
        body(xs_ref)


def _outproj_call(merged, xp, xs, w_out_b, gate1, scale2, shift2, norm2, w_router_pad, t_sample):
    m, d = merged.shape
    n_p = xp.shape[0]
    tm = 2 * ROW_TILE
    npt = n_p // tm
    tps = t_sample // tm
    row = functools.partial(_mod_row, n_prompt_tiles=npt, tiles_per_sample_seq=tps)
    mod = pl.BlockSpec((None, 1, d), lambda i: (row(i), 0, 0))
    return pl.pallas_call(
        functools.partial(_outproj_kernel, n_prompt_tiles=npt, n_experts=N_EXPERTS),
        grid=(m // tm,),
        in_specs=[pl.BlockSpec((tm, d), lambda i: (i, 0)),
                  pl.BlockSpec((tm, d), lambda i: (jnp.minimum(i, npt - 1), 0)),
                  pl.BlockSpec((tm, d), lambda i: (jnp.maximum(i - npt, 0), 0)),
                  pl.BlockSpec((d, d), lambda i: (0, 0)),
                  mod, mod, mod,
                  pl.BlockSpec((1, d), lambda i: (0, 0)),
                  pl.BlockSpec((d, LANES), lambda i: (0, 0))],
        out_specs=[pl.BlockSpec((tm, d), lambda i: (i, 0)),
                   pl.BlockSpec((tm, d), lambda i: (i, 0)),
                   pl.BlockSpec((tm, LANES), lambda i: (i, 0))],
        out_shape=[jax.ShapeDtypeStruct((m, d), F32),
                   jax.ShapeDtypeStruct((m, d), BF16),
                   jax.ShapeDtypeStruct((m, LANES), F32)],
        compiler_params=_params("arbitrary"),
        name="out_proj_router",
    )(merged, xp, xs, w_out_b, gate1, scale2, shift2, norm2, w_router_pad)


def _select_kernel(a_ref, slot_ref, *, cap):
    a = a_ref[...]
    r, t = a.shape

    def as_float(bits):
        return pltpu.bitcast(bits, F32)

    def bisect(_, carry):
        lo, hi = carry
        mid = lo + jnp.right_shift(hi - lo + 1, 1)
        cnt = jnp.sum(jnp.where(a >= as_float(mid), 1.0, 0.0), axis=-1, keepdims=True)
        ok = cnt >= cap
        return jnp.where(ok, mid, lo), jnp.where(ok, hi, mid - 1)

    lo0 = jnp.zeros((r, 1), jnp.int32)
    hi0 = jnp.full((r, 1), 0x3F800000, jnp.int32)
    thr, _ = lax.fori_loop(0, 31, bisect, (lo0, hi0))

    gt = jnp.where(a >= as_float(thr + 1), 1.0, 0.0)
    eq = jnp.where(a >= as_float(thr), 1.0, 0.0) - gt
    need = cap - jnp.sum(gt, axis=-1, keepdims=True)
    before = lax.broadcasted_iota(jnp.int32, (t, t), 0) < lax.broadcasted_iota(jnp.int32, (t, t), 1)
    tri = jnp.where(before, 1.0, 0.0).astype(BF16)
    eq_rank = _dot(eq.astype(BF16), tri)
    sel = gt + eq * jnp.where(eq_rank < need, 1.0, 0.0)
    pos = _dot(sel.astype(BF16), tri)
    slot_ref[...] = jnp.where(sel > 0.5, pos, -1.0).astype(jnp.int32)


def _select_call(aff_t, cap):
    r, t = aff_t.shape
    return pl.pallas_call(
        functools.partial(_select_kernel, cap=cap),
        grid=(1,),
        in_specs=[pl.BlockSpec((r, t), lambda i: (0, 0))],
        out_specs=pl.BlockSpec((r, t), lambda i: (0, 0)),
        out_shape=jax.ShapeDtypeStruct((r, t), jnp.int32),
        compiler_params=_params("arbitrary"),
        name="select",
    )(aff_t)


def _gather_kernel(slot_ref, aff_ref, h_ref, xs_ref, gate_ref, *, cap, group):
    e_total, t = slot_ref.shape
    h = h_ref[...]
    ci = lax.broadcasted_iota(jnp.int32, (cap, t), 0)
    for e0 in range(0, e_total, group):
        hots = []
        for e in range(e0, e0 + group):
            hit = ci == slot_ref[e:e + 1, :]
            hots.append(jnp.where(hit, 1.0, 0.0).astype(BF16))
            gate_ref[e] = jnp.sum(jnp.where(hit, aff_ref[e:e + 1, :], 0.0), axis=-1, keepdims=True)
        onehot = hots[0] if group == 1 else jnp.concatenate(hots, axis=0)
        xs = _dot(onehot, h).astype(xs_ref.dtype)
        xs_ref[e0:e0 + group] = xs.reshape(group, cap, xs.shape[-1])


def _gather_call(slot_t, aff_t, h2, *, row0, nseq, t, cap):
    e = slot_t.shape[1]
    d = h2.shape[1]
    rb0 = row0 // t
    group = max(1, min(e, 512 // cap))
    return pl.pallas_call(
        functools.partial(_gather_kernel, cap=cap, group=group),
        grid=(nseq,),
        in_specs=[pl.BlockSpec((None, e, t), lambda b: (b, 0, 0)),
                  pl.BlockSpec((None, e, t), lambda b: (b, 0, 0)),
                  pl.BlockSpec((t, d), lambda b: (rb0 + b, 0))],
        out_specs=[pl.BlockSpec((e, cap, d), lambda b: (0, b, 0)),
                   pl.BlockSpec((e, cap, 1), lambda b: (0, b, 0))],
        out_shape=[jax.ShapeDtypeStruct((e, nseq * cap, d), BF16),
                   jax.ShapeDtypeStruct((e, nseq * cap, 1), F32)],
        compiler_params=_params("arbitrary"),
        name="gather",
    )(slot_t, aff_t, h2)


def _ffn_kernel(xp_ref, xs_ref, gp_ref, gs_ref, wg_ref, wu_ref, wd_ref, yp_ref, ys_ref,
                accp, accs, wgb, wub, wdb, *, chunk):
    f = pl.program_id(1)

    @pl.when(f == 0)
    def _():
        accp[...] = jnp.zeros_like(accp)
        accs[...] = jnp.zeros_like(accs)

    wgb[...] = wg_ref[...].astype(BF16)
    wub[...] = wu_ref[...].astype(BF16)
    wdb[...] = wd_ref[...].astype(BF16)

    def part(x_ref, acc):
        m = x_ref.shape[0]
        mc = min(chunk, m)
        for m0 in range(0, m, mc):
            x = x_ref[m0:m0 + mc, :]
            hg = _dot(x, wgb[...])
            hu = _dot(x, wub[...])
            hid = (hg * _sigmoid(hg) * hu).astype(BF16)
            acc[m0:m0 + mc, :] += _dot(hid, wdb[...])

    part(xp_ref, accp)
    part(xs_ref, accs)

    @pl.when(f == pl.num_programs(1) - 1)
    def _():
        yp_ref[...] = (accp[...] * gp_ref[...]).astype(yp_ref.dtype)
        ys_ref[...] = (accs[...] * gs_ref[...]).astype(ys_ref.dtype)


def _ffn_call(xs_p, xs_s, gate_p, gate_s, w_gate, w_up, w_down):
    e, mp, d = xs_p.shape
    ms = xs_s.shape[1]
    ff = w_gate.shape[2]
    tf = 256 if ff % 256 == 0 else ff
    tok = lambda m, w: pl.BlockSpec((None, m, w), lambda ee, f: (ee, 0, 0))
    return pl.pallas_call(
        functools.partial(_ffn_kernel, chunk=512),
        grid=(e, ff // tf),
        in_specs=[tok(mp, d), tok(ms, d), tok(mp, 1), tok(ms, 1),
                  pl.BlockSpec((None, d, tf), lambda ee, f: (ee, 0, f)),
                  pl.BlockSpec((None, d, tf), lambda ee, f: (ee, 0, f)),
                  pl.BlockSpec((None, tf, d), lambda ee, f: (ee, f, 0))],
        out_specs=[pl.BlockSpec((None, mp, d), lambda ee, f: (ee, 0, 0), pipeline_mode=pl.Buffered(1)),
                   pl.BlockSpec((None, ms, d), lambda ee, f: (ee, 0, 0), pipeline_mode=pl.Buffered(1))],
        out_shape=[jax.ShapeDtypeStruct((e, mp, d), BF16), jax.ShapeDtypeStruct((e, ms, d), BF16)],
        scratch_shapes=[pltpu.VMEM((mp, d), F32), pltpu.VMEM((ms, d), F32),
                        pltpu.VMEM((d, tf), BF16), pltpu.VMEM((d, tf), BF16), pltpu.VMEM((tf, d), BF16)],
        compiler_params=_params("arbitrary", "arbitrary"),
        name="expert_ffn",
    )(xs_p, xs_s, gate_p, gate_s, w_gate, w_up, w_down)


def _scatter_kernel(slot_ref, y_ref, x1_ref, g2_ref, wn_ref, o_ref, *, cap):
    e_total = y_ref.shape[0]
    tm = slot_ref.shape[0]
    slot = slot_ref[...]
    lane = lax.broadcasted_iota(jnp.int32, (tm, LANES), 1)
    per_block = max(1, LANES // cap)
    blocks = []
    for b0 in range(0, e_total, per_block):
        acc = jnp.zeros((tm, LANES), F32)
        for j in range(per_block):
            s = slot[:, b0 + j:b0 + j + 1]
            key = jnp.where(s >= 0, s + j * cap, -1)
            acc = acc + jnp.where(lane == key, 1.0, 0.0)
        blocks.append(acc.astype(BF16))
    onehot = jnp.concatenate(blocks, axis=1)
    y = y_ref[...].reshape(e_total * cap, y_ref.shape[-1])
    moe = _dot(onehot, y)
    x2 = x1_ref[...] + g2_ref[...] * moe
    o_ref[...] = x2 * lax.rsqrt(jnp.mean(x2 * x2, axis=-1, keepdims=True) + EPS) * wn_ref[...]


def _scatter_call(slot, y, x1, gate2, final_norm, *, row0, nseq, t, cap, mod_row0, mod_per_seq):
    e, _, d = y.shape
    tm = ROW_TILE
    nt = t // tm
    rb0 = row0 // tm
    assert cap == LANES or LANES % cap == 0
    return pl.pallas_call(
        functools.partial(_scatter_kernel, cap=cap),
        grid=(nseq, nt),
        in_specs=[pl.BlockSpec((None, tm, e), lambda b, i: (b, i, 0)),
                  pl.BlockSpec((e, cap, d), lambda b, i: (0, b, 0)),
                  pl.BlockSpec((tm, d), lambda b, i: (rb0 + b * nt + i, 0)),
                  pl.BlockSpec((None, 1, d), lambda b, i: (mod_row0 + b * mod_per_seq, 0, 0)),
                  pl.BlockSpec((1, d), lambda b, i: (0, 0))],
        out_specs=pl.BlockSpec((tm, d), lambda b, i: (b * nt + i, 0)),
        out_shape=jax.ShapeDtypeStruct((nseq * t, d), F32),
        compiler_params=_params("arbitrary", "arbitrary"),
        name="scatter_final",
    )(slot, y, x1, gate2, final_norm)


def kernel(x_prompt, x_sample, state_s5_re, state_s5_im, state_ret, c, c_ctx, final_norm, w_ada, b_ada, norm1, norm2, w_in, s5_a_re, s5_a_im, s5_log_dt, s5_b_re, s5_b_im, s5_c_re, s5_c_im, s5_d, w_s5_glu, ret_decay_logit, ret_gn_w, w_ret_out, w_out, w_router, w_exp_gate, w_exp_up, w_exp_down):
    bp, tp, d = x_prompt.shape
    bs, ts, _ = x_sample.shape
    depth = w_ada.shape[0]
    n_p, n_s = bp * tp, bs * ts
    xp = x_prompt.reshape(n_p, d)
    xs = x_sample.reshape(n_s, d)

    mod_rows = 16
    cvec = jnp.zeros((mod_rows, d), F32).at[0].set(c_ctx).at[1:1 + bs].set(c)
    rope_tabs = _grid_rope_tables(ts)
    cap_p = CAPACITY_FACTOR * tp // N_EXPERTS
    cap_s = CAPACITY_FACTOR * ts // N_EXPERTS

    new_re, new_im, new_ret = [], [], []
    for l in range(depth):
        mods = _ada_call(cvec, w_ada[l], b_ada[l])
        shift1, scale1, gate1, shift2, scale2, gate2 = [m.reshape(mod_rows, 1, d) for m in jnp.split(mods, 6, axis=-1)]

        h = _norm_mod_call(xp, xs, scale1, shift1, norm1[l].reshape(1, d), ts)
        u_rows = _uproj_call(h, w_in[l])
        proj = _proj_call(h, w_in[l], S5_WIDTH)

        ws_cmp, wy_cmp, a_step, d_rows = _s5_chunk_weights(
            s5_a_re[l], s5_a_im[l], s5_log_dt[l], s5_b_re[l], s5_b_im[l], s5_c_re[l], s5_c_im[l], s5_d[l])
        h0_s = _state_to_lanes(state_s5_re[:, l].astype(F32), state_s5_im[:, l].astype(F32))
        s5_paths = ((0, bp, tp // S5_CHUNK, False), (n_p // S5_CHUNK, bs, ts // S5_CHUNK, True))
        yg, hfin_p = _s5_call(u_rows, ws_cmp, wy_cmp, a_step, d_rows, h0_s, s5_paths)
        s5_re, s5_im = _lanes_to_state(hfin_p)
        new_re.append(s5_re)
        new_im.append(s5_im)

        log_gamma = jax.nn.log_sigmoid(ret_decay_logit[l].astype(F32))
        gn_w = ret_gn_w[l].reshape(1, -1).astype(F32)
        o_p, sfin = _ret_call(proj, log_gamma, gn_w, None, None, row0=0, nseq=bp, t=tp, group=4, latent=False,
                              want_state=True)
        o_s, = _ret_call(proj, log_gamma, gn_w, rope_tabs, state_ret[:, l].astype(F32), row0=n_p, nseq=bs, t=ts,
                         group=2, latent=True, want_state=False)
        new_ret.append(sfin)
        merged = _merge_call(yg, o_p, o_s, proj, w_s5_glu[l], w_ret_out[l])
        w_router_pad = jnp.zeros((d, LANES), F32).at[:, :N_EXPERTS].set(w_router[l].astype(F32))
        x1, h2, aff = _outproj_call(merged, xp, xs, w_out[l].astype(BF16), gate1, scale2, shift2,
                                    norm2[l].reshape(1, d), w_router_pad, ts)

        aff = aff[:, :N_EXPERTS]
        aff_p = jnp.transpose(aff[:n_p].reshape(bp, tp, N_EXPERTS), (0, 2, 1))
        aff_s = jnp.transpose(aff[n_p:].reshape(bs, ts, N_EXPERTS), (0, 2, 1))
        slot_p = _select_call(aff_p.reshape(bp * N_EXPERTS, tp), cap_p).reshape(bp, N_EXPERTS, tp)
        slot_s = _select_call(aff_s.reshape(bs * N_EXPERTS, ts), cap_s).reshape(bs, N_EXPERTS, ts)
        xe_p, ge_p = _gather_call(slot_p, aff_p, h2, row0=0, nseq=bp, t=tp, cap=cap_p)
        xe_s, ge_s = _gather_call(slot_s, aff_s, h2, row0=n_p, nseq=bs, t=ts, cap=cap_s)
        ye_p, ye_s = _ffn_call(xe_p, xe_s, ge_p, ge_s, w_exp_gate[l], w_exp_up[l], w_exp_down[l])

        last = l == depth - 1
        wn = final_norm.reshape(1, d).astype(F32) if last else None
        assert last, "only the final layer applies the output norm in the scatter kernel"
        yp = _scatter_call(jnp.transpose(slot_p, (0, 2, 1)), ye_p, x1, gate2, wn, row0=0, nseq=bp, t=tp, cap=cap_p,
                           mod_row0=0, mod_per_seq=0)
        ysm = _scatter_call(jnp.transpose(slot_s, (0, 2, 1)), ye_s, x1, gate2, wn, row0=n_p, nseq=bs, t=ts, cap=cap_s,
                            mod_row0=1, mod_per_seq=1)

    y_prompt = yp.reshape(bp, tp, d)
    y_sample = ysm.reshape(bs, ts, d)
    return (y_prompt, y_sample, jnp.stack(new_re, axis=1), jnp.stack(new_im, axis=1), jnp.stack(new_ret, axis=1))
```

```python
import functools
import math

import jax
import jax.numpy as jnp
from jax import lax
from jax.experimental import pallas as pl
from jax.experimental.pallas import tpu as pltpu

F32 = jnp.float32
BF16 = jnp.bfloat16
HIGHEST = lax.Precision.HIGHEST

EPS = 1e-6
GRID_W = 64
S5_WIDTH = 1024
S5_GROUP = 16
S5_STATE = 64
RET_HEADS = 8
RET_DK = 128
RET_DV = 256
ROPE_BASE = 10000.0
RET_CHUNK = 256
N_EXPERTS = 16
CAPACITY_FACTOR = 2

LANES = 128
S5_CHUNK = 8
S5_GB = LANES // S5_GROUP
S5_NGB = S5_WIDTH // LANES
S5_SCOLS = 4 * S5_GB * S5_STATE
ROW_TILE = 256
VMEM_LIMIT = 56 * 1024 * 1024


def _params(*sem):
    return pltpu.CompilerParams(dimension_semantics=sem, vmem_limit_bytes=VMEM_LIMIT)


def _sigmoid(x):
    return 1.0 / (1.0 + jnp.exp(-x))


def _gelu_tanh(x):
    return 0.5 * x * (1.0 + jnp.tanh(math.sqrt(2.0 / math.pi) * (x + 0.044715 * (x * x * x))))


def _dot(a, b):
    return jnp.dot(a, b, preferred_element_type=F32)


def _ada_kernel(c_ref, w_ref, b_ref, o_ref):
    c = c_ref[...]
    s = c * _sigmoid(c)
    o_ref[...] = jnp.dot(s, w_ref[...], precision=HIGHEST, preferred_element_type=F32) + b_ref[...]


def _ada_call(cvec, w_ada, b_ada):
    r, d = cvec.shape
    n = w_ada.shape[1]
    tn = 1024 if n % 1024 == 0 else 512
    return pl.pallas_call(
        _ada_kernel,
        grid=(n // tn,),
        in_specs=[pl.BlockSpec((r, d), lambda j: (0, 0)),
                  pl.BlockSpec((d, tn), lambda j: (0, j)),
                  pl.BlockSpec((1, tn), lambda j: (0, j))],
        out_specs=pl.BlockSpec((r, tn), lambda j: (0, j)),
        out_shape=jax.ShapeDtypeStruct((r, n), F32),
        compiler_params=_params("arbitrary"),
        name="ada",
    )(cvec, w_ada, b_ada.reshape(1, n))


def _mod_row(i, n_prompt_tiles, tiles_per_sample_seq):
    return jnp.where(i < n_prompt_tiles, 0, 1 + (i - n_prompt_tiles) // tiles_per_sample_seq)


def _proj_kernel(a_ref, w_ref, o_ref, wb_ref):
    @pl.when(pl.program_id(1) == 0)
    def _():
        wb_ref[...] = w_ref[...].astype(BF16)

    o_ref[...] = _dot(a_ref[...], wb_ref[...]).astype(o_ref.dtype)


def _proj_call(a, w, col0):
    m, k = a.shape
    n = w.shape[1] - col0
    tm = 1024 if m % 1024 == 0 else m
    tn = 1024 if n % 1024 == 0 else 512
    cb0 = col0 // tn
    return pl.pallas_call(
        _proj_kernel,
        grid=(n // tn, m // tm),
        in_specs=[pl.BlockSpec((tm, k), lambda j, i: (i, 0)),
                  pl.BlockSpec((k, tn), lambda j, i: (0, cb0 + j))],
        out_specs=pl.BlockSpec((tm, tn), lambda j, i: (i, j)),
        out_shape=jax.ShapeDtypeStruct((m, n), BF16),
        scratch_shapes=[pltpu.VMEM((k, tn), BF16)],
        compiler_params=_params("arbitrary", "arbitrary"),
        name="in_proj",
    )(a, w)


def _cmul(ar, ai, br, bi):
    return ar * br - ai * bi, ar * bi + ai * br


def _uproj_kernel(xp_ref, xs_ref, sc_ref, sh_ref, nw_ref, w_ref, h_ref, o_ref, wb_ref, r_scr, *, n_prompt_tiles):
    i = pl.program_id(0)

    @pl.when(i == 0)
    def _():
        wb_ref[...] = w_ref[...].astype(BF16)

    def body(x_ref):
        x = x_ref[...]
        y = x * lax.rsqrt(jnp.mean(x * x, axis=-1, keepdims=True) + EPS)
        hb = ((y * nw_ref[...]) * (1.0 + sc_ref[...]) + sh_ref[...]).astype(BF16)
        h_ref[...] = hb
        res = _dot(hb, wb_ref[...])
        for g in range(S5_NGB):
            r_scr[g] = res[:, g * LANES:(g + 1) * LANES]

    @pl.when(i < n_prompt_tiles)
    def _():
        body(xp_ref)

    @pl.when(i >= n_prompt_tiles)
    def _():
        body(xs_ref)

    rows = h_ref.shape[0] // S5_CHUNK
    for g in range(S5_NGB):
        for tau in range(S5_CHUNK):
            c0 = (g * S5_CHUNK + tau) * LANES
            o_ref[:, c0:c0 + LANES] = r_scr[g, pl.ds(tau, rows, stride=S5_CHUNK), :].astype(o_ref.dtype)


def _uproj_call(xp, xs, scale, shift, nw, w, t_sample):
    n_p, d = xp.shape
    m = n_p + xs.shape[0]
    tm = 2 * ROW_TILE
    npt = n_p // tm
    row = functools.partial(_mod_row, n_prompt_tiles=npt, tiles_per_sample_seq=t_sample // tm)
    mod = pl.BlockSpec((None, 1, d), lambda i: (row(i), 0, 0))
    return pl.pallas_call(
        functools.partial(_uproj_kernel, n_prompt_tiles=npt),
        grid=(m // tm,),
        in_specs=[pl.BlockSpec((tm, d), lambda i: (jnp.minimum(i, npt - 1), 0)),
                  pl.BlockSpec((tm, d), lambda i: (jnp.maximum(i - npt, 0), 0)),
                  mod, mod,
                  pl.BlockSpec((1, d), lambda i: (0, 0)),
                  pl.BlockSpec((d, S5_WIDTH), lambda i: (0, 0))],
        out_specs=[pl.BlockSpec((tm, d), lambda i: (i, 0)),
                   pl.BlockSpec((tm // S5_CHUNK, S5_CHUNK * S5_WIDTH), lambda i: (i, 0))],
        out_shape=[jax.ShapeDtypeStruct((m, d), BF16),
                   jax.ShapeDtypeStruct((m // S5_CHUNK, S5_CHUNK * S5_WIDTH), BF16)],
        scratch_shapes=[pltpu.VMEM((d, S5_WIDTH), BF16), pltpu.VMEM((S5_NGB, tm, LANES), F32)],
        compiler_params=_params("arbitrary"),
        name="norm_u_proj",
    )(xp, xs, scale, shift, nw, w)


def _s5_chunk_weights(a_re, a_im, log_dt, b_re, b_im, c_re, c_im, d_skip):
    L, G, P, HG, GB, NGB = S5_CHUNK, S5_WIDTH // S5_GROUP, S5_STATE, S5_GROUP, S5_GB, S5_NGB
    a_re, a_im = a_re.astype(F32), a_im.astype(F32)
    dt = jnp.exp(log_dt.astype(F32))[..., None]
    adt_re, adt_im = a_re * dt, a_im * dt
    mag = jnp.exp(adt_re)
    lam_re, lam_im = mag * jnp.cos(adt_im), mag * jnp.sin(adt_im)
    den = a_re * a_re + a_im * a_im
    q_re = ((lam_re - 1.0) * a_re + lam_im * a_im) / den
    q_im = (lam_im * a_re - (lam_re - 1.0) * a_im) / den
    bb_re, bb_im = _cmul(q_re[..., None], q_im[..., None], b_re.astype(F32), b_im.astype(F32))
    n = jnp.arange(L + 1, dtype=F32)
    pmag = jnp.exp(adt_re[..., None] * n)
    pw_re, pw_im = pmag * jnp.cos(adt_im[..., None] * n), pmag * jnp.sin(adt_im[..., None] * n)

    def ws_exponents(pw, d):
        e = pw[0][..., :L][..., ::-1] if d == 0 else pw[1][..., :L]
        return jnp.transpose(e, (2, 0, 1))[:, :, None]

    def ws_dir(d):
        bt_re, bt_im = jnp.transpose(bb_re[d], (0, 2, 1)), jnp.transpose(bb_im[d], (0, 2, 1))
        return _cmul(ws_exponents(pw_re, d), ws_exponents(pw_im, d), bt_re[None], bt_im[None])

    (wf_re, wf_im), (wb_re, wb_im) = ws_dir(0), ws_dir(1)
    ws = jnp.concatenate([wf_re, wb_re, wf_im, wb_im], axis=-1).reshape(L, NGB, GB * HG, 4 * P)

    def wy_exponents(pw):
        return jnp.stack([pw[0], pw[1][..., ::-1]])

    ct_re = jnp.transpose(c_re.astype(F32), (0, 1, 3, 2))[:, :, :, None]
    ct_im = jnp.transpose(c_im.astype(F32), (0, 1, 3, 2))[:, :, :, None]
    y_re, y_im = _cmul(wy_exponents(pw_re)[..., None], wy_exponents(pw_im)[..., None], ct_re, ct_im)
    wy = jnp.stack([y_re, -y_im]).reshape(2, 2, NGB, GB * P, (L + 1) * HG)

    def lanes(x):
        return jnp.transpose(x.reshape(2, NGB, GB * P), (1, 0, 2)).reshape(NGB, 1, 2 * GB * P)

    a_step = jnp.concatenate([lanes(pw_re[..., L]), lanes(pw_im[..., L])], axis=1)
    return ws.astype(BF16), wy.astype(BF16), a_step, d_skip.astype(F32).reshape(NGB, 1, LANES)


def _s5_expand(xws_ref, zwy_ref, d_ref, wm, wws, wwy):
    L, hg, p = S5_CHUNK, S5_GROUP, S5_STATE
    lg_hg, lg_p, lg_lanes = hg.bit_length() - 1, p.bit_length() - 1, LANES.bit_length() - 1
    gmask = S5_GB - 1
    rblk = 512

    def expand(x, src_of_col, row_group, col_group, o_ref, cblk):
        rows, k = x.shape
        for c0 in range(0, o_ref.shape[1], cblk):
            kk = lax.broadcasted_iota(jnp.int32, (k, cblk), 0)
            cc = lax.broadcasted_iota(jnp.int32, (k, cblk), 1) + c0
            spread = jnp.where(kk == src_of_col(cc), 1.0, 0.0).astype(BF16)
            rr = lax.broadcasted_iota(jnp.int32, (rblk, cblk), 0)
            c2 = lax.broadcasted_iota(jnp.int32, (rblk, cblk), 1) + c0
            keep = row_group(rr) == col_group(c2)
            for r0 in range(0, rows, rblk):
                full = _dot(x[r0:r0 + rblk, :], spread)
                o_ref[r0:r0 + rblk, c0:c0 + cblk] = jnp.where(keep, full, 0.0).astype(o_ref.dtype)

    lane_group = lambda i: jnp.right_shift(i, lg_hg) & gmask
    state_group = lambda i: jnp.right_shift(i, lg_p) & gmask
    chunk_src = lambda c: jnp.right_shift(c, lg_lanes) * hg + (c & (hg - 1))
    state_src = lambda c: jnp.right_shift(c, lg_p + 3) * p + (c & (p - 1))
    xws = xws_ref[...]
    expand(xws.reshape(L * LANES, xws.shape[-1]), state_src, lane_group, state_group, wws, 1024)
    zwy = zwy_ref[...]
    expand(zwy.reshape(S5_SCOLS, zwy.shape[-1]), chunk_src, state_group, lane_group, wwy, wwy.shape[1])

    h = S5_SCOLS // 4
    last, first = wws[(L - 1) * LANES:L * LANES, :], wws[0:LANES, :]
    pf = _dot(last[:, 0:h], wwy[0:h, :]) + _dot(last[:, 2 * h:3 * h], wwy[2 * h:3 * h, :])
    pb = _dot(first[:, h:2 * h], wwy[h:2 * h, :]) + _dot(first[:, 3 * h:4 * h], wwy[3 * h:4 * h, :])
    eye = lax.broadcasted_iota(jnp.int32, (LANES, LANES), 0) == lax.broadcasted_iota(jnp.int32, (LANES, LANES), 1)
    p0 = pf[:, 0:LANES] + pb[:, L * LANES:(L + 1) * LANES] + jnp.where(eye, d_ref[...], 0.0)
    for s in range(L):
        for t in range(L):
            if t > s:
                blk = pf[:, (t - s) * LANES:(t - s + 1) * LANES]
            elif t < s:
                blk = pb[:, (L - (s - t)) * LANES:(L - (s - t) + 1) * LANES]
            else:
                blk = p0
            wm[s * LANES:(s + 1) * LANES, t * LANES:(t + 1) * LANES] = blk.astype(wm.dtype)


def _s5_kernel(u_ref, xws_ref, zwy_ref, d_ref, a_ref, h0_ref, y_ref, hfin_ref,
               wm, wws, wwy, s_scr, hb_scr, y_scr, *, paths):
    nu = S5_CHUNK
    half = S5_SCOLS // 4
    nsl = half // LANES
    cw = 2 * LANES

    _s5_expand(xws_ref, zwy_ref, d_ref, wm, wws, wwy)
    a = a_ref[...]

    for row0, nseq, nchunk, has_h0 in paths:
        r = nseq * nchunk

        for q in range(4):
            sq = _dot(u_ref[row0:row0 + r, :], wws[:, q * half:(q + 1) * half])
            for k in range(nsl):
                for s in range(nseq):
                    s_scr[q * nsl + k, pl.ds(s, nchunk, stride=nseq), :] = (
                        sq[s * nchunk:(s + 1) * nchunk, k * LANES:(k + 1) * LANES])

        if has_h0:
            h0 = h0_ref[...]
            carry0 = tuple(h0[:, j * LANES:(j + 1) * LANES] for j in range(4 * nsl))
        else:
            carry0 = tuple(jnp.zeros((nseq, LANES), F32) for _ in range(4 * nsl))

        def step(i, carry, nseq=nseq, nchunk=nchunk):
            rf = pl.multiple_of(i * nseq, nseq)
            rb = pl.multiple_of((nchunk - 1 - i) * nseq, nseq)
            new = list(carry)
            for k in range(nsl):
                for re, im, rows, acol in ((k, 2 * nsl + k, rf, k), (nsl + k, 3 * nsl + k, rb, nsl + k)):
                    s_re = s_scr[re, pl.ds(rows, nseq), :]
                    s_im = s_scr[im, pl.ds(rows, nseq), :]
                    h_re, h_im = carry[re], carry[im]
                    s_scr[re, pl.ds(rows, nseq), :] = h_re
                    s_scr[im, pl.ds(rows, nseq), :] = h_im
                    a_re = a[0:1, acol * LANES:(acol + 1) * LANES]
                    a_im = a[1:2, acol * LANES:(acol + 1) * LANES]
                    new[re] = a_re * h_re - a_im * h_im + s_re
                    new[im] = a_re * h_im + a_im * h_re + s_im
            return tuple(new)

        fin = lax.fori_loop(0, nchunk, step, carry0)
        if not has_h0:
            for j in range(4 * nsl):
                hfin_ref[:, j * LANES:(j + 1) * LANES] = fin[j]

        for j in range(4 * nsl):
            for s in range(nseq):
                hb_scr[s * nchunk:(s + 1) * nchunk, j * LANES:(j + 1) * LANES] = (
                    s_scr[j, pl.ds(s, nchunk, stride=nseq), :].astype(BF16))

        for j in range(nu * LANES // cw):
            c0 = j * cw
            y = _dot(u_ref[row0:row0 + r, :], wm[:, c0:c0 + cw])
            for q in range(4):
                off = c0 + (LANES if q % 2 == 0 else 0)
                y = y + _dot(hb_scr[0:r, q * half:(q + 1) * half], wwy[q * half:(q + 1) * half, off:off + cw])
            g = _gelu_tanh(y)
            for t2 in range(cw // LANES):
                tau = j * (cw // LANES) + t2
                y_scr[pl.ds(row0 * nu + tau, r, stride=nu), :] = g[:, t2 * LANES:(t2 + 1) * LANES]

    y_ref[...] = y_scr[...].astype(y_ref.dtype)


def _s5_call(u_rows, ws_cmp, wy_cmp, a_step, d_rows, h0, paths):
    rows = u_rows.shape[0]
    n_tok = rows * S5_CHUNK
    sc, k = S5_SCOLS, S5_CHUNK * LANES
    rmax = max(p[1] * p[2] for p in paths)
    nseq_fin = [p[1] for p in paths if not p[3]][0]
    blk = lambda a, b: pl.BlockSpec((None, a, b), lambda g: (g, 0, 0))
    return pl.pallas_call(
        functools.partial(_s5_kernel, paths=paths),
        grid=(S5_NGB,),
        in_specs=[pl.BlockSpec((rows, k), lambda g: (0, g)),
                  pl.BlockSpec((S5_CHUNK, None, LANES, ws_cmp.shape[-1]), lambda g: (0, g, 0, 0)),
                  pl.BlockSpec((2, 2, None, sc // 4, wy_cmp.shape[-1]), lambda g: (0, 0, g, 0, 0)),
                  blk(1, LANES), blk(2, sc // 2), blk(h0.shape[1], sc)],
        out_specs=[pl.BlockSpec((n_tok, LANES), lambda g: (0, g)), blk(nseq_fin, sc)],
        out_shape=[jax.ShapeDtypeStruct((n_tok, S5_WIDTH), BF16),
                   jax.ShapeDtypeStruct((S5_NGB, nseq_fin, sc), F32)],
        scratch_shapes=[pltpu.VMEM((k, k), BF16), pltpu.VMEM((k, sc), BF16),
                        pltpu.VMEM((sc, (S5_CHUNK + 1) * LANES), BF16),
                        pltpu.VMEM((4 * (sc // 4 // LANES), rmax, LANES), F32), pltpu.VMEM((rmax, sc), BF16),
                        pltpu.VMEM((n_tok, LANES), F32)],
        compiler_params=_params("arbitrary"),
        name="s5",
    )(u_rows, ws_cmp, wy_cmp, d_rows, a_step, h0)


def _state_to_lanes(s_re, s_im):
    b = s_re.shape[0]

    def one(x):
        return jnp.transpose(x.reshape(b, 2, S5_NGB, S5_GB * S5_STATE), (2, 0, 1, 3)).reshape(S5_NGB, b, -1)

    return jnp.concatenate([one(s_re), one(s_im)], axis=-1)


def _lanes_to_state(h):
    b = h.shape[1]
    x = h.reshape(S5_NGB, b, 2, 2, S5_GB, S5_STATE)
    x = jnp.transpose(x, (2, 1, 3, 0, 4, 5)).reshape(2, b, 2, S5_NGB * S5_GB, S5_STATE)
    return x[0], x[1]


def _ret_kernel(lg_ref, q_ref, k_ref, v_ref, g_ref, gn_ref, *rest, t, chunk, group, latent, want_state):
    rest = list(rest)
    if latent:
        cos_ref, sin_ref, s0_ref = rest[:3]
        rest = rest[3:]
    o_ref = rest.pop(0)
    sfin_ref = rest.pop(0) if want_state else None
    dmat_ref = rest.pop(0)
    n = t // chunk

    h = pl.program_id(0)
    lgf = lg_ref[0, h]
    lgb = lg_ref[1, h]

    @pl.when(pl.program_id(1) == 0)
    def _():
        ti = lax.broadcasted_iota(jnp.int32, (chunk, chunk), 0)
        si = lax.broadcasted_iota(jnp.int32, (chunk, chunk), 1)
        diff = (ti - si).astype(F32)
        dmat_ref[...] = jnp.exp(jnp.where(diff >= 0, lgf * diff, -lgb * diff))

    pos = lax.broadcasted_iota(jnp.int32, (chunk, 1), 0).astype(F32)
    q_dec_f = jnp.exp(lgf * (pos + 1.0))
    q_dec_b = jnp.exp(lgb * (chunk - pos))
    k_dec_f = jnp.exp(lgf * (chunk - 1.0 - pos))
    k_dec_b = jnp.exp(lgb * pos)
    step_f = jnp.exp(lgf * chunk)
    step_b = jnp.exp(lgb * chunk)
    tn = (((0,), (0,)), ((), ()))
    if latent:
        lane = lax.broadcasted_iota(jnp.int32, (chunk, RET_DK), 1)
        first = (lane % (RET_DK // 2)) < (RET_DK // 4)

        def rope(x, rows):
            swapped = jnp.where(first, pltpu.roll(x, RET_DK - RET_DK // 4, 1), pltpu.roll(x, RET_DK // 4, 1))
            return x * cos_ref[rows, :] + swapped * sin_ref[rows, :]

    for j in range(group):
        qs, kf, kr, vs, outs = [], [], [], [], []
        for c in range(n):
            rows = slice(j * t + c * chunk, j * t + (c + 1) * chunk)
            q = q_ref[rows, :].astype(F32) * (RET_DK ** -0.5)
            k = k_ref[rows, :].astype(F32)
            if latent:
                q = rope(q, slice(c * chunk, (c + 1) * chunk))
                k = rope(k, slice(c * chunk, (c + 1) * chunk))
            qb = q.astype(BF16)
            v = v_ref[rows, :]
            scores = lax.dot_general(qb, k.astype(BF16), (((1,), (1,)), ((), ())), preferred_element_type=F32)
            outs.append(_dot((scores * dmat_ref[...]).astype(BF16), v))
            qs.append(qb)
            vs.append(v)
            kf.append((k * k_dec_f).astype(BF16))
            kr.append((k * k_dec_b).astype(BF16))

        state = s0_ref[j, 0] if latent else None
        for c in range(n):
            if state is not None:
                outs[c] = outs[c] + _dot(qs[c], state.astype(BF16)) * q_dec_f
            if c < n - 1 or want_state:
                upd = lax.dot_general(kf[c], vs[c], tn, preferred_element_type=F32)
                state = upd if state is None else step_f * state + upd
        if want_state:
            sfin_ref[j, 0] = state

        state = s0_ref[j, 1] if latent else None
        for c in reversed(range(n)):
            if state is not None:
                outs[c] = outs[c] + _dot(qs[c], state.astype(BF16)) * q_dec_b
            if c > 0 or want_state:
                upd = lax.dot_general(kr[c], vs[c], tn, preferred_element_type=F32)
                state = upd if state is None else step_b * state + upd
        if want_state:
            sfin_ref[j, 1] = state

        for c in range(n):
            rows = slice(j * t + c * chunk, j * t + (c + 1) * chunk)
            o = outs[c]
            mu = jnp.mean(o, axis=-1, keepdims=True)
            d = o - mu
            var = jnp.mean(d * d, axis=-1, keepdims=True)
            on = d * lax.rsqrt(var + EPS) * gn_ref[...]
            g = g_ref[rows, :].astype(F32)
            o_ref[rows, :] = (g * _sigmoid(g) * on).astype(o_ref.dtype)


def _ret_call(proj, log_gamma, gn_w, rope_tabs, s0, *, row0, nseq, t, chunk, group, latent, want_state):
    h, dk, dv = RET_HEADS, RET_DK, RET_DV
    gt = group * t
    rb0 = row0 // gt
    q0 = 0
    k0 = q0 + h
    v0 = (2 * h * dk) // dv
    g0 = v0 + h
    in_specs = [pl.BlockSpec(memory_space=pltpu.SMEM),
                pl.BlockSpec((gt, dk), lambda hh, b: (rb0 + b, q0 + hh)),
                pl.BlockSpec((gt, dk), lambda hh, b: (rb0 + b, k0 + hh)),
                pl.BlockSpec((gt, dv), lambda hh, b: (rb0 + b, v0 + hh)),
                pl.BlockSpec((gt, dv), lambda hh, b: (rb0 + b, g0 + hh)),
                pl.BlockSpec((1, dv), lambda hh, b: (0, hh))]
    args = [log_gamma, proj, proj, proj, proj, gn_w]
    if latent:
        in_specs += [pl.BlockSpec((t, dk), lambda hh, b: (0, 0)),
                     pl.BlockSpec((t, dk), lambda hh, b: (0, 0)),
                     pl.BlockSpec((group, 2, None, dk, dv), lambda hh, b: (b, 0, hh, 0, 0))]
        args += [rope_tabs[0], rope_tabs[1], s0]
    out_specs = [pl.BlockSpec((gt, dv), lambda hh, b: (b, hh))]
    out_shape = [jax.ShapeDtypeStruct((nseq * t, h * dv), BF16)]
    if want_state:
        out_specs.append(pl.BlockSpec((group, 2, None, dk, dv), lambda hh, b: (b, 0, hh, 0, 0)))
        out_shape.append(jax.ShapeDtypeStruct((nseq, 2, h, dk, dv), F32))
    return pl.pallas_call(
        functools.partial(_ret_kernel, t=t, chunk=chunk, group=group, latent=latent, want_state=want_state),
        grid=(h, nseq // group),
        in_specs=in_specs,
        out_specs=out_specs,
        out_shape=out_shape,
        scratch_shapes=[pltpu.VMEM((chunk, chunk), F32)],
        compiler_params=_params("arbitrary", "arbitrary"),
        name="retention_latent" if latent else "retention_ctx",
    )(*args)


def _grid_rope_tables(t):
    quarter = RET_DK // 4
    freqs = ROPE_BASE ** (-jnp.arange(quarter, dtype=F32) / quarter)
    pos = jnp.arange(t)
    row = (pos // GRID_W).astype(F32)[:, None] * freqs[None, :]
    col = (pos % GRID_W).astype(F32)[:, None] * freqs[None, :]
    cos = jnp.concatenate([jnp.cos(row), jnp.cos(row), jnp.cos(col), jnp.cos(col)], axis=-1)
    sin = jnp.concatenate([-jnp.sin(row), jnp.sin(row), -jnp.sin(col), jnp.sin(col)], axis=-1)
    return cos, sin


def _merge_kernel(yg_ref, op_ref, os_ref, ga_ref, gb_ref, wga_ref, wgb_ref, wr_ref, out_ref, wga_b, wgb_b, wr_b,
                  *, n_prompt_tiles):
    i = pl.program_id(1)

    @pl.when(i == 0)
    def _():
        wga_b[...] = wga_ref[...].astype(BF16)
        wgb_b[...] = wgb_ref[...].astype(BF16)
        wr_b[...] = wr_ref[...].astype(BF16)

    def body(o_ref):
        yg = yg_ref[...]
        za = _dot(yg, wga_b[...])
        zb = _dot(yg, wgb_b[...])
        ob = _dot(o_ref[...], wr_b[...])
        out_a = za * _sigmoid(zb)
        ga = ga_ref[...].astype(F32)
        gb = gb_ref[...].astype(F32)
        out_ref[...] = (_sigmoid(ga) * out_a + _sigmoid(gb) * ob).astype(out_ref.dtype)

    @pl.when(i < n_prompt_tiles)
    def _():
        body(op_ref)

    @pl.when(i >= n_prompt_tiles)
    def _():
        body(os_ref)


def _merge_call(yg, o_p, o_s, proj, w_glu, w_ret_out):
    m, ks = yg.shape
    kr = o_p.shape[1]
    d = w_ret_out.shape[1]
    tm = 1024 if m % 1024 == 0 else m
    tn = 512 if d % 512 == 0 else d
    ga0 = (proj.shape[1] - 2 * d) // tn
    gb0 = (proj.shape[1] - d) // tn
    nb = d // tn
    npt = o_p.shape[0] // tm
    return pl.pallas_call(
        functools.partial(_merge_kernel, n_prompt_tiles=npt),
        grid=(nb, m // tm),
        in_specs=[pl.BlockSpec((tm, ks), lambda j, i: (i, 0)),
                  pl.BlockSpec((tm, kr), lambda j, i: (jnp.minimum(i, npt - 1), 0)),
                  pl.BlockSpec((tm, kr), lambda j, i: (jnp.maximum(i - npt, 0), 0)),
                  pl.BlockSpec((tm, tn), lambda j, i: (i, ga0 + j)),
                  pl.BlockSpec((tm, tn), lambda j, i: (i, gb0 + j)),
                  pl.BlockSpec((ks, tn), lambda j, i: (0, j)),
                  pl.BlockSpec((ks, tn), lambda j, i: (0, nb + j)),
                  pl.BlockSpec((kr, tn), lambda j, i: (0, j))],
        out_specs=pl.BlockSpec((tm, tn), lambda j, i: (i, j)),
        out_shape=jax.ShapeDtypeStruct((m, d), BF16),
        scratch_shapes=[pltpu.VMEM((ks, tn), BF16), pltpu.VMEM((ks, tn), BF16), pltpu.VMEM((kr, tn), BF16)],
        compiler_params=_params("arbitrary", "arbitrary"),
        name="merge",
    )(yg, o_p, o_s, proj, proj, w_glu, w_glu, w_ret_out)


def _outproj_kernel(mg_ref, xp_ref, xs_ref, w_ref, g1_ref, sc_ref, sh_ref, n2_ref, wr_ref,
                    x1_ref, h2_ref, aff_ref, *, n_prompt_tiles, n_experts):
    i = pl.program_id(0)
    upd = g1_ref[...] * _dot(mg_ref[...], w_ref[...])

    def body(x_ref):
        x1 = x_ref[...] + upd
        x1_ref[...] = x1
        y = x1 * lax.rsqrt(jnp.mean(x1 * x1, axis=-1, keepdims=True) + EPS)
        h2 = (y * n2_ref[...]) * (1.0 + sc_ref[...]) + sh_ref[...]
        h2_ref[...] = h2.astype(h2_ref.dtype)
        wr = wr_ref[...]
        w_hi = wr.astype(BF16)
        w_lo = (wr - w_hi.astype(F32)).astype(BF16)
        h_hi = h2.astype(BF16)
        h_lo = (h2 - h_hi.astype(F32)).astype(BF16)
        hw = _dot(h_hi, jnp.concatenate([w_hi, w_lo], axis=1))
        logits = hw[:, :LANES] + (hw[:, LANES:] + _dot(h_lo, w_hi))
        lane = lax.broadcasted_iota(jnp.int32, logits.shape, 1)
        logits = jnp.where(lane < n_experts, logits, -jnp.inf)
        e = jnp.exp(logits - jnp.max(logits, axis=-1, keepdims=True))
        aff_ref[...] = e / jnp.sum(e, axis=-1, keepdims=True)

    @pl.when(i < n_prompt_tiles)
    def _():
        body(xp_ref)

    @pl.when(i >= n_prompt_tiles)
    def _():
        body(xs_ref)


def _outproj_call(merged, xp, xs, w_out_b, gate1, scale2, shift2, norm2, w_router_pad, t_sample):
    m, d = merged.shape
    n_p = xp.shape[0]
    tm = 2 * ROW_TILE
    npt = n_p // tm
    tps = t_sample // tm
    row = functools.partial(_mod_row, n_prompt_tiles=npt, tiles_per_sample_seq=tps)
    mod = pl.BlockSpec((None, 1, d), lambda i: (row(i), 0, 0))
    return pl.pallas_call(
        functools.partial(_outproj_kernel, n_prompt_tiles=npt, n_experts=N_EXPERTS),
        grid=(m // tm,),
        in_specs=[pl.BlockSpec((tm, d), lambda i: (i, 0)),
                  pl.BlockSpec((tm, d), lambda i: (jnp.minimum(i, npt - 1), 0)),
                  pl.BlockSpec((tm, d), lambda i: (jnp.maximum(i - npt, 0), 0)),
                  pl.BlockSpec((d, d), lambda i: (0, 0)),
                  mod, mod, mod,
                  pl.BlockSpec((1, d), lambda i: (0, 0)),
                  pl.BlockSpec((d, LANES), lambda i: (0, 0))],
        out_specs=[pl.BlockSpec((tm, d), lambda i: (i, 0)),
                   pl.BlockSpec((tm, d), lambda i: (i, 0)),
                   pl.BlockSpec((tm, LANES), lambda i: (i, 0))],
        out_shape=[jax.ShapeDtypeStruct((m, d), F32),
                   jax.ShapeDtypeStruct((m, d), BF16),
                   jax.ShapeDtypeStruct((m, LANES), F32)],
        compiler_params=_params("arbitrary"),
        name="out_proj_router",
    )(merged, xp, xs, w_out_b, gate1, scale2, shift2, norm2, w_router_pad)


def _select_kernel(a_ref, slot_ref, *, cap):
    a = a_ref[...]
    r, t = a.shape

    def as_float(bits):
        return pltpu.bitcast(bits, F32)

    def bisect(_, carry):
        lo, hi = carry
        mid = lo + jnp.right_shift(hi - lo + 1, 1)
        cnt = jnp.sum(jnp.where(a >= as_float(mid), 1.0, 0.0), axis=-1, keepdims=True)
        ok = cnt >= cap
        return jnp.where(ok, mid, lo), jnp.where(ok, hi, mid - 1)

    lo0 = jnp.zeros((r, 1), jnp.int32)
    hi0 = jnp.full((r, 1), 0x3F800000, jnp.int32)
    thr, _ = lax.fori_loop(0, 31, bisect, (lo0, hi0))

    gt = jnp.where(a >= as_float(thr + 1), 1.0, 0.0)
    eq = jnp.where(a >= as_float(thr), 1.0, 0.0) - gt
    need = cap - jnp.sum(gt, axis=-1, keepdims=True)
    before = lax.broadcasted_iota(jnp.int32, (t, t), 0) < lax.broadcasted_iota(jnp.int32, (t, t), 1)
    tri = jnp.where(before, 1.0, 0.0).astype(BF16)
    eq_rank = _dot(eq.astype(BF16), tri)
    sel = gt + eq * jnp.where(eq_rank < need, 1.0, 0.0)
    pos = _dot(sel.astype(BF16), tri)
    slot_ref[...] = jnp.where(sel > 0.5, pos, -1.0).astype(jnp.int32)


def _select_call(aff_t, cap):
    r, t = aff_t.shape
    return pl.pallas_call(
        functools.partial(_select_kernel, cap=cap),
        grid=(1,),
        in_specs=[pl.BlockSpec((r, t), lambda i: (0, 0))],
        out_specs=pl.BlockSpec((r, t), lambda i: (0, 0)),
        out_shape=jax.ShapeDtypeStruct((r, t), jnp.int32),
        compiler_params=_params("arbitrary"),
        name="select",
    )(aff_t)


def _gather_kernel(slot_ref, aff_ref, h_ref, xs_ref, gate_ref, *, cap, group):
    e_total, t = slot_ref.shape
    h = h_ref[...]
    ci = lax.broadcasted_iota(jnp.int32, (cap, t), 0)
    for e0 in range(0, e_total, group):
        hots = []
        for e in range(e0, e0 + group):
            hit = ci == slot_ref[e:e + 1, :]
            hots.append(jnp.where(hit, 1.0, 0.0).astype(BF16))
            gate_ref[e] = jnp.sum(jnp.where(hit, aff_ref[e:e + 1, :], 0.0), axis=-1, keepdims=True)
        onehot = hots[0] if group == 1 else jnp.concatenate(hots, axis=0)
        xs = _dot(onehot, h).astype(xs_ref.dtype)
        xs_ref[e0:e0 + group] = xs.reshape(group, cap, xs.shape[-1])


def _gather_call(slot_t, aff_t, h2, *, row0, nseq, t, cap):
    e = slot_t.shape[1]
    d = h2.shape[1]
    rb0 = row0 // t
    group = max(1, min(e, 512 // cap))
    return pl.pallas_call(
        functools.partial(_gather_kernel, cap=cap, group=group),
        grid=(nseq,),
        in_specs=[pl.BlockSpec((None, e, t), lambda b: (b, 0, 0)),
                  pl.BlockSpec((None, e, t), lambda b: (b, 0, 0)),
                  pl.BlockSpec((t, d), lambda b: (rb0 + b, 0))],
        out_specs=[pl.BlockSpec((e, cap, d), lambda b: (0, b, 0)),
                   pl.BlockSpec((e, cap, 1), lambda b: (0, b, 0))],
        out_shape=[jax.ShapeDtypeStruct((e, nseq * cap, d), BF16),
                   jax.ShapeDtypeStruct((e, nseq * cap, 1), F32)],
        compiler_params=_params("arbitrary"),
        name="gather",
    )(slot_t, aff_t, h2)


def _ffn_kernel(xp_ref, xs_ref, gp_ref, gs_ref, wg_ref, wu_ref, wd_ref, yp_ref, ys_ref,
                accp, accs, wgb, wub, wdb, *, chunk):
    f = pl.program_id(1)

    @pl.when(f == 0)
    def _():
        accp[...] = jnp.zeros_like(accp)
        accs[...] = jnp.zeros_like(accs)

    wgb[...] = wg_ref[...].astype(BF16)
    wub[...] = wu_ref[...].astype(BF16)
    wdb[...] = wd_ref[...].astype(BF16)

    def part(x_ref, acc):
        m = x_ref.shape[0]
        mc = min(chunk, m)
        for m0 in range(0, m, mc):
            x = x_ref[m0:m0 + mc, :]
            hg = _dot(x, wgb[...])
            hu = _dot(x, wub[...])
            hid = (hg * _sigmoid(hg) * hu).astype(BF16)
            acc[m0:m0 + mc, :] += _dot(hid, wdb[...])

    part(xp_ref, accp)
    part(xs_ref, accs)

    @pl.when(f == pl.num_programs(1) - 1)
    def _():
        yp_ref[...] = (accp[...] * gp_ref[...]).astype(yp_ref.dtype)
        ys_ref[...] = (accs[...] * gs_ref[...]).astype(ys_ref.dtype)


def _ffn_call(xs_p, xs_s, gate_p, gate_s, w_gate, w_up, w_down):
    e, mp, d = xs_p.shape
    ms = xs_s.shape[1]
    ff = w_gate.shape[2]
    tf = 256 if ff % 256 == 0 else ff
    tok = lambda m, w: pl.BlockSpec((None, m, w), lambda ee, f: (ee, 0, 0))
    return pl.pallas_call(
        functools.partial(_ffn_kernel, chunk=512),
        grid=(e, ff // tf),
        in_specs=[tok(mp, d), tok(ms, d), tok(mp, 1), tok(ms, 1),
                  pl.BlockSpec((None, d, tf), lambda ee, f: (ee, 0, f)),
                  pl.BlockSpec((None, d, tf), lambda ee, f: (ee, 0, f)),
                  pl.BlockSpec((None, tf, d), lambda ee, f: (ee, f, 0))],
        out_specs=[pl.BlockSpec((None, mp, d), lambda ee, f: (ee, 0, 0), pipeline_mode=pl.Buffered(1)),
                   pl.BlockSpec((None, ms, d), lambda ee, f: (ee, 0, 0), pipeline_mode=pl.Buffered(1))],
        out_shape=[jax.ShapeDtypeStruct((e, mp, d), BF16), jax.ShapeDtypeStruct((e, ms, d), BF16)],
        scratch_shapes=[pltpu.VMEM((mp, d), F32), pltpu.VMEM((ms, d), F32),
                        pltpu.VMEM((d, tf), BF16), pltpu.VMEM((d, tf), BF16), pltpu.VMEM((tf, d), BF16)],
        compiler_params=_params("arbitrary", "arbitrary"),
        name="expert_ffn",
    )(xs_p, xs_s, gate_p, gate_s, w_gate, w_up, w_down)


def _scatter_kernel(slot_ref, y_ref, x1_ref, g2_ref, wn_ref, o_ref, *, cap):
    e_total = y_ref.shape[0]
    tm = slot_ref.shape[0]
    slot = slot_ref[...]
    lane = lax.broadcasted_iota(jnp.int32, (tm, LANES), 1)
    per_block = max(1, LANES // cap)
    blocks = []
    for b0 in range(0, e_total, per_block):
        acc = jnp.zeros((tm, LANES), F32)
        for j in range(per_block):
            s = slot[:, b0 + j:b0 + j + 1]
            key = jnp.where(s >= 0, s + j * cap, -1)
            acc = acc + jnp.where(lane == key, 1.0, 0.0)
        blocks.append(acc.astype(BF16))
    onehot = jnp.concatenate(blocks, axis=1)
    y = y_ref[...].reshape(e_total * cap, y_ref.shape[-1])
    moe = _dot(onehot, y)
    x2 = x1_ref[...] + g2_ref[...] * moe
    o_ref[...] = x2 * lax.rsqrt(jnp.mean(x2 * x2, axis=-1, keepdims=True) + EPS) * wn_ref[...]


def _scatter_call(slot, y, x1, gate2, final_norm, *, row0, nseq, t, cap, mod_row0, mod_per_seq):
    e, _, d = y.shape
    tm = ROW_TILE
    nt = t // tm
    rb0 = row0 // tm
    assert cap == LANES or LANES % cap == 0
    return pl.pallas_call(
        functools.partial(_scatter_kernel, cap=cap),
        grid=(nseq, nt),
        in_specs=[pl.BlockSpec((None, tm, e), lambda b, i: (b, i, 0)),
                  pl.BlockSpec((e, cap, d), lambda b, i: (0, b, 0)),
                  pl.BlockSpec((tm, d), lambda b, i: (rb0 + b * nt + i, 0)),
                  pl.BlockSpec((None, 1, d), lambda b, i: (mod_row0 + b * mod_per_seq, 0, 0)),
                  pl.BlockSpec((1, d), lambda b, i: (0, 0))],
        out_specs=pl.BlockSpec((tm, d), lambda b, i: (b * nt + i, 0)),
        out_shape=jax.ShapeDtypeStruct((nseq * t, d), F32),
        compiler_params=_params("arbitrary", "arbitrary"),
        name="scatter_final",
    )(slot, y, x1, gate2, final_norm)


def kernel(x_prompt, x_sample, state_s5_re, state_s5_im, state_ret, c, c_ctx, final_norm, w_ada, b_ada, norm1, norm2, w_in, s5_a_re, s5_a_im, s5_log_dt, s5_b_re, s5_b_im, s5_c_re, s5_c_im, s5_d, w_s5_glu, ret_decay_logit, ret_gn_w, w_ret_out, w_out, w_router, w_exp_gate, w_exp_up, w_exp_down):
    bp, tp, d = x_prompt.shape
    bs, ts, _ = x_sample.shape
    depth = w_ada.shape[0]
    n_p, n_s = bp * tp, bs * ts
    xp = x_prompt.reshape(n_p, d)
    xs = x_sample.reshape(n_s, d)

    mod_rows = 16
    cvec = jnp.zeros((mod_rows, d), F32).at[0].set(c_ctx).at[1:1 + bs].set(c)
    rope_tabs = _grid_rope_tables(ts)
    cap_p = CAPACITY_FACTOR * tp // N_EXPERTS
    cap_s = CAPACITY_FACTOR * ts // N_EXPERTS

    new_re, new_im, new_ret = [], [], []
    for l in range(depth):
        mods = _ada_call(cvec, w_ada[l], b_ada[l])
        shift1, scale1, gate1, shift2, scale2, gate2 = [m.reshape(mod_rows, 1, d) for m in jnp.split(mods, 6, axis=-1)]

        h, u_rows = _uproj_call(xp, xs, scale1, shift1, norm1[l].reshape(1, d), w_in[l], ts)
        proj = _proj_call(h, w_in[l], S5_WIDTH)

        ws_cmp, wy_cmp, a_step, d_rows = _s5_chunk_weights(
            s5_a_re[l], s5_a_im[l], s5_log_dt[l], s5_b_re[l], s5_b_im[l], s5_c_re[l], s5_c_im[l], s5_d[l])
        h0_s = _state_to_lanes(state_s5_re[:, l].astype(F32), state_s5_im[:, l].astype(F32))
        s5_paths = ((0, bp, tp // S5_CHUNK, False), (n_p // S5_CHUNK, bs, ts // S5_CHUNK, True))
        yg, hfin_p = _s5_call(u_rows, ws_cmp, wy_cmp, a_step, d_rows, h0_s, s5_paths)
        s5_re, s5_im = _lanes_to_state(hfin_p)
        new_re.append(s5_re)
        new_im.append(s5_im)

        log_gamma = jax.nn.log_sigmoid(ret_decay_logit[l].astype(F32))
        gn_w = ret_gn_w[l].reshape(1, -1).astype(F32)
        o_p, sfin = _ret_call(proj, log_gamma, gn_w, None, None, row0=0, nseq=bp, t=tp, chunk=RET_CHUNK, group=4, latent=False,
                              want_state=True)
        o_s, = _ret_call(proj, log_gamma, gn_w, rope_tabs, state_ret[:, l].astype(F32), row0=n_p, nseq=bs, t=ts,
                         chunk=RET_CHUNK, group=2, latent=True, want_state=False)
        new_ret.append(sfin)
        merged = _merge_call(yg, o_p, o_s, proj, w_s5_glu[l], w_ret_out[l])
        w_router_pad = jnp.zeros((d, LANES), F32).at[:, :N_EXPERTS].set(w_router[l].astype(F32))
        x1, h2, aff = _outproj_call(merged, xp, xs, w_out[l].astype(BF16), gate1, scale2, shift2,
                                    norm2[l].reshape(1, d), w_router_pad, ts)

        aff = aff[:, :N_EXPERTS]
        aff_p = jnp.transpose(aff[:n_p].reshape(bp, tp, N_EXPERTS), (0, 2, 1))
        aff_s = jnp.transpose(aff[n_p:].reshape(bs, ts, N_EXPERTS), (0, 2, 1))
        slot_p = _select_call(aff_p.reshape(bp * N_EXPERTS, tp), cap_p).reshape(bp, N_EXPERTS, tp)
        slot_s = _select_call(aff_s.reshape(bs * N_EXPERTS, ts), cap_s).reshape(bs, N_EXPERTS, ts)
        xe_p, ge_p = _gather_call(slot_p, aff_p, h2, row0=0, nseq=bp, t=tp, cap=cap_p)
        xe_s, ge_s = _gather_call(slot_s, aff_s, h2, row0=n_p, nseq=bs, t=ts, cap=cap_s)
        ye_p, ye_s = _ffn_call(xe_p, xe_s, ge_p, ge_s, w_exp_gate[l], w_exp_up[l], w_exp_down[l])

        last = l == depth - 1
        wn = final_norm.reshape(1, d).astype(F32) if last else None
        assert last, "only the final layer applies the output norm in the scatter kernel"
        yp = _scatter_call(jnp.transpose(slot_p, (0, 2, 1)), ye_p, x1, gate2, wn, row0=0, nseq=bp, t=tp, cap=cap_p,
                           mod_row0=0, mod_per_seq=0)
        ysm = _scatter_call(jnp.transpose(slot_s, (0, 2, 1)), ye_s, x1, gate2, wn, row0=n_p, nseq=bs, t=ts, cap=cap_s,
                            mod_row0=1, mod_per_seq=1)

    y_prompt = yp.reshape(bp, tp, d)
    y_sample = ysm.reshape(bs, ts, d)
    return (y_prompt, y_sample, jnp.stack(new_re, axis=1), jnp.stack(new_im, axis=1), jnp.stack(new_ret, axis=1))
```

```python
import functools
import math

import jax
import jax.numpy as jnp
from jax import lax
from jax.experimental import pallas as pl
from jax.experimental.pallas import tpu as pltpu

F32 = jnp.float32
BF16 = jnp.bfloat16

EPS = 1e-6
GRID_W = 64
S5_WIDTH = 1024
S5_GROUP = 16
S5_STATE = 64
RET_HEADS = 8
RET_DK = 128
RET_DV = 256
ROPE_BASE = 10000.0
RET_CHUNK = 256
N_EXPERTS = 16
CAPACITY_FACTOR = 2

LANES = 128
S5_CHUNK = 8
S5_GB = LANES // S5_GROUP
S5_NGB = S5_WIDTH // LANES
S5_SCOLS = 4 * S5_GB * S5_STATE
ROW_TILE = 256
VMEM_LIMIT = 56 * 1024 * 1024


def _params(*sem):
    return pltpu.CompilerParams(dimension_semantics=sem, vmem_limit_bytes=VMEM_LIMIT)


def _sigmoid(x):
    return 1.0 / (1.0 + jnp.exp(-x))


def _gelu_tanh(x):
    return 0.5 * x * (1.0 + jnp.tanh(math.sqrt(2.0 / math.pi) * (x + 0.044715 * (x * x * x))))


def _dot(a, b):
    return jnp.dot(a, b, preferred_element_type=F32)


def _ada_kernel(c_ref, w_ref, b_ref, o_ref):
    c = c_ref[...]
    s = c * _sigmoid(c)
    w = w_ref[...]
    s_hi = s.astype(BF16)
    s_lo = (s - s_hi.astype(F32)).astype(BF16)
    w_hi = w.astype(BF16)
    w_lo = (w - w_hi.astype(F32)).astype(BF16)
    o_ref[...] = _dot(s_hi, w_hi) + (_dot(s_hi, w_lo) + _dot(s_lo, w_hi)) + b_ref[...]


def _ada_call(cvec, w_ada, b_ada):
    r, d = cvec.shape
    n = w_ada.shape[1]
    tn = 1024 if n % 1024 == 0 else 512
    return pl.pallas_call(
        _ada_kernel,
        grid=(n // tn,),
        in_specs=[pl.BlockSpec((r, d), lambda j: (0, 0)),
                  pl.BlockSpec((d, tn), lambda j: (0, j)),
                  pl.BlockSpec((1, tn), lambda j: (0, j))],
        out_specs=pl.BlockSpec((r, tn), lambda j: (0, j)),
        out_shape=jax.ShapeDtypeStruct((r, n), F32),
        compiler_params=_params("arbitrary"),
        name="ada",
    )(cvec, w_ada, b_ada.reshape(1, n))


def _mod_row(i, n_prompt_tiles, tiles_per_sample_seq):
    return jnp.where(i < n_prompt_tiles, 0, 1 + (i - n_prompt_tiles) // tiles_per_sample_seq)


def _proj_kernel(a_ref, w_ref, o_ref, wb_ref):
    @pl.when(pl.program_id(1) == 0)
    def _():
        wb_ref[...] = w_ref[...].astype(BF16)

    o_ref[...] = _dot(a_ref[...], wb_ref[...]).astype(o_ref.dtype)


def _proj_call(a, w, col0):
    m, k = a.shape
    n = w.shape[1] - col0
    tm = next((c for c in (1536, 1024) if m % c == 0), m)
    tn = 1024 if n % 1024 == 0 else 512
    cb0 = col0 // tn
    return pl.pallas_call(
        _proj_kernel,
        grid=(n // tn, m // tm),
        in_specs=[pl.BlockSpec((tm, k), lambda j, i: (i, 0)),
                  pl.BlockSpec((k, tn), lambda j, i: (0, cb0 + j))],
        out_specs=pl.BlockSpec((tm, tn), lambda j, i: (i, j)),
        out_shape=jax.ShapeDtypeStruct((m, n), BF16),
        scratch_shapes=[pltpu.VMEM((k, tn), BF16)],
        compiler_params=_params("arbitrary", "arbitrary"),
        name="in_proj",
    )(a, w)


def _cmul(ar, ai, br, bi):
    return ar * br - ai * bi, ar * bi + ai * br


def _uproj_kernel(xp_ref, xs_ref, sc_ref, sh_ref, nw_ref, w_ref, h_ref, o_ref, wb_ref, r_scr, *, n_prompt_tiles):
    i = pl.program_id(0)

    @pl.when(i == 0)
    def _():
        wb_ref[...] = w_ref[...].astype(BF16)

    def body(x_ref):
        x = x_ref[...]
        y = x * lax.rsqrt(jnp.mean(x * x, axis=-1, keepdims=True) + EPS)
        hb = ((y * nw_ref[...]) * (1.0 + sc_ref[...]) + sh_ref[...]).astype(BF16)
        h_ref[...] = hb
        res = _dot(hb, wb_ref[...])
        for g in range(S5_NGB):
            r_scr[g] = res[:, g * LANES:(g + 1) * LANES]

    @pl.when(i < n_prompt_tiles)
    def _():
        body(xp_ref)

    @pl.when(i >= n_prompt_tiles)
    def _():
        body(xs_ref)

    rows = h_ref.shape[0] // S5_CHUNK
    for g in range(S5_NGB):
        for tau in range(S5_CHUNK):
            c0 = (g * S5_CHUNK + tau) * LANES
            o_ref[:, c0:c0 + LANES] = r_scr[g, pl.ds(tau, rows, stride=S5_CHUNK), :].astype(o_ref.dtype)


def _uproj_call(xp, xs, scale, shift, nw, w, t_sample):
    n_p, d = xp.shape
    m = n_p + xs.shape[0]
    tm = 2 * ROW_TILE
    npt = n_p // tm
    row = functools.partial(_mod_row, n_prompt_tiles=npt, tiles_per_sample_seq=t_sample // tm)
    mod = pl.BlockSpec((None, 1, d), lambda i: (row(i), 0, 0))
    return pl.pallas_call(
        functools.partial(_uproj_kernel, n_prompt_tiles=npt),
        grid=(m // tm,),
        in_specs=[pl.BlockSpec((tm, d), lambda i: (jnp.minimum(i, npt - 1), 0)),
                  pl.BlockSpec((tm, d), lambda i: (jnp.maximum(i - npt, 0), 0)),
                  mod, mod,
                  pl.BlockSpec((1, d), lambda i: (0, 0)),
                  pl.BlockSpec((d, S5_WIDTH), lambda i: (0, 0))],
        out_specs=[pl.BlockSpec((tm, d), lambda i: (i, 0)),
                   pl.BlockSpec((tm // S5_CHUNK, S5_CHUNK * S5_WIDTH), lambda i: (i, 0))],
        out_shape=[jax.ShapeDtypeStruct((m, d), BF16),
                   jax.ShapeDtypeStruct((m // S5_CHUNK, S5_CHUNK * S5_WIDTH), BF16)],
        scratch_shapes=[pltpu.VMEM((d, S5_WIDTH), BF16), pltpu.VMEM((S5_NGB, tm, LANES), F32)],
        compiler_params=_params("arbitrary"),
        name="norm_u_proj",
    )(xp, xs, scale, shift, nw, w)


def _s5_chunk_weights(a_re, a_im, log_dt, b_re, b_im, c_re, c_im, d_skip):
    L, G, P, HG, GB, NGB = S5_CHUNK, S5_WIDTH // S5_GROUP, S5_STATE, S5_GROUP, S5_GB, S5_NGB
    a_re, a_im = a_re.astype(F32), a_im.astype(F32)
    dt = jnp.exp(log_dt.astype(F32))[..., None]
    adt_re, adt_im = a_re * dt, a_im * dt
    mag = jnp.exp(adt_re)
    lam_re, lam_im = mag * jnp.cos(adt_im), mag * jnp.sin(adt_im)
    den = a_re * a_re + a_im * a_im
    q_re = ((lam_re - 1.0) * a_re + lam_im * a_im) / den
    q_im = (lam_im * a_re - (lam_re - 1.0) * a_im) / den
    bb_re, bb_im = _cmul(q_re[..., None], q_im[..., None], b_re.astype(F32), b_im.astype(F32))
    n = jnp.arange(L + 1, dtype=F32)
    pmag = jnp.exp(adt_re[..., None] * n)
    pw_re, pw_im = pmag * jnp.cos(adt_im[..., None] * n), pmag * jnp.sin(adt_im[..., None] * n)

    def ws_exponents(pw, d):
        e = pw[0][..., :L][..., ::-1] if d == 0 else pw[1][..., :L]
        return jnp.transpose(e, (2, 0, 1))[:, :, None]

    def ws_dir(d):
        bt_re, bt_im = jnp.transpose(bb_re[d], (0, 2, 1)), jnp.transpose(bb_im[d], (0, 2, 1))
        return _cmul(ws_exponents(pw_re, d), ws_exponents(pw_im, d), bt_re[None], bt_im[None])

    (wf_re, wf_im), (wb_re, wb_im) = ws_dir(0), ws_dir(1)
    ws = jnp.concatenate([wf_re, wb_re, wf_im, wb_im], axis=-1).reshape(L, NGB, GB * HG, 4 * P)

    def wy_exponents(pw):
        return jnp.stack([pw[0], pw[1][..., ::-1]])

    ct_re = jnp.transpose(c_re.astype(F32), (0, 1, 3, 2))[:, :, :, None]
    ct_im = jnp.transpose(c_im.astype(F32), (0, 1, 3, 2))[:, :, :, None]
    y_re, y_im = _cmul(wy_exponents(pw_re)[..., None], wy_exponents(pw_im)[..., None], ct_re, ct_im)
    wy = jnp.stack([y_re, -y_im]).reshape(2, 2, NGB, GB * P, (L + 1) * HG)

    def lanes(x):
        return jnp.transpose(x.reshape(2, NGB, GB * P), (1, 0, 2)).reshape(NGB, 1, 2 * GB * P)

    a_step = jnp.concatenate([lanes(pw_re[..., L]), lanes(pw_im[..., L])], axis=1)
    return ws.astype(BF16), wy.astype(BF16), a_step, d_skip.astype(F32).reshape(NGB, 1, LANES)


def _s5_expand(xws_ref, zwy_ref, d_ref, wm, wws, wwy):
    L, hg, p = S5_CHUNK, S5_GROUP, S5_STATE
    lg_hg, lg_p, lg_lanes = hg.bit_length() - 1, p.bit_length() - 1, LANES.bit_length() - 1
    gmask = S5_GB - 1
    rblk = 512

    def expand(x, src_of_col, row_group, col_group, o_ref, cblk):
        rows, k = x.shape
        for c0 in range(0, o_ref.shape[1], cblk):
            kk = lax.broadcasted_iota(jnp.int32, (k, cblk), 0)
            cc = lax.broadcasted_iota(jnp.int32, (k, cblk), 1) + c0
            spread = jnp.where(kk == src_of_col(cc), 1.0, 0.0).astype(BF16)
            rr = lax.broadcasted_iota(jnp.int32, (rblk, cblk), 0)
            c2 = lax.broadcasted_iota(jnp.int32, (rblk, cblk), 1) + c0
            keep = row_group(rr) == col_group(c2)
            for r0 in range(0, rows, rblk):
                full = _dot(x[r0:r0 + rblk, :], spread)
                o_ref[r0:r0 + rblk, c0:c0 + cblk] = jnp.where(keep, full, 0.0).astype(o_ref.dtype)

    lane_group = lambda i: jnp.right_shift(i, lg_hg) & gmask
    state_group = lambda i: jnp.right_shift(i, lg_p) & gmask
    chunk_src = lambda c: jnp.right_shift(c, lg_lanes) * hg + (c & (hg - 1))
    state_src = lambda c: jnp.right_shift(c, lg_p + 3) * p + (c & (p - 1))
    xws = xws_ref[...]
    expand(xws.reshape(L * LANES, xws.shape[-1]), state_src, lane_group, state_group, wws, 1024)
    zwy = zwy_ref[...]
    expand(zwy.reshape(S5_SCOLS, zwy.shape[-1]), chunk_src, state_group, lane_group, wwy, wwy.shape[1])

    h = S5_SCOLS // 4
    last, first = wws[(L - 1) * LANES:L * LANES, :], wws[0:LANES, :]
    pf = _dot(last[:, 0:h], wwy[0:h, :]) + _dot(last[:, 2 * h:3 * h], wwy[2 * h:3 * h, :])
    pb = _dot(first[:, h:2 * h], wwy[h:2 * h, :]) + _dot(first[:, 3 * h:4 * h], wwy[3 * h:4 * h, :])
    eye = lax.broadcasted_iota(jnp.int32, (LANES, LANES), 0) == lax.broadcasted_iota(jnp.int32, (LANES, LANES), 1)
    p0 = pf[:, 0:LANES] + pb[:, L * LANES:(L + 1) * LANES] + jnp.where(eye, d_ref[...], 0.0)
    for s in range(L):
        for t in range(L):
            if t > s:
                blk = pf[:, (t - s) * LANES:(t - s + 1) * LANES]
            elif t < s:
                blk = pb[:, (L - (s - t)) * LANES:(L - (s - t) + 1) * LANES]
            else:
                blk = p0
            wm[s * LANES:(s + 1) * LANES, t * LANES:(t + 1) * LANES] = blk.astype(wm.dtype)


def _s5_kernel(u_ref, xws_ref, zwy_ref, d_ref, a_ref, h0_ref, y_ref, hfin_ref,
               wm, wws, wwy, s_scr, hb_scr, y_scr, *, paths):
    nu = S5_CHUNK
    half = S5_SCOLS // 4
    nsl = half // LANES
    cw = 2 * LANES

    _s5_expand(xws_ref, zwy_ref, d_ref, wm, wws, wwy)
    a = a_ref[...]

    for row0, nseq, nchunk, has_h0 in paths:
        r = nseq * nchunk

        for q in range(4):
            sq = _dot(u_ref[row0:row0 + r, :], wws[:, q * half:(q + 1) * half])
            for k in range(nsl):
                for s in range(nseq):
                    s_scr[q * nsl + k, pl.ds(s, nchunk, stride=nseq), :] = (
                        sq[s * nchunk:(s + 1) * nchunk, k * LANES:(k + 1) * LANES])

        if has_h0:
            h0 = h0_ref[...]
            carry0 = tuple(h0[:, j * LANES:(j + 1) * LANES] for j in range(4 * nsl))
        else:
            carry0 = tuple(jnp.zeros((nseq, LANES), F32) for _ in range(4 * nsl))

        def step(i, carry, nseq=nseq, nchunk=nchunk):
            rf = pl.multiple_of(i * nseq, nseq)
            rb = pl.multiple_of((nchunk - 1 - i) * nseq, nseq)
            new = list(carry)
            for k in range(nsl):
                for re, im, rows, acol in ((k, 2 * nsl + k, rf, k), (nsl + k, 3 * nsl + k, rb, nsl + k)):
                    s_re = s_scr[re, pl.ds(rows, nseq), :]
                    s_im = s_scr[im, pl.ds(rows, nseq), :]
                    h_re, h_im = carry[re], carry[im]
                    s_scr[re, pl.ds(rows, nseq), :] = h_re
                    s_scr[im, pl.ds(rows, nseq), :] = h_im
                    a_re = a[0:1, acol * LANES:(acol + 1) * LANES]
                    a_im = a[1:2, acol * LANES:(acol + 1) * LANES]
                    new[re] = a_re * h_re - a_im * h_im + s_re
                    new[im] = a_re * h_im + a_im * h_re + s_im
            return tuple(new)

        fin = lax.fori_loop(0, nchunk, step, carry0)
        if not has_h0:
            for j in range(4 * nsl):
                hfin_ref[:, j * LANES:(j + 1) * LANES] = fin[j]

        for j in range(4 * nsl):
            for s in range(nseq):
                hb_scr[s * nchunk:(s + 1) * nchunk, j * LANES:(j + 1) * LANES] = (
                    s_scr[j, pl.ds(s, nchunk, stride=nseq), :].astype(BF16))

        for j in range(nu * LANES // cw):
            c0 = j * cw
            y = _dot(u_ref[row0:row0 + r, :], wm[:, c0:c0 + cw])
            for q in range(4):
                off = c0 + (LANES if q % 2 == 0 else 0)
                y = y + _dot(hb_scr[0:r, q * half:(q + 1) * half], wwy[q * half:(q + 1) * half, off:off + cw])
            g = _gelu_tanh(y)
            for t2 in range(cw // LANES):
                tau = j * (cw // LANES) + t2
                y_scr[pl.ds(row0 * nu + tau, r, stride=nu), :] = g[:, t2 * LANES:(t2 + 1) * LANES]

    y_ref[...] = y_scr[...].astype(y_ref.dtype)


def _s5_call(u_rows, ws_cmp, wy_cmp, a_step, d_rows, h0, paths):
    rows = u_rows.shape[0]
    n_tok = rows * S5_CHUNK
    sc, k = S5_SCOLS, S5_CHUNK * LANES
    rmax = max(p[1] * p[2] for p in paths)
    nseq_fin = [p[1] for p in paths if not p[3]][0]
    blk = lambda a, b: pl.BlockSpec((None, a, b), lambda g: (g, 0, 0))
    return pl.pallas_call(
        functools.partial(_s5_kernel, paths=paths),
        grid=(S5_NGB,),
        in_specs=[pl.BlockSpec((rows, k), lambda g: (0, g)),
                  pl.BlockSpec((S5_CHUNK, None, LANES, ws_cmp.shape[-1]), lambda g: (0, g, 0, 0)),
                  pl.BlockSpec((2, 2, None, sc // 4, wy_cmp.shape[-1]), lambda g: (0, 0, g, 0, 0)),
                  blk(1, LANES), blk(2, sc // 2), blk(h0.shape[1], sc)],
        out_specs=[pl.BlockSpec((n_tok, LANES), lambda g: (0, g)), blk(nseq_fin, sc)],
        out_shape=[jax.ShapeDtypeStruct((n_tok, S5_WIDTH), BF16),
                   jax.ShapeDtypeStruct((S5_NGB, nseq_fin, sc), F32)],
        scratch_shapes=[pltpu.VMEM((k, k), BF16), pltpu.VMEM((k, sc), BF16),
                        pltpu.VMEM((sc, (S5_CHUNK + 1) * LANES), BF16),
                        pltpu.VMEM((4 * (sc // 4 // LANES), rmax, LANES), F32), pltpu.VMEM((rmax, sc), BF16),
                        pltpu.VMEM((n_tok, LANES), F32)],
        compiler_params=_params("arbitrary"),
        name="s5",
    )(u_rows, ws_cmp, wy_cmp, d_rows, a_step, h0)


def _state_to_lanes(s_re, s_im):
    b = s_re.shape[0]

    def one(x):
        return jnp.transpose(x.reshape(b, 2, S5_NGB, S5_GB * S5_STATE), (2, 0, 1, 3)).reshape(S5_NGB, b, -1)

    return jnp.concatenate([one(s_re), one(s_im)], axis=-1)


def _lanes_to_state(h):
    b = h.shape[1]
    x = h.reshape(S5_NGB, b, 2, 2, S5_GB, S5_STATE)
    x = jnp.transpose(x, (2, 1, 3, 0, 4, 5)).reshape(2, b, 2, S5_NGB * S5_GB, S5_STATE)
    return x[0], x[1]


def _ret_kernel(lg_ref, q_ref, k_ref, v_ref, g_ref, gn_ref, *rest, t, chunk, group, latent, want_state):
    rest = list(rest)
    if latent:
        cos_ref, sin_ref, s0_ref = rest[:3]
        rest = rest[3:]
    o_ref = rest.pop(0)
    sfin_ref = rest.pop(0) if want_state else None
    dmat_ref = rest.pop(0)
    n = t // chunk

    h = pl.program_id(0)
    lgf = lg_ref[0, h]
    lgb = lg_ref[1, h]

    @pl.when(pl.program_id(1) == 0)
    def _():
        ti = lax.broadcasted_iota(jnp.int32, (chunk, chunk), 0)
        si = lax.broadcasted_iota(jnp.int32, (chunk, chunk), 1)
        diff = (ti - si).astype(F32)
        dmat_ref[...] = jnp.exp(jnp.where(diff >= 0, lgf * diff, -lgb * diff))

    pos = lax.broadcasted_iota(jnp.int32, (chunk, 1), 0).astype(F32)
    q_dec_f = jnp.exp(lgf * (pos + 1.0))
    q_dec_b = jnp.exp(lgb * (chunk - pos))
    k_dec_f = jnp.exp(lgf * (chunk - 1.0 - pos))
    k_dec_b = jnp.exp(lgb * pos)
    step_f = jnp.exp(lgf * chunk)
    step_b = jnp.exp(lgb * chunk)
    tn = (((0,), (0,)), ((), ()))
    if latent:
        lane = lax.broadcasted_iota(jnp.int32, (chunk, RET_DK), 1)
        first = (lane % (RET_DK // 2)) < (RET_DK // 4)

        def rope(x, rows):
            swapped = jnp.where(first, pltpu.roll(x, RET_DK - RET_DK // 4, 1), pltpu.roll(x, RET_DK // 4, 1))
            return x * cos_ref[rows, :] + swapped * sin_ref[rows, :]

    for j in range(group):
        qs, kf, kr, vs, outs = [], [], [], [], []
        for c in range(n):
            rows = slice(j * t + c * chunk, j * t + (c + 1) * chunk)
            q = q_ref[rows, :].astype(F32) * (RET_DK ** -0.5)
            k = k_ref[rows, :].astype(F32)
            if latent:
                q = rope(q, slice(c * chunk, (c + 1) * chunk))
                k = rope(k, slice(c * chunk, (c + 1) * chunk))
            qb = q.astype(BF16)
            v = v_ref[rows, :]
            scores = lax.dot_general(qb, k.astype(BF16), (((1,), (1,)), ((), ())), preferred_element_type=F32)
            outs.append(_dot((scores * dmat_ref[...]).astype(BF16), v))
            qs.append(qb)
            vs.append(v)
            kf.append((k * k_dec_f).astype(BF16))
            kr.append((k * k_dec_b).astype(BF16))

        state = s0_ref[j, 0] if latent else None
        for c in range(n):
            if state is not None:
                outs[c] = outs[c] + _dot(qs[c], state.astype(BF16)) * q_dec_f
            if c < n - 1 or want_state:
                upd = lax.dot_general(kf[c], vs[c], tn, preferred_element_type=F32)
                state = upd if state is None else step_f * state + upd
        if want_state:
            sfin_ref[j, 0] = state

        state = s0_ref[j, 1] if latent else None
        for c in reversed(range(n)):
            if state is not None:
                outs[c] = outs[c] + _dot(qs[c], state.astype(BF16)) * q_dec_b
            if c > 0 or want_state:
                upd = lax.dot_general(kr[c], vs[c], tn, preferred_element_type=F32)
                state = upd if state is None else step_b * state + upd
        if want_state:
            sfin_ref[j, 1] = state

        for c in range(n):
            rows = slice(j * t + c * chunk, j * t + (c + 1) * chunk)
            o = outs[c]
            mu = jnp.mean(o, axis=-1, keepdims=True)
            d = o - mu
            var = jnp.mean(d * d, axis=-1, keepdims=True)
            on = d * lax.rsqrt(var + EPS) * gn_ref[...]
            g = g_ref[rows, :].astype(F32)
            o_ref[rows, :] = (g * _sigmoid(g) * on).astype(o_ref.dtype)


def _ret_call(proj, log_gamma, gn_w, rope_tabs, s0, *, row0, nseq, t, chunk, group, latent, want_state):
    h, dk, dv = RET_HEADS, RET_DK, RET_DV
    gt = group * t
    rb0 = row0 // gt
    q0 = 0
    k0 = q0 + h
    v0 = (2 * h * dk) // dv
    g0 = v0 + h
    in_specs = [pl.BlockSpec(memory_space=pltpu.SMEM),
                pl.BlockSpec((gt, dk), lambda hh, b: (rb0 + b, q0 + hh)),
                pl.BlockSpec((gt, dk), lambda hh, b: (rb0 + b, k0 + hh)),
                pl.BlockSpec((gt, dv), lambda hh, b: (rb0 + b, v0 + hh)),
                pl.BlockSpec((gt, dv), lambda hh, b: (rb0 + b, g0 + hh)),
                pl.BlockSpec((1, dv), lambda hh, b: (0, hh))]
    args = [log_gamma, proj, proj, proj, proj, gn_w]
    if latent:
        in_specs += [pl.BlockSpec((t, dk), lambda hh, b: (0, 0)),
                     pl.BlockSpec((t, dk), lambda hh, b: (0, 0)),
                     pl.BlockSpec((group, 2, None, dk, dv), lambda hh, b: (b, 0, hh, 0, 0))]
        args += [rope_tabs[0], rope_tabs[1], s0]
    out_specs = [pl.BlockSpec((gt, dv), lambda hh, b: (b, hh))]
    out_shape = [jax.ShapeDtypeStruct((nseq * t, h * dv), BF16)]
    if want_state:
        out_specs.append(pl.BlockSpec((group, 2, None, dk, dv), lambda hh, b: (b, 0, hh, 0, 0)))
        out_shape.append(jax.ShapeDtypeStruct((nseq, 2, h, dk, dv), F32))
    return pl.pallas_call(
        functools.partial(_ret_kernel, t=t, chunk=chunk, group=group, latent=latent, want_state=want_state),
        grid=(h, nseq // group),
        in_specs=in_specs,
        out_specs=out_specs,
        out_shape=out_shape,
        scratch_shapes=[pltpu.VMEM((chunk, chunk), F32)],
        compiler_params=_params("arbitrary", "arbitrary"),
        name="retention_latent" if latent else "retention_ctx",
    )(*args)


def _grid_rope_tables(t):
    quarter = RET_DK // 4
    freqs = ROPE_BASE ** (-jnp.arange(quarter, dtype=F32) / quarter)
    pos = jnp.arange(t)
    row = (pos // GRID_W).astype(F32)[:, None] * freqs[None, :]
    col = (pos % GRID_W).astype(F32)[:, None] * freqs[None, :]
    cos = jnp.concatenate([jnp.cos(row), jnp.cos(row), jnp.cos(col), jnp.cos(col)], axis=-1)
    sin = jnp.concatenate([-jnp.sin(row), jnp.sin(row), -jnp.sin(col), jnp.sin(col)], axis=-1)
    return cos, sin


def _merge_kernel(yg_ref, op_ref, os_ref, ga_ref, gb_ref, wga_ref, wgb_ref, wr_ref, out_ref, wga_b, wgb_b, wr_b,
                  *, n_prompt_tiles):
    i = pl.program_id(1)

    @pl.when(i == 0)
    def _():
        wga_b[...] = wga_ref[...].astype(BF16)
        wgb_b[...] = wgb_ref[...].astype(BF16)
        wr_b[...] = wr_ref[...].astype(BF16)

    def body(o_ref):
        yg = yg_ref[...]
        za = _dot(yg, wga_b[...])
        zb = _dot(yg, wgb_b[...])
        ob = _dot(o_ref[...], wr_b[...])
        out_a = za * _sigmoid(zb)
        ga = ga_ref[...].astype(F32)
        gb = gb_ref[...].astype(F32)
        out_ref[...] = (_sigmoid(ga) * out_a + _sigmoid(gb) * ob).astype(out_ref.dtype)

    @pl.when(i < n_prompt_tiles)
    def _():
        body(op_ref)

    @pl.when(i >= n_prompt_tiles)
    def _():
        body(os_ref)


def _merge_call(yg, o_p, o_s, proj, w_glu, w_ret_out):
    m, ks = yg.shape
    kr = o_p.shape[1]
    d = w_ret_out.shape[1]
    tm = 1024 if m % 1024 == 0 else m
    tn = 512 if d % 512 == 0 else d
    ga0 = (proj.shape[1] - 2 * d) // tn
    gb0 = (proj.shape[1] - d) // tn
    nb = d // tn
    npt = o_p.shape[0] // tm
    return pl.pallas_call(
        functools.partial(_merge_kernel, n_prompt_tiles=npt),
        grid=(nb, m // tm),
        in_specs=[pl.BlockSpec((tm, ks), lambda j, i: (i, 0)),
                  pl.BlockSpec((tm, kr), lambda j, i: (jnp.minimum(i, npt - 1), 0)),
                  pl.BlockSpec((tm, kr), lambda j, i: (jnp.maximum(i - npt, 0), 0)),
                  pl.BlockSpec((tm, tn), lambda j, i: (i, ga0 + j)),
                  pl.BlockSpec((tm, tn), lambda j, i: (i, gb0 + j)),
                  pl.BlockSpec((ks, tn), lambda j, i: (0, j)),
                  pl.BlockSpec((ks, tn), lambda j, i: (0, nb + j)),
                  pl.BlockSpec((kr, tn), lambda j, i: (0, j))],
        out_specs=pl.BlockSpec((tm, tn), lambda j, i: (i, j)),
        out_shape=jax.ShapeDtypeStruct((m, d), BF16),
        scratch_shapes=[pltpu.VMEM((ks, tn), BF16), pltpu.VMEM((ks, tn), BF16), pltpu.VMEM((kr, tn), BF16)],
        compiler_params=_params("arbitrary", "arbitrary"),
        name="merge",
    )(yg, o_p, o_s, proj, proj, w_glu, w_glu, w_ret_out)


def _outproj_kernel(mg_ref, xp_ref, xs_ref, w_ref, g1_ref, sc_ref, sh_ref, n2_ref, wr_ref,
                    x1_ref, h2_ref, aff_ref, *, n_prompt_tiles, n_experts):
    i = pl.program_id(0)
    upd = g1_ref[...] * _dot(mg_ref[...], w_ref[...])

    def body(x_ref):
        x1 = x_ref[...] + upd
        x1_ref[...] = x1
        y = x1 * lax.rsqrt(jnp.mean(x1 * x1, axis=-1, keepdims=True) + EPS)
        h2 = (y * n2_ref[...]) * (1.0 + sc_ref[...]) + sh_ref[...]
        h2_ref[...] = h2.astype(h2_ref.dtype)
        wr = wr_ref[...]
        w_hi = wr.astype(BF16)
        w_lo = (wr - w_hi.astype(F32)).astype(BF16)
        h_hi = h2.astype(BF16)
        h_lo = (h2 - h_hi.astype(F32)).astype(BF16)
        hw = _dot(h_hi, jnp.concatenate([w_hi, w_lo], axis=1))
        logits = hw[:, :LANES] + (hw[:, LANES:] + _dot(h_lo, w_hi))
        lane = lax.broadcasted_iota(jnp.int32, logits.shape, 1)
        logits = jnp.where(lane < n_experts, logits, -jnp.inf)
        e = jnp.exp(logits - jnp.max(logits, axis=-1, keepdims=True))
        aff_ref[...] = e / jnp.sum(e, axis=-1, keepdims=True)

    @pl.when(i < n_prompt_tiles)
    def _():
        body(xp_ref)

    @pl.when(i >= n_prompt_tiles)
    def _():
        body(xs_ref)


def _outproj_call(merged, xp, xs, w_out_b, gate1, scale2, shift2, norm2, w_router_pad, t_sample):
    m, d = merged.shape
    n_p = xp.shape[0]
    tm = 2 * ROW_TILE
    npt = n_p // tm
    tps = t_sample // tm
    row = functools.partial(_mod_row, n_prompt_tiles=npt, tiles_per_sample_seq=tps)
    mod = pl.BlockSpec((None, 1, d), lambda i: (row(i), 0, 0))
    return pl.pallas_call(
        functools.partial(_outproj_kernel, n_prompt_tiles=npt, n_experts=N_EXPERTS),
        grid=(m // tm,),
        in_specs=[pl.BlockSpec((tm, d), lambda i: (i, 0)),
                  pl.BlockSpec((tm, d), lambda i: (jnp.minimum(i, npt - 1), 0)),
                  pl.BlockSpec((tm, d), lambda i: (jnp.maximum(i - npt, 0), 0)),
                  pl.BlockSpec((d, d), lambda i: (0, 0)),
                  mod, mod, mod,
                  pl.BlockSpec((1, d), lambda i: (0, 0)),
                  pl.BlockSpec((d, LANES), lambda i: (0, 0))],
        out_specs=[pl.BlockSpec((tm, d), lambda i: (i, 0)),
                   pl.BlockSpec((tm, d), lambda i: (i, 0)),
                   pl.BlockSpec((tm, LANES), lambda i: (i, 0))],
        out_shape=[jax.ShapeDtypeStruct((m, d), F32),
                   jax.ShapeDtypeStruct((m, d), BF16),
                   jax.ShapeDtypeStruct((m, LANES), F32)],
        compiler_params=_params("arbitrary"),
        name="out_proj_router",
    )(merged, xp, xs, w_out_b, gate1, scale2, shift2, norm2, w_router_pad)


def _select_kernel(a_ref, slot_ref, *, cap):
    a = a_ref[...]
    r, t = a.shape

    def as_float(bits):
        return pltpu.bitcast(bits, F32)

    def bisect(_, carry):
        lo, hi = carry
        mid = lo + jnp.right_shift(hi - lo + 1, 1)
        cnt = jnp.sum(jnp.where(a >= as_float(mid), 1.0, 0.0), axis=-1, keepdims=True)
        ok = cnt >= cap
        return jnp.where(ok, mid, lo), jnp.where(ok, hi, mid - 1)

    lo0 = jnp.zeros((r, 1), jnp.int32)
    hi0 = jnp.full((r, 1), 0x3F800000, jnp.int32)
    thr, _ = lax.fori_loop(0, 31, bisect, (lo0, hi0))

    gt = jnp.where(a >= as_float(thr + 1), 1.0, 0.0)
    eq = jnp.where(a >= as_float(thr), 1.0, 0.0) - gt
    need = cap - jnp.sum(gt, axis=-1, keepdims=True)
    before = lax.broadcasted_iota(jnp.int32, (t, t), 0) < lax.broadcasted_iota(jnp.int32, (t, t), 1)
    tri = jnp.where(before, 1.0, 0.0).astype(BF16)
    eq_rank = _dot(eq.astype(BF16), tri)
    sel = gt + eq * jnp.where(eq_rank < need, 1.0, 0.0)
    pos = _dot(sel.astype(BF16), tri)
    slot_ref[...] = jnp.where(sel > 0.5, pos, -1.0).astype(jnp.int32)


def _select_call(aff_t, cap):
    r, t = aff_t.shape
    return pl.pallas_call(
        functools.partial(_select_kernel, cap=cap),
        grid=(1,),
        in_specs=[pl.BlockSpec((r, t), lambda i: (0, 0))],
        out_specs=pl.BlockSpec((r, t), lambda i: (0, 0)),
        out_shape=jax.ShapeDtypeStruct((r, t), jnp.int32),
        compiler_params=_params("arbitrary"),
        name="select",
    )(aff_t)


def _gather_kernel(slot_ref, aff_ref, h_ref, xs_ref, gate_ref, *, cap, group):
    e_total, t = slot_ref.shape
    h = h_ref[...]
    ci = lax.broadcasted_iota(jnp.int32, (cap, t), 0)
    for e0 in range(0, e_total, group):
        hots = []
        for e in range(e0, e0 + group):
            hit = ci == slot_ref[e:e + 1, :]
            hots.append(jnp.where(hit, 1.0, 0.0).astype(BF16))
            gate_ref[e] = jnp.sum(jnp.where(hit, aff_ref[e:e + 1, :], 0.0), axis=-1, keepdims=True)
        onehot = hots[0] if group == 1 else jnp.concatenate(hots, axis=0)
        xs = _dot(onehot, h).astype(xs_ref.dtype)
        xs_ref[e0:e0 + group] = xs.reshape(group, cap, xs.shape[-1])


def _gather_call(slot_t, aff_t, h2, *, row0, nseq, t, cap):
    e = slot_t.shape[1]
    d = h2.shape[1]
    rb0 = row0 // t
    group = max(1, min(e, 512 // cap))
    return pl.pallas_call(
        functools.partial(_gather_kernel, cap=cap, group=group),
        grid=(nseq,),
        in_specs=[pl.BlockSpec((None, e, t), lambda b: (b, 0, 0)),
                  pl.BlockSpec((None, e, t), lambda b: (b, 0, 0)),
                  pl.BlockSpec((t, d), lambda b: (rb0 + b, 0))],
        out_specs=[pl.BlockSpec((e, cap, d), lambda b: (0, b, 0)),
                   pl.BlockSpec((e, cap, 1), lambda b: (0, b, 0))],
        out_shape=[jax.ShapeDtypeStruct((e, nseq * cap, d), BF16),
                   jax.ShapeDtypeStruct((e, nseq * cap, 1), F32)],
        compiler_params=_params("arbitrary"),
        name="gather",
    )(slot_t, aff_t, h2)


def _ffn_kernel(xp_ref, xs_ref, gp_ref, gs_ref, wg_ref, wu_ref, wd_ref, yp_ref, ys_ref,
                accp, accs, wgb, wub, wdb, *, chunk):
    f = pl.program_id(1)

    @pl.when(f == 0)
    def _():
        accp[...] = jnp.zeros_like(accp)
        accs[...] = jnp.zeros_like(accs)

    wgb[...] = wg_ref[...].astype(BF16)
    wub[...] = wu_ref[...].astype(BF16)
    wdb[...] = wd_ref[...].astype(BF16)

    def part(x_ref, acc):
        m = x_ref.shape[0]
        mc = min(chunk, m)
        for m0 in range(0, m, mc):
            x = x_ref[m0:m0 + mc, :]
            hg = _dot(x, wgb[...])
            hu = _dot(x, wub[...])
            hid = (hg * _sigmoid(hg) * hu).astype(BF16)
            acc[m0:m0 + mc, :] += _dot(hid, wdb[...])

    part(xp_ref, accp)
    part(xs_ref, accs)

    @pl.when(f == pl.num_programs(1) - 1)
    def _():
        yp_ref[...] = (accp[...] * gp_ref[...]).astype(yp_ref.dtype)
        ys_ref[...] = (accs[...] * gs_ref[...]).astype(ys_ref.dtype)


def _ffn_call(xs_p, xs_s, gate_p, gate_s, w_gate, w_up, w_down):
    e, mp, d = xs_p.shape
    ms = xs_s.shape[1]
    ff = w_gate.shape[2]
    tf = 256 if ff % 256 == 0 else ff
    tok = lambda m, w: pl.BlockSpec((None, m, w), lambda ee, f: (ee, 0, 0))
    return pl.pallas_call(
        functools.partial(_ffn_kernel, chunk=512),
        grid=(e, ff // tf),
        in_specs=[tok(mp, d), tok(ms, d), tok(mp, 1), tok(ms, 1),
                  pl.BlockSpec((None, d, tf), lambda ee, f: (ee, 0, f)),
                  pl.BlockSpec((None, d, tf), lambda ee, f: (ee, 0, f)),
                  pl.BlockSpec((None, tf, d), lambda ee, f: (ee, f, 0))],
        out_specs=[pl.BlockSpec((None, mp, d), lambda ee, f: (ee, 0, 0), pipeline_mode=pl.Buffered(1)),
                   pl.BlockSpec((None, ms, d), lambda ee, f: (ee, 0, 0), pipeline_mode=pl.Buffered(1))],
        out_shape=[jax.ShapeDtypeStruct((e, mp, d), BF16), jax.ShapeDtypeStruct((e, ms, d), BF16)],
        scratch_shapes=[pltpu.VMEM((mp, d), F32), pltpu.VMEM((ms, d), F32),
                        pltpu.VMEM((d, tf), BF16), pltpu.VMEM((d, tf), BF16), pltpu.VMEM((tf, d), BF16)],
        compiler_params=_params("arbitrary", "arbitrary"),
        name="expert_ffn",
    )(xs_p, xs_s, gate_p, gate_s, w_gate, w_up, w_down)


def _scatter_kernel(slot_ref, y_ref, x1_ref, g2_ref, wn_ref, o_ref, *, cap):
    e_total = y_ref.shape[0]
    tm = slot_ref.shape[0]
    slot = slot_ref[...]
    lane = lax.broadcasted_iota(jnp.int32, (tm, LANES), 1)
    per_block = max(1, LANES // cap)
    blocks = []
    for b0 in range(0, e_total, per_block):
        acc = jnp.zeros((tm, LANES), F32)
        for j in range(per_block):
            s = slot[:, b0 + j:b0 + j + 1]
            key = jnp.where(s >= 0, s + j * cap, -1)
            acc = acc + jnp.where(lane == key, 1.0, 0.0)
        blocks.append(acc.astype(BF16))
    onehot = jnp.concatenate(blocks, axis=1)
    y = y_ref[...].reshape(e_total * cap, y_ref.shape[-1])
    moe = _dot(onehot, y)
    x2 = x1_ref[...] + g2_ref[...] * moe
    o_ref[...] = x2 * lax.rsqrt(jnp.mean(x2 * x2, axis=-1, keepdims=True) + EPS) * wn_ref[...]


def _scatter_call(slot, y, x1, gate2, final_norm, *, row0, nseq, t, cap, mod_row0, mod_per_seq):
    e, _, d = y.shape
    tm = ROW_TILE
    nt = t // tm
    rb0 = row0 // tm
    assert cap == LANES or LANES % cap == 0
    return pl.pallas_call(
        functools.partial(_scatter_kernel, cap=cap),
        grid=(nseq, nt),
        in_specs=[pl.BlockSpec((None, tm, e), lambda b, i: (b, i, 0)),
                  pl.BlockSpec((e, cap, d), lambda b, i: (0, b, 0)),
                  pl.BlockSpec((tm, d), lambda b, i: (rb0 + b * nt + i, 0)),
                  pl.BlockSpec((None, 1, d), lambda b, i: (mod_row0 + b * mod_per_seq, 0, 0)),
                  pl.BlockSpec((1, d), lambda b, i: (0, 0))],
        out_specs=pl.BlockSpec((tm, d), lambda b, i: (b * nt + i, 0)),
        out_shape=jax.ShapeDtypeStruct((nseq * t, d), F32),
        compiler_params=_params("arbitrary", "arbitrary"),
        name="scatter_final",
    )(slot, y, x1, gate2, final_norm)


def kernel(x_prompt, x_sample, state_s5_re, state_s5_im, state_ret, c, c_ctx, final_norm, w_ada, b_ada, norm1, norm2, w_in, s5_a_re, s5_a_im, s5_log_dt, s5_b_re, s5_b_im, s5_c_re, s5_c_im, s5_d, w_s5_glu, ret_decay_logit, ret_gn_w, w_ret_out, w_out, w_router, w_exp_gate, w_exp_up, w_exp_down):
    bp, tp, d = x_prompt.shape
    bs, ts, _ = x_sample.shape
    depth = w_ada.shape[0]
    n_p, n_s = bp * tp, bs * ts
    xp = x_prompt.reshape(n_p, d)
    xs = x_sample.reshape(n_s, d)

    mod_rows = 16
    cvec = jnp.zeros((mod_rows, d), F32).at[0].set(c_ctx).at[1:1 + bs].set(c)
    rope_tabs = _grid_rope_tables(ts)
    cap_p = CAPACITY_FACTOR * tp // N_EXPERTS
    cap_s = CAPACITY_FACTOR * ts // N_EXPERTS

    new_re, new_im, new_ret = [], [], []
    for l in range(depth):
        mods = _ada_call(cvec, w_ada[l], b_ada[l])
        shift1, scale1, gate1, shift2, scale2, gate2 = [m.reshape(mod_rows, 1, d) for m in jnp.split(mods, 6, axis=-1)]

        h, u_rows = _uproj_call(xp, xs, scale1, shift1, norm1[l].reshape(1, d), w_in[l], ts)
        proj = _proj_call(h, w_in[l], S5_WIDTH)

        ws_cmp, wy_cmp, a_step, d_rows = _s5_chunk_weights(
            s5_a_re[l], s5_a_im[l], s5_log_dt[l], s5_b_re[l], s5_b_im[l], s5_c_re[l], s5_c_im[l], s5_d[l])
        h0_s = _state_to_lanes(state_s5_re[:, l].astype(F32), state_s5_im[:, l].astype(F32))
        s5_paths = ((0, bp, tp // S5_CHUNK, False), (n_p // S5_CHUNK, bs, ts // S5_CHUNK, True))
        yg, hfin_p = _s5_call(u_rows, ws_cmp, wy_cmp, a_step, d_rows, h0_s, s5_paths)
        s5_re, s5_im = _lanes_to_state(hfin_p)
        new_re.append(s5_re)
        new_im.append(s5_im)

        log_gamma = jax.nn.log_sigmoid(ret_decay_logit[l].astype(F32))
        gn_w = ret_gn_w[l].reshape(1, -1).astype(F32)
        o_p, sfin = _ret_call(proj, log_gamma, gn_w, None, None, row0=0, nseq=bp, t=tp, chunk=RET_CHUNK, group=4, latent=False,
                              want_state=True)
        o_s, = _ret_call(proj, log_gamma, gn_w, rope_tabs, state_ret[:, l].astype(F32), row0=n_p, nseq=bs, t=ts,
                         chunk=RET_CHUNK, group=4, latent=True, want_state=False)
        new_ret.append(sfin)
        merged = _merge_call(yg, o_p, o_s, proj, w_s5_glu[l], w_ret_out[l])
        w_router_pad = jnp.zeros((d, LANES), F32).at[:, :N_EXPERTS].set(w_router[l].astype(F32))
        x1, h2, aff = _outproj_call(merged, xp, xs, w_out[l].astype(BF16), gate1, scale2, shift2,
                                    norm2[l].reshape(1, d), w_router_pad, ts)

        aff = aff[:, :N_EXPERTS]
        aff_p = jnp.transpose(aff[:n_p].reshape(bp, tp, N_EXPERTS), (0, 2, 1))
        aff_s = jnp.transpose(aff[n_p:].reshape(bs, ts, N_EXPERTS), (0, 2, 1))
        slot_p = _select_call(aff_p.reshape(bp * N_EXPERTS, tp), cap_p).reshape(bp, N_EXPERTS, tp)
        slot_s = _select_call(aff_s.reshape(bs * N_EXPERTS, ts), cap_s).reshape(bs, N_EXPERTS, ts)
        xe_p, ge_p = _gather_call(slot_p, aff_p, h2, row0=0, nseq=bp, t=tp, cap=cap_p)
        xe_s, ge_s = _gather_call(slot_s, aff_s, h2, row0=n_p, nseq=bs, t=ts, cap=cap_s)
        ye_p, ye_s = _ffn_call(xe_p, xe_s, ge_p, ge_s, w_exp_gate[l], w_exp_up[l], w_exp_down[l])

        last = l == depth - 1
        wn = final_norm.reshape(1, d).astype(F32) if last else None
        assert last, "only the final layer applies the output norm in the scatter kernel"
        yp = _scatter_call(jnp.transpose(slot_p, (0, 2, 1)), ye_p, x1, gate2, wn, row0=0, nseq=bp, t=tp, cap=cap_p,
                           mod_row0=0, mod_per_seq=0)
        ysm = _scatter_call(jnp.transpose(slot_s, (0, 2, 1)), ye_s, x1, gate2, wn, row0=n_p, nseq=bs, t=ts, cap=cap_s,
                            mod_row0=1, mod_per_seq=1)

    y_prompt = yp.reshape(bp, tp, d)
    y_sample = ysm.reshape(bs, ts, d)
    return (y_prompt, y_sample, jnp.stack(new_re, axis=1), jnp.stack(new_im, axis=1), jnp.stack(new_ret, axis=1))
```

```python
import functools
import math

import jax
import jax.numpy as jnp
from jax import lax
from jax.experimental import pallas as pl
from jax.experimental.pallas import tpu as pltpu

F32 = jnp.float32
BF16 = jnp.bfloat16

EPS = 1e-6
GRID_W = 64
S5_WIDTH = 1024
S5_GROUP = 16
S5_STATE = 64
RET_HEADS = 8
RET_DK = 128
RET_DV = 256
ROPE_BASE = 10000.0
RET_CHUNK = 256
N_EXPERTS = 16
CAPACITY_FACTOR = 2

LANES = 128
S5_CHUNK = 8
S5_GB = LANES // S5_GROUP
S5_NGB = S5_WIDTH // LANES
S5_SCOLS = 4 * S5_GB * S5_STATE
ROW_TILE = 256
VMEM_LIMIT = 56 * 1024 * 1024


def _params(*sem):
    return pltpu.CompilerParams(dimension_semantics=sem, vmem_limit_bytes=VMEM_LIMIT)


def _sigmoid(x):
    return 1.0 / (1.0 + jnp.exp(-x))


def _gelu_tanh(x):
    return 0.5 * x * (1.0 + jnp.tanh(math.sqrt(2.0 / math.pi) * (x + 0.044715 * (x * x * x))))


def _dot(a, b):
    return jnp.dot(a, b, preferred_element_type=F32)


def _ada_kernel(c_ref, w_ref, b_ref, o_ref):
    c = c_ref[...]
    s = c * _sigmoid(c)
    w = w_ref[...]
    s_hi = s.astype(BF16)
    s_lo = (s - s_hi.astype(F32)).astype(BF16)
    w_hi = w.astype(BF16)
    w_lo = (w - w_hi.astype(F32)).astype(BF16)
    o_ref[...] = _dot(s_hi, w_hi) + (_dot(s_hi, w_lo) + _dot(s_lo, w_hi)) + b_ref[...]


def _ada_call(cvec, w_ada, b_ada):
    r, d = cvec.shape
    n = w_ada.shape[1]
    tn = 1024 if n % 1024 == 0 else 512
    return pl.pallas_call(
        _ada_kernel,
        grid=(n // tn,),
        in_specs=[pl.BlockSpec((r, d), lambda j: (0, 0)),
                  pl.BlockSpec((d, tn), lambda j: (0, j)),
                  pl.BlockSpec((1, tn), lambda j: (0, j))],
        out_specs=pl.BlockSpec((r, tn), lambda j: (0, j)),
        out_shape=jax.ShapeDtypeStruct((r, n), F32),
        compiler_params=_params("arbitrary"),
        name="ada",
    )(cvec, w_ada, b_ada.reshape(1, n))


def _mod_row(i, n_prompt_tiles, tiles_per_sample_seq):
    return jnp.where(i < n_prompt_tiles, 0, 1 + (i - n_prompt_tiles) // tiles_per_sample_seq)


def _proj_kernel(a_ref, w_ref, o_ref, wb_ref):
    @pl.when(pl.program_id(1) == 0)
    def _():
        wb_ref[...] = w_ref[...].astype(BF16)

    o_ref[...] = _dot(a_ref[...], wb_ref[...]).astype(o_ref.dtype)


def _proj_call(a, w, col0):
    m, k = a.shape
    n = w.shape[1] - col0
    tm = next((c for c in (1536, 1024) if m % c == 0), m)
    tn = 1024 if n % 1024 == 0 else 512
    cb0 = col0 // tn
    return pl.pallas_call(
        _proj_kernel,
        grid=(n // tn, m // tm),
        in_specs=[pl.BlockSpec((tm, k), lambda j, i: (i, 0)),
                  pl.BlockSpec((k, tn), lambda j, i: (0, cb0 + j))],
        out_specs=pl.BlockSpec((tm, tn), lambda j, i: (i, j)),
        out_shape=jax.ShapeDtypeStruct((m, n), BF16),
        scratch_shapes=[pltpu.VMEM((k, tn), BF16)],
        compiler_params=_params("arbitrary", "arbitrary"),
        name="in_proj",
    )(a, w)


def _cmul(ar, ai, br, bi):
    return ar * br - ai * bi, ar * bi + ai * br


def _uproj_kernel(xp_ref, xs_ref, sc_ref, sh_ref, nw_ref, w_ref, h_ref, o_ref, wb_ref, r_scr, *, n_prompt_tiles):
    i = pl.program_id(0)

    @pl.when(i == 0)
    def _():
        wb_ref[...] = w_ref[...].astype(BF16)

    def body(x_ref):
        x = x_ref[...]
        y = x * lax.rsqrt(jnp.mean(x * x, axis=-1, keepdims=True) + EPS)
        hb = ((y * nw_ref[...]) * (1.0 + sc_ref[...]) + sh_ref[...]).astype(BF16)
        h_ref[...] = hb
        res = _dot(hb, wb_ref[...])
        for g in range(S5_NGB):
            r_scr[g] = res[:, g * LANES:(g + 1) * LANES]

    @pl.when(i < n_prompt_tiles)
    def _():
        body(xp_ref)

    @pl.when(i >= n_prompt_tiles)
    def _():
        body(xs_ref)

    rows = h_ref.shape[0] // S5_CHUNK
    for g in range(S5_NGB):
        for tau in range(S5_CHUNK):
            c0 = (g * S5_CHUNK + tau) * LANES
            o_ref[:, c0:c0 + LANES] = r_scr[g, pl.ds(tau, rows, stride=S5_CHUNK), :].astype(o_ref.dtype)


def _uproj_call(xp, xs, scale, shift, nw, w, t_sample):
    n_p, d = xp.shape
    m = n_p + xs.shape[0]
    tm = 2 * ROW_TILE
    npt = n_p // tm
    row = functools.partial(_mod_row, n_prompt_tiles=npt, tiles_per_sample_seq=t_sample // tm)
    mod = pl.BlockSpec((None, 1, d), lambda i: (row(i), 0, 0))
    return pl.pallas_call(
        functools.partial(_uproj_kernel, n_prompt_tiles=npt),
        grid=(m // tm,),
        in_specs=[pl.BlockSpec((tm, d), lambda i: (jnp.minimum(i, npt - 1), 0)),
                  pl.BlockSpec((tm, d), lambda i: (jnp.maximum(i - npt, 0), 0)),
                  mod, mod,
                  pl.BlockSpec((1, d), lambda i: (0, 0)),
                  pl.BlockSpec((d, S5_WIDTH), lambda i: (0, 0))],
        out_specs=[pl.BlockSpec((tm, d), lambda i: (i, 0)),
                   pl.BlockSpec((tm // S5_CHUNK, S5_CHUNK * S5_WIDTH), lambda i: (i, 0))],
        out_shape=[jax.ShapeDtypeStruct((m, d), BF16),
                   jax.ShapeDtypeStruct((m // S5_CHUNK, S5_CHUNK * S5_WIDTH), BF16)],
        scratch_shapes=[pltpu.VMEM((d, S5_WIDTH), BF16), pltpu.VMEM((S5_NGB, tm, LANES), F32)],
        compiler_params=_params("arbitrary"),
        name="norm_u_proj",
    )(xp, xs, scale, shift, nw, w)


def _s5_chunk_weights(a_re, a_im, log_dt, b_re, b_im, c_re, c_im, d_skip):
    L, G, P, HG, GB, NGB = S5_CHUNK, S5_WIDTH // S5_GROUP, S5_STATE, S5_GROUP, S5_GB, S5_NGB
    a_re, a_im = a_re.astype(F32), a_im.astype(F32)
    dt = jnp.exp(log_dt.astype(F32))[..., None]
    adt_re, adt_im = a_re * dt, a_im * dt
    mag = jnp.exp(adt_re)
    lam_re, lam_im = mag * jnp.cos(adt_im), mag * jnp.sin(adt_im)
    den = a_re * a_re + a_im * a_im
    q_re = ((lam_re - 1.0) * a_re + lam_im * a_im) / den
    q_im = (lam_im * a_re - (lam_re - 1.0) * a_im) / den
    bb_re, bb_im = _cmul(q_re[..., None], q_im[..., None], b_re.astype(F32), b_im.astype(F32))
    n = jnp.arange(L + 1, dtype=F32)
    pmag = jnp.exp(adt_re[..., None] * n)
    pw_re, pw_im = pmag * jnp.cos(adt_im[..., None] * n), pmag * jnp.sin(adt_im[..., None] * n)

    def ws_exponents(pw, d):
        e = pw[0][..., :L][..., ::-1] if d == 0 else pw[1][..., :L]
        return jnp.transpose(e, (2, 0, 1))[:, :, None]

    def ws_dir(d):
        bt_re, bt_im = jnp.transpose(bb_re[d], (0, 2, 1)), jnp.transpose(bb_im[d], (0, 2, 1))
        return _cmul(ws_exponents(pw_re, d), ws_exponents(pw_im, d), bt_re[None], bt_im[None])

    (wf_re, wf_im), (wb_re, wb_im) = ws_dir(0), ws_dir(1)
    ws = jnp.concatenate([wf_re, wb_re, wf_im, wb_im], axis=-1).reshape(L, NGB, GB * HG, 4 * P)

    def wy_exponents(pw):
        return jnp.stack([pw[0], pw[1][..., ::-1]])

    ct_re = jnp.transpose(c_re.astype(F32), (0, 1, 3, 2))[:, :, :, None]
    ct_im = jnp.transpose(c_im.astype(F32), (0, 1, 3, 2))[:, :, :, None]
    y_re, y_im = _cmul(wy_exponents(pw_re)[..., None], wy_exponents(pw_im)[..., None], ct_re, ct_im)
    wy = jnp.stack([y_re, -y_im]).reshape(2, 2, NGB, GB * P, (L + 1) * HG)

    def lanes(x):
        return jnp.transpose(x.reshape(2, NGB, GB * P), (1, 0, 2)).reshape(NGB, 1, 2 * GB * P)

    a_step = jnp.concatenate([lanes(pw_re[..., L]), lanes(pw_im[..., L])], axis=1)
    return ws.astype(BF16), wy.astype(BF16), a_step, d_skip.astype(F32).reshape(NGB, 1, LANES)


def _s5_expand(xws_ref, zwy_ref, d_ref, wm, wws, wwy):
    L, hg, p = S5_CHUNK, S5_GROUP, S5_STATE
    lg_hg, lg_p, lg_lanes = hg.bit_length() - 1, p.bit_length() - 1, LANES.bit_length() - 1
    gmask = S5_GB - 1
    rblk = 512

    def expand(x, src_of_col, row_group, col_group, o_ref, cblk):
        rows, k = x.shape
        for c0 in range(0, o_ref.shape[1], cblk):
            kk = lax.broadcasted_iota(jnp.int32, (k, cblk), 0)
            cc = lax.broadcasted_iota(jnp.int32, (k, cblk), 1) + c0
            spread = jnp.where(kk == src_of_col(cc), 1.0, 0.0).astype(BF16)
            rr = lax.broadcasted_iota(jnp.int32, (rblk, cblk), 0)
            c2 = lax.broadcasted_iota(jnp.int32, (rblk, cblk), 1) + c0
            keep = row_group(rr) == col_group(c2)
            for r0 in range(0, rows, rblk):
                full = _dot(x[r0:r0 + rblk, :], spread)
                o_ref[r0:r0 + rblk, c0:c0 + cblk] = jnp.where(keep, full, 0.0).astype(o_ref.dtype)

    lane_group = lambda i: jnp.right_shift(i, lg_hg) & gmask
    state_group = lambda i: jnp.right_shift(i, lg_p) & gmask
    chunk_src = lambda c: jnp.right_shift(c, lg_lanes) * hg + (c & (hg - 1))
    state_src = lambda c: jnp.right_shift(c, lg_p + 3) * p + (c & (p - 1))
    xws = xws_ref[...]
    expand(xws.reshape(L * LANES, xws.shape[-1]), state_src, lane_group, state_group, wws, 1024)
    zwy = zwy_ref[...]
    expand(zwy.reshape(S5_SCOLS, zwy.shape[-1]), chunk_src, state_group, lane_group, wwy, wwy.shape[1])

    h = S5_SCOLS // 4
    last, first = wws[(L - 1) * LANES:L * LANES, :], wws[0:LANES, :]
    pf = _dot(last[:, 0:h], wwy[0:h, :]) + _dot(last[:, 2 * h:3 * h], wwy[2 * h:3 * h, :])
    pb = _dot(first[:, h:2 * h], wwy[h:2 * h, :]) + _dot(first[:, 3 * h:4 * h], wwy[3 * h:4 * h, :])
    eye = lax.broadcasted_iota(jnp.int32, (LANES, LANES), 0) == lax.broadcasted_iota(jnp.int32, (LANES, LANES), 1)
    p0 = pf[:, 0:LANES] + pb[:, L * LANES:(L + 1) * LANES] + jnp.where(eye, d_ref[...], 0.0)
    for s in range(L):
        for t in range(L):
            if t > s:
                blk = pf[:, (t - s) * LANES:(t - s + 1) * LANES]
            elif t < s:
                blk = pb[:, (L - (s - t)) * LANES:(L - (s - t) + 1) * LANES]
            else:
                blk = p0
            wm[s * LANES:(s + 1) * LANES, t * LANES:(t + 1) * LANES] = blk.astype(wm.dtype)


def _s5_kernel(u_ref, xws_ref, zwy_ref, d_ref, a_ref, h0_ref, y_ref, hfin_ref,
               wm, wws, wwy, s_scr, hb_scr, y_scr, *, paths):
    nu = S5_CHUNK
    half = S5_SCOLS // 4
    nsl = half // LANES
    cw = 2 * LANES

    _s5_expand(xws_ref, zwy_ref, d_ref, wm, wws, wwy)
    a = a_ref[...]

    for row0, nseq, nchunk, has_h0 in paths:
        r = nseq * nchunk

        for q in range(4):
            sq = _dot(u_ref[row0:row0 + r, :], wws[:, q * half:(q + 1) * half])
            for k in range(nsl):
                for s in range(nseq):
                    s_scr[q * nsl + k, pl.ds(s, nchunk, stride=nseq), :] = (
                        sq[s * nchunk:(s + 1) * nchunk, k * LANES:(k + 1) * LANES])

        if has_h0:
            h0 = h0_ref[...]
            carry0 = tuple(h0[:, j * LANES:(j + 1) * LANES] for j in range(4 * nsl))
        else:
            carry0 = tuple(jnp.zeros((nseq, LANES), F32) for _ in range(4 * nsl))

        def step(i, carry, nseq=nseq, nchunk=nchunk):
            f0 = pl.multiple_of(2 * i * nseq, nseq)
            f1 = pl.multiple_of((2 * i + 1) * nseq, nseq)
            b0 = pl.multiple_of((nchunk - 1 - 2 * i) * nseq, nseq)
            b1 = pl.multiple_of((nchunk - 2 - 2 * i) * nseq, nseq)
            new = list(carry)
            for k in range(nsl):
                for re, im, r0, r1, acol in ((k, 2 * nsl + k, f0, f1, k), (nsl + k, 3 * nsl + k, b0, b1, nsl + k)):
                    a_re = a[0:1, acol * LANES:(acol + 1) * LANES]
                    a_im = a[1:2, acol * LANES:(acol + 1) * LANES]
                    a2_re, a2_im = _cmul(a_re, a_im, a_re, a_im)
                    s0_re, s0_im = s_scr[re, pl.ds(r0, nseq), :], s_scr[im, pl.ds(r0, nseq), :]
                    s1_re, s1_im = s_scr[re, pl.ds(r1, nseq), :], s_scr[im, pl.ds(r1, nseq), :]
                    h_re, h_im = carry[re], carry[im]
                    s_scr[re, pl.ds(r0, nseq), :] = h_re
                    s_scr[im, pl.ds(r0, nseq), :] = h_im
                    m_re, m_im = _cmul(a_re, a_im, h_re, h_im)
                    s_scr[re, pl.ds(r1, nseq), :] = m_re + s0_re
                    s_scr[im, pl.ds(r1, nseq), :] = m_im + s0_im
                    t_re, t_im = _cmul(a_re, a_im, s0_re, s0_im)
                    n_re, n_im = _cmul(a2_re, a2_im, h_re, h_im)
                    new[re] = n_re + (t_re + s1_re)
                    new[im] = n_im + (t_im + s1_im)
            return tuple(new)

        assert nchunk % 2 == 0
        fin = lax.fori_loop(0, nchunk // 2, step, carry0)
        if not has_h0:
            for j in range(4 * nsl):
                hfin_ref[:, j * LANES:(j + 1) * LANES] = fin[j]

        for j in range(4 * nsl):
            for s in range(nseq):
                hb_scr[s * nchunk:(s + 1) * nchunk, j * LANES:(j + 1) * LANES] = (
                    s_scr[j, pl.ds(s, nchunk, stride=nseq), :].astype(BF16))

        for j in range(nu * LANES // cw):
            c0 = j * cw
            y = _dot(u_ref[row0:row0 + r, :], wm[:, c0:c0 + cw])
            for q in range(4):
                off = c0 + (LANES if q % 2 == 0 else 0)
                y = y + _dot(hb_scr[0:r, q * half:(q + 1) * half], wwy[q * half:(q + 1) * half, off:off + cw])
            g = _gelu_tanh(y)
            for t2 in range(cw // LANES):
                tau = j * (cw // LANES) + t2
                y_scr[pl.ds(row0 * nu + tau, r, stride=nu), :] = g[:, t2 * LANES:(t2 + 1) * LANES]

    y_ref[...] = y_scr[...].astype(y_ref.dtype)


def _s5_call(u_rows, ws_cmp, wy_cmp, a_step, d_rows, h0, paths):
    rows = u_rows.shape[0]
    n_tok = rows * S5_CHUNK
    sc, k = S5_SCOLS, S5_CHUNK * LANES
    rmax = max(p[1] * p[2] for p in paths)
    nseq_fin = [p[1] for p in paths if not p[3]][0]
    blk = lambda a, b: pl.BlockSpec((None, a, b), lambda g: (g, 0, 0))
    return pl.pallas_call(
        functools.partial(_s5_kernel, paths=paths),
        grid=(S5_NGB,),
        in_specs=[pl.BlockSpec((rows, k), lambda g: (0, g)),
                  pl.BlockSpec((S5_CHUNK, None, LANES, ws_cmp.shape[-1]), lambda g: (0, g, 0, 0)),
                  pl.BlockSpec((2, 2, None, sc // 4, wy_cmp.shape[-1]), lambda g: (0, 0, g, 0, 0)),
                  blk(1, LANES), blk(2, sc // 2), blk(h0.shape[1], sc)],
        out_specs=[pl.BlockSpec((n_tok, LANES), lambda g: (0, g)), blk(nseq_fin, sc)],
        out_shape=[jax.ShapeDtypeStruct((n_tok, S5_WIDTH), BF16),
                   jax.ShapeDtypeStruct((S5_NGB, nseq_fin, sc), F32)],
        scratch_shapes=[pltpu.VMEM((k, k), BF16), pltpu.VMEM((k, sc), BF16),
                        pltpu.VMEM((sc, (S5_CHUNK + 1) * LANES), BF16),
                        pltpu.VMEM((4 * (sc // 4 // LANES), rmax, LANES), F32), pltpu.VMEM((rmax, sc), BF16),
                        pltpu.VMEM((n_tok, LANES), F32)],
        compiler_params=_params("arbitrary"),
        name="s5",
    )(u_rows, ws_cmp, wy_cmp, d_rows, a_step, h0)


def _state_to_lanes(s_re, s_im):
    b = s_re.shape[0]

    def one(x):
        return jnp.transpose(x.reshape(b, 2, S5_NGB, S5_GB * S5_STATE), (2, 0, 1, 3)).reshape(S5_NGB, b, -1)

    return jnp.concatenate([one(s_re), one(s_im)], axis=-1)


def _lanes_to_state(h):
    b = h.shape[1]
    x = h.reshape(S5_NGB, b, 2, 2, S5_GB, S5_STATE)
    x = jnp.transpose(x, (2, 1, 3, 0, 4, 5)).reshape(2, b, 2, S5_NGB * S5_GB, S5_STATE)
    return x[0], x[1]


def _ret_kernel(lg_ref, q_ref, k_ref, v_ref, g_ref, gn_ref, *rest, t, chunk, group, latent, want_state):
    rest = list(rest)
    if latent:
        cos_ref, sin_ref, s0_ref = rest[:3]
        rest = rest[3:]
    o_ref = rest.pop(0)
    sfin_ref = rest.pop(0) if want_state else None
    dmat_ref = rest.pop(0)
    n = t // chunk

    h = pl.program_id(0)
    lgf = lg_ref[0, h]
    lgb = lg_ref[1, h]

    @pl.when(pl.program_id(1) == 0)
    def _():
        ti = lax.broadcasted_iota(jnp.int32, (chunk, chunk), 0)
        si = lax.broadcasted_iota(jnp.int32, (chunk, chunk), 1)
        diff = (ti - si).astype(F32)
        dmat_ref[...] = jnp.exp(jnp.where(diff >= 0, lgf * diff, -lgb * diff))

    pos = lax.broadcasted_iota(jnp.int32, (chunk, 1), 0).astype(F32)
    q_dec_f = jnp.exp(lgf * (pos + 1.0))
    q_dec_b = jnp.exp(lgb * (chunk - pos))
    k_dec_f = jnp.exp(lgf * (chunk - 1.0 - pos))
    k_dec_b = jnp.exp(lgb * pos)
    step_f = jnp.exp(lgf * chunk)
    step_b = jnp.exp(lgb * chunk)
    tn = (((0,), (0,)), ((), ()))
    if latent:
        lane = lax.broadcasted_iota(jnp.int32, (chunk, RET_DK), 1)
        first = (lane % (RET_DK // 2)) < (RET_DK // 4)

        def rope(x, rows):
            swapped = jnp.where(first, pltpu.roll(x, RET_DK - RET_DK // 4, 1), pltpu.roll(x, RET_DK // 4, 1))
            return x * cos_ref[rows, :] + swapped * sin_ref[rows, :]

    for j in range(group):
        qs, kf, kr, vs, outs = [], [], [], [], []
        for c in range(n):
            rows = slice(j * t + c * chunk, j * t + (c + 1) * chunk)
            q = q_ref[rows, :].astype(F32) * (RET_DK ** -0.5)
            k = k_ref[rows, :].astype(F32)
            if latent:
                q = rope(q, slice(c * chunk, (c + 1) * chunk))
                k = rope(k, slice(c * chunk, (c + 1) * chunk))
            qb = q.astype(BF16)
            v = v_ref[rows, :]
            scores = lax.dot_general(qb, k.astype(BF16), (((1,), (1,)), ((), ())), preferred_element_type=F32)
            outs.append(_dot((scores * dmat_ref[...]).astype(BF16), v))
            qs.append(qb)
            vs.append(v)
            kf.append((k * k_dec_f).astype(BF16))
            kr.append((k * k_dec_b).astype(BF16))

        state = s0_ref[j, 0] if latent else None
        for c in range(n):
            if state is not None:
                outs[c] = outs[c] + _dot(qs[c], state.astype(BF16)) * q_dec_f
            if c < n - 1 or want_state:
                upd = lax.dot_general(kf[c], vs[c], tn, preferred_element_type=F32)
                state = upd if state is None else step_f * state + upd
        if want_state:
            sfin_ref[j, 0] = state

        state = s0_ref[j, 1] if latent else None
        for c in reversed(range(n)):
            if state is not None:
                outs[c] = outs[c] + _dot(qs[c], state.astype(BF16)) * q_dec_b
            if c > 0 or want_state:
                upd = lax.dot_general(kr[c], vs[c], tn, preferred_element_type=F32)
                state = upd if state is None else step_b * state + upd
        if want_state:
            sfin_ref[j, 1] = state

        for c in range(n):
            rows = slice(j * t + c * chunk, j * t + (c + 1) * chunk)
            o = outs[c]
            mu = jnp.mean(o, axis=-1, keepdims=True)
            d = o - mu
            var = jnp.mean(d * d, axis=-1, keepdims=True)
            on = d * lax.rsqrt(var + EPS) * gn_ref[...]
            g = g_ref[rows, :].astype(F32)
            o_ref[rows, :] = (g * _sigmoid(g) * on).astype(o_ref.dtype)


def _ret_call(proj, log_gamma, gn_w, rope_tabs, s0, *, row0, nseq, t, chunk, group, latent, want_state):
    h, dk, dv = RET_HEADS, RET_DK, RET_DV
    gt = group * t
    rb0 = row0 // gt
    q0 = 0
    k0 = q0 + h
    v0 = (2 * h * dk) // dv
    g0 = v0 + h
    in_specs = [pl.BlockSpec(memory_space=pltpu.SMEM),
                pl.BlockSpec((gt, dk), lambda hh, b: (rb0 + b, q0 + hh)),
                pl.BlockSpec((gt, dk), lambda hh, b: (rb0 + b, k0 + hh)),
                pl.BlockSpec((gt, dv), lambda hh, b: (rb0 + b, v0 + hh)),
                pl.BlockSpec((gt, dv), lambda hh, b: (rb0 + b, g0 + hh)),
                pl.BlockSpec((1, dv), lambda hh, b: (0, hh))]
    args = [log_gamma, proj, proj, proj, proj, gn_w]
    if latent:
        in_specs += [pl.BlockSpec((t, dk), lambda hh, b: (0, 0)),
                     pl.BlockSpec((t, dk), lambda hh, b: (0, 0)),
                     pl.BlockSpec((group, 2, None, dk, dv), lambda hh, b: (b, 0, hh, 0, 0))]
        args += [rope_tabs[0], rope_tabs[1], s0]
    out_specs = [pl.BlockSpec((gt, dv), lambda hh, b: (b, hh))]
    out_shape = [jax.ShapeDtypeStruct((nseq * t, h * dv), BF16)]
    if want_state:
        out_specs.append(pl.BlockSpec((group, 2, None, dk, dv), lambda hh, b: (b, 0, hh, 0, 0)))
        out_shape.append(jax.ShapeDtypeStruct((nseq, 2, h, dk, dv), F32))
    return pl.pallas_call(
        functools.partial(_ret_kernel, t=t, chunk=chunk, group=group, latent=latent, want_state=want_state),
        grid=(h, nseq // group),
        in_specs=in_specs,
        out_specs=out_specs,
        out_shape=out_shape,
        scratch_shapes=[pltpu.VMEM((chunk, chunk), F32)],
        compiler_params=_params("arbitrary", "arbitrary"),
        name="retention_latent" if latent else "retention_ctx",
    )(*args)


def _grid_rope_tables(t):
    quarter = RET_DK // 4
    freqs = ROPE_BASE ** (-jnp.arange(quarter, dtype=F32) / quarter)
    pos = jnp.arange(t)
    row = (pos // GRID_W).astype(F32)[:, None] * freqs[None, :]
    col = (pos % GRID_W).astype(F32)[:, None] * freqs[None, :]
    cos = jnp.concatenate([jnp.cos(row), jnp.cos(row), jnp.cos(col), jnp.cos(col)], axis=-1)
    sin = jnp.concatenate([-jnp.sin(row), jnp.sin(row), -jnp.sin(col), jnp.sin(col)], axis=-1)
    return cos, sin


def _merge_kernel(yg_ref, op_ref, os_ref, ga_ref, gb_ref, wga_ref, wgb_ref, wr_ref, out_ref, wga_b, wgb_b, wr_b,
                  *, n_prompt_tiles):
    i = pl.program_id(1)

    @pl.when(i == 0)
    def _():
        wga_b[...] = wga_ref[...].astype(BF16)
        wgb_b[...] = wgb_ref[...].astype(BF16)
        wr_b[...] = wr_ref[...].astype(BF16)

    def body(o_ref):
        yg = yg_ref[...]
        za = _dot(yg, wga_b[...])
        zb = _dot(yg, wgb_b[...])
        ob = _dot(o_ref[...], wr_b[...])
        out_a = za * _sigmoid(zb)
        ga = ga_ref[...].astype(F32)
        gb = gb_ref[...].astype(F32)
        out_ref[...] = (_sigmoid(ga) * out_a + _sigmoid(gb) * ob).astype(out_ref.dtype)

    @pl.when(i < n_prompt_tiles)
    def _():
        body(op_ref)

    @pl.when(i >= n_prompt_tiles)
    def _():
        body(os_ref)


def _merge_call(yg, o_p, o_s, proj, w_glu, w_ret_out):
    m, ks = yg.shape
    kr = o_p.shape[1]
    d = w_ret_out.shape[1]
    tm = 1024 if m % 1024 == 0 else m
    tn = 512 if d % 512 == 0 else d
    ga0 = (proj.shape[1] - 2 * d) // tn
    gb0 = (proj.shape[1] - d) // tn
    nb = d // tn
    npt = o_p.shape[0] // tm
    return pl.pallas_call(
        functools.partial(_merge_kernel, n_prompt_tiles=npt),
        grid=(nb, m // tm),
        in_specs=[pl.BlockSpec((tm, ks), lambda j, i: (i, 0)),
                  pl.BlockSpec((tm, kr), lambda j, i: (jnp.minimum(i, npt - 1), 0)),
                  pl.BlockSpec((tm, kr), lambda j, i: (jnp.maximum(i - npt, 0), 0)),
                  pl.BlockSpec((tm, tn), lambda j, i: (i, ga0 + j)),
                  pl.BlockSpec((tm, tn), lambda j, i: (i, gb0 + j)),
                  pl.BlockSpec((ks, tn), lambda j, i: (0, j)),
                  pl.BlockSpec((ks, tn), lambda j, i: (0, nb + j)),
                  pl.BlockSpec((kr, tn), lambda j, i: (0, j))],
        out_specs=pl.BlockSpec((tm, tn), lambda j, i: (i, j)),
        out_shape=jax.ShapeDtypeStruct((m, d), BF16),
        scratch_shapes=[pltpu.VMEM((ks, tn), BF16), pltpu.VMEM((ks, tn), BF16), pltpu.VMEM((kr, tn), BF16)],
        compiler_params=_params("arbitrary", "arbitrary"),
        name="merge",
    )(yg, o_p, o_s, proj, proj, w_glu, w_glu, w_ret_out)


def _outproj_kernel(mg_ref, xp_ref, xs_ref, w_ref, g1_ref, sc_ref, sh_ref, n2_ref, wr_ref,
                    x1_ref, h2_ref, aff_ref, *, n_prompt_tiles, n_experts):
    i = pl.program_id(0)
    upd = g1_ref[...] * _dot(mg_ref[...], w_ref[...])

    def body(x_ref):
        x1 = x_ref[...] + upd
        x1_ref[...] = x1
        y = x1 * lax.rsqrt(jnp.mean(x1 * x1, axis=-1, keepdims=True) + EPS)
        h2 = (y * n2_ref[...]) * (1.0 + sc_ref[...]) + sh_ref[...]
        h2_ref[...] = h2.astype(h2_ref.dtype)
        wr = wr_ref[...]
        w_hi = wr.astype(BF16)
        w_lo = (wr - w_hi.astype(F32)).astype(BF16)
        h_hi = h2.astype(BF16)
        h_lo = (h2 - h_hi.astype(F32)).astype(BF16)
        hw = _dot(h_hi, jnp.concatenate([w_hi, w_lo], axis=1))
        logits = hw[:, :LANES] + (hw[:, LANES:] + _dot(h_lo, w_hi))
        lane = lax.broadcasted_iota(jnp.int32, logits.shape, 1)
        logits = jnp.where(lane < n_experts, logits, -jnp.inf)
        e = jnp.exp(logits - jnp.max(logits, axis=-1, keepdims=True))
        aff_ref[...] = e / jnp.sum(e, axis=-1, keepdims=True)

    @pl.when(i < n_prompt_tiles)
    def _():
        body(xp_ref)

    @pl.when(i >= n_prompt_tiles)
    def _():
        body(xs_ref)


def _outproj_call(merged, xp, xs, w_out_b, gate1, scale2, shift2, norm2, w_router_pad, t_sample):
    m, d = merged.shape
    n_p = xp.shape[0]
    tm = 2 * ROW_TILE
    npt = n_p // tm
    tps = t_sample // tm
    row = functools.partial(_mod_row, n_prompt_tiles=npt, tiles_per_sample_seq=tps)
    mod = pl.BlockSpec((None, 1, d), lambda i: (row(i), 0, 0))
    return pl.pallas_call(
        functools.partial(_outproj_kernel, n_prompt_tiles=npt, n_experts=N_EXPERTS),
        grid=(m // tm,),
        in_specs=[pl.BlockSpec((tm, d), lambda i: (i, 0)),
                  pl.BlockSpec((tm, d), lambda i: (jnp.minimum(i, npt - 1), 0)),
                  pl.BlockSpec((tm, d), lambda i: (jnp.maximum(i - npt, 0), 0)),
                  pl.BlockSpec((d, d), lambda i: (0, 0)),
                  mod, mod, mod,
                  pl.BlockSpec((1, d), lambda i: (0, 0)),
                  pl.BlockSpec((d, LANES), lambda i: (0, 0))],
        out_specs=[pl.BlockSpec((tm, d), lambda i: (i, 0)),
                   pl.BlockSpec((tm, d), lambda i: (i, 0)),
                   pl.BlockSpec((tm, LANES), lambda i: (i, 0))],
        out_shape=[jax.ShapeDtypeStruct((m, d), F32),
                   jax.ShapeDtypeStruct((m, d), BF16),
                   jax.ShapeDtypeStruct((m, LANES), F32)],
        compiler_params=_params("arbitrary"),
        name="out_proj_router",
    )(merged, xp, xs, w_out_b, gate1, scale2, shift2, norm2, w_router_pad)


def _select_kernel(a_ref, slot_ref, *, cap):
    a = a_ref[...]
    r, t = a.shape

    def as_float(bits):
        return pltpu.bitcast(bits, F32)

    def bisect(_, carry):
        lo, hi = carry
        mid = lo + jnp.right_shift(hi - lo + 1, 1)
        cnt = jnp.sum(jnp.where(a >= as_float(mid), 1.0, 0.0), axis=-1, keepdims=True)
        ok = cnt >= cap
        return jnp.where(ok, mid, lo), jnp.where(ok, hi, mid - 1)

    lo0 = jnp.zeros((r, 1), jnp.int32)
    hi0 = jnp.full((r, 1), 0x3F800000, jnp.int32)
    thr, _ = lax.fori_loop(0, 31, bisect, (lo0, hi0))

    gt = jnp.where(a >= as_float(thr + 1), 1.0, 0.0)
    eq = jnp.where(a >= as_float(thr), 1.0, 0.0) - gt
    need = cap - jnp.sum(gt, axis=-1, keepdims=True)
    before = lax.broadcasted_iota(jnp.int32, (t, t), 0) < lax.broadcasted_iota(jnp.int32, (t, t), 1)
    tri = jnp.where(before, 1.0, 0.0).astype(BF16)
    eq_rank = _dot(eq.astype(BF16), tri)
    sel = gt + eq * jnp.where(eq_rank < need, 1.0, 0.0)
    pos = _dot(sel.astype(BF16), tri)
    slot_ref[...] = jnp.where(sel > 0.5, pos, -1.0).astype(jnp.int32)


def _select_call(aff_t, cap):
    r, t = aff_t.shape
    return pl.pallas_call(
        functools.partial(_select_kernel, cap=cap),
        grid=(1,),
        in_specs=[pl.BlockSpec((r, t), lambda i: (0, 0))],
        out_specs=pl.BlockSpec((r, t), lambda i: (0, 0)),
        out_shape=jax.ShapeDtypeStruct((r, t), jnp.int32),
        compiler_params=_params("arbitrary"),
        name="select",
    )(aff_t)


def _gather_kernel(slot_ref, aff_ref, h_ref, xs_ref, gate_ref, *, cap, group):
    e_total, t = slot_ref.shape
    h = h_ref[...]
    ci = lax.broadcasted_iota(jnp.int32, (cap, t), 0)
    for e0 in range(0, e_total, group):
        hots = []
        for e in range(e0, e0 + group):
            hit = ci == slot_ref[e:e + 1, :]
            hots.append(jnp.where(hit, 1.0, 0.0).astype(BF16))
            gate_ref[e] = jnp.sum(jnp.where(hit, aff_ref[e:e + 1, :], 0.0), axis=-1, keepdims=True)
        onehot = hots[0] if group == 1 else jnp.concatenate(hots, axis=0)
        xs = _dot(onehot, h).astype(xs_ref.dtype)
        xs_ref[e0:e0 + group] = xs.reshape(group, cap, xs.shape[-1])


def _gather_call(slot_t, aff_t, h2, *, row0, nseq, t, cap):
    e = slot_t.shape[1]
    d = h2.shape[1]
    rb0 = row0 // t
    group = max(1, min(e, 512 // cap))
    return pl.pallas_call(
        functools.partial(_gather_kernel, cap=cap, group=group),
        grid=(nseq,),
        in_specs=[pl.BlockSpec((None, e, t), lambda b: (b, 0, 0)),
                  pl.BlockSpec((None, e, t), lambda b: (b, 0, 0)),
                  pl.BlockSpec((t, d), lambda b: (rb0 + b, 0))],
        out_specs=[pl.BlockSpec((e, cap, d), lambda b: (0, b, 0)),
                   pl.BlockSpec((e, cap, 1), lambda b: (0, b, 0))],
        out_shape=[jax.ShapeDtypeStruct((e, nseq * cap, d), BF16),
                   jax.ShapeDtypeStruct((e, nseq * cap, 1), F32)],
        compiler_params=_params("arbitrary"),
        name="gather",
    )(slot_t, aff_t, h2)


def _ffn_kernel(xp_ref, xs_ref, gp_ref, gs_ref, wg_ref, wu_ref, wd_ref, yp_ref, ys_ref,
                accp, accs, wgb, wub, wdb, *, chunk):
    f = pl.program_id(1)

    @pl.when(f == 0)
    def _():
        accp[...] = jnp.zeros_like(accp)
        accs[...] = jnp.zeros_like(accs)

    wgb[...] = wg_ref[...].astype(BF16)
    wub[...] = wu_ref[...].astype(BF16)
    wdb[...] = wd_ref[...].astype(BF16)

    def part(x_ref, acc):
        m = x_ref.shape[0]
        mc = min(chunk, m)
        for m0 in range(0, m, mc):
            x = x_ref[m0:m0 + mc, :]
            hg = _dot(x, wgb[...])
            hu = _dot(x, wub[...])
            hid = (hg * _sigmoid(hg) * hu).astype(BF16)
            acc[m0:m0 + mc, :] += _dot(hid, wdb[...])

    part(xp_ref, accp)
    part(xs_ref, accs)

    @pl.when(f == pl.num_programs(1) - 1)
    def _():
        yp_ref[...] = (accp[...] * gp_ref[...]).astype(yp_ref.dtype)
        ys_ref[...] = (accs[...] * gs_ref[...]).astype(ys_ref.dtype)


def _ffn_call(xs_p, xs_s, gate_p, gate_s, w_gate, w_up, w_down):
    e, mp, d = xs_p.shape
    ms = xs_s.shape[1]
    ff = w_gate.shape[2]
    tf = 256 if ff % 256 == 0 else ff
    tok = lambda m, w: pl.BlockSpec((None, m, w), lambda ee, f: (ee, 0, 0))
    return pl.pallas_call(
        functools.partial(_ffn_kernel, chunk=512),
        grid=(e, ff // tf),
        in_specs=[tok(mp, d), tok(ms, d), tok(mp, 1), tok(ms, 1),
                  pl.BlockSpec((None, d, tf), lambda ee, f: (ee, 0, f)),
                  pl.BlockSpec((None, d, tf), lambda ee, f: (ee, 0, f)),
                  pl.BlockSpec((None, tf, d), lambda ee, f: (ee, f, 0))],
        out_specs=[pl.BlockSpec((None, mp, d), lambda ee, f: (ee, 0, 0), pipeline_mode=pl.Buffered(1)),
                   pl.BlockSpec((None, ms, d), lambda ee, f: (ee, 0, 0), pipeline_mode=pl.Buffered(1))],
        out_shape=[jax.ShapeDtypeStruct((e, mp, d), BF16), jax.ShapeDtypeStruct((e, ms, d), BF16)],
        scratch_shapes=[pltpu.VMEM((mp, d), F32), pltpu.VMEM((ms, d), F32),
                        pltpu.VMEM((d, tf), BF16), pltpu.VMEM((d, tf), BF16), pltpu.VMEM((tf, d), BF16)],
        compiler_params=_params("arbitrary", "arbitrary"),
        name="expert_ffn",
    )(xs_p, xs_s, gate_p, gate_s, w_gate, w_up, w_down)


def _scatter_kernel(slot_ref, y_ref, x1_ref, g2_ref, wn_ref, o_ref, *, cap):
    e_total = y_ref.shape[0]
    tm = slot_ref.shape[0]
    slot = slot_ref[...]
    lane = lax.broadcasted_iota(jnp.int32, (tm, LANES), 1)
    per_block = max(1, LANES // cap)
    blocks = []
    for b0 in range(0, e_total, per_block):
        acc = jnp.zeros((tm, LANES), F32)
        for j in range(per_block):
            s = slot[:, b0 + j:b0 + j + 1]
            key = jnp.where(s >= 0, s + j * cap, -1)
            acc = acc + jnp.where(lane == key, 1.0, 0.0)
        blocks.append(acc.astype(BF16))
    onehot = jnp.concatenate(blocks, axis=1)
    y = y_ref[...].reshape(e_total * cap, y_ref.shape[-1])
    moe = _dot(onehot, y)
    x2 = x1_ref[...] + g2_ref[...] * moe
    o_ref[...] = x2 * lax.rsqrt(jnp.mean(x2 * x2, axis=-1, keepdims=True) + EPS) * wn_ref[...]


def _scatter_call(slot, y, x1, gate2, final_norm, *, row0, nseq, t, cap, mod_row0, mod_per_seq):
    e, _, d = y.shape
    tm = min(2 * ROW_TILE, t)
    nt = t // tm
    rb0 = row0 // tm
    assert cap == LANES or LANES % cap == 0
    return pl.pallas_call(
        functools.partial(_scatter_kernel, cap=cap),
        grid=(nseq, nt),
        in_specs=[pl.BlockSpec((None, tm, e), lambda b, i: (b, i, 0)),
                  pl.BlockSpec((e, cap, d), lambda b, i: (0, b, 0)),
                  pl.BlockSpec((tm, d), lambda b, i: (rb0 + b * nt + i, 0)),
                  pl.BlockSpec((None, 1, d), lambda b, i: (mod_row0 + b * mod_per_seq, 0, 0)),
                  pl.BlockSpec((1, d), lambda b, i: (0, 0))],
        out_specs=pl.BlockSpec((tm, d), lambda b, i: (b * nt + i, 0)),
        out_shape=jax.ShapeDtypeStruct((nseq * t, d), F32),
        compiler_params=_params("arbitrary", "arbitrary"),
        name="scatter_final",
    )(slot, y, x1, gate2, final_norm)


def kernel(x_prompt, x_sample, state_s5_re, state_s5_im, state_ret, c, c_ctx, final_norm, w_ada, b_ada, norm1, norm2, w_in, s5_a_re, s5_a_im, s5_log_dt, s5_b_re, s5_b_im, s5_c_re, s5_c_im, s5_d, w_s5_glu, ret_decay_logit, ret_gn_w, w_ret_out, w_out, w_router, w_exp_gate, w_exp_up, w_exp_down):
    bp, tp, d = x_prompt.shape
    bs, ts, _ = x_sample.shape
    depth = w_ada.shape[0]
    n_p, n_s = bp * tp, bs * ts
    xp = x_prompt.reshape(n_p, d)
    xs = x_sample.reshape(n_s, d)

    mod_rows = 16
    cvec = jnp.zeros((mod_rows, d), F32).at[0].set(c_ctx).at[1:1 + bs].set(c)
    rope_tabs = _grid_rope_tables(ts)
    cap_p = CAPACITY_FACTOR * tp // N_EXPERTS
    cap_s = CAPACITY_FACTOR * ts // N_EXPERTS

    new_re, new_im, new_ret = [], [], []
    for l in range(depth):
        mods = _ada_call(cvec, w_ada[l], b_ada[l])
        shift1, scale1, gate1, shift2, scale2, gate2 = [m.reshape(mod_rows, 1, d) for m in jnp.split(mods, 6, axis=-1)]

        h, u_rows = _uproj_call(xp, xs, scale1, shift1, norm1[l].reshape(1, d), w_in[l], ts)
        proj = _proj_call(h, w_in[l], S5_WIDTH)

        ws_cmp, wy_cmp, a_step, d_rows = _s5_chunk_weights(
            s5_a_re[l], s5_a_im[l], s5_log_dt[l], s5_b_re[l], s5_b_im[l], s5_c_re[l], s5_c_im[l], s5_d[l])
        h0_s = _state_to_lanes(state_s5_re[:, l].astype(F32), state_s5_im[:, l].astype(F32))
        s5_paths = ((0, bp, tp // S5_CHUNK, False), (n_p // S5_CHUNK, bs, ts // S5_CHUNK, True))
        yg, hfin_p = _s5_call(u_rows, ws_cmp, wy_cmp, a_step, d_rows, h0_s, s5_paths)
        s5_re, s5_im = _lanes_to_state(hfin_p)
        new_re.append(s5_re)
        new_im.append(s5_im)

        log_gamma = jax.nn.log_sigmoid(ret_decay_logit[l].astype(F32))
        gn_w = ret_gn_w[l].reshape(1, -1).astype(F32)
        o_p, sfin = _ret_call(proj, log_gamma, gn_w, None, None, row0=0, nseq=bp, t=tp, chunk=RET_CHUNK, group=4, latent=False,
                              want_state=True)
        o_s, = _ret_call(proj, log_gamma, gn_w, rope_tabs, state_ret[:, l].astype(F32), row0=n_p, nseq=bs, t=ts,
                         chunk=RET_CHUNK, group=4, latent=True, want_state=False)
        new_ret.append(sfin)
        merged = _merge_call(yg, o_p, o_s, proj, w_s5_glu[l], w_ret_out[l])
        w_router_pad = jnp.zeros((d, LANES), F32).at[:, :N_EXPERTS].set(w_router[l].astype(F32))
        x1, h2, aff = _outproj_call(merged, xp, xs, w_out[l].astype(BF16), gate1, scale2, shift2,
                                    norm2[l].reshape(1, d), w_router_pad, ts)

        aff = aff[:, :N_EXPERTS]
        aff_p = jnp.transpose(aff[:n_p].reshape(bp, tp, N_EXPERTS), (0, 2, 1))
        aff_s = jnp.transpose(aff[n_p:].reshape(bs, ts, N_EXPERTS), (0, 2, 1))
        slot_p = _select_call(aff_p.reshape(bp * N_EXPERTS, tp), cap_p).reshape(bp, N_EXPERTS, tp)
        slot_s = _select_call(aff_s.reshape(bs * N_EXPERTS, ts), cap_s).reshape(bs, N_EXPERTS, ts)
        xe_p, ge_p = _gather_call(slot_p, aff_p, h2, row0=0, nseq=bp, t=tp, cap=cap_p)
        xe_s, ge_s = _gather_call(slot_s, aff_s, h2, row0=n_p, nseq=bs, t=ts, cap=cap_s)
        ye_p, ye_s = _ffn_call(xe_p, xe_s, ge_p, ge_s, w_exp_gate[l], w_exp_up[l], w_exp_down[l])

        last = l == depth - 1
        wn = final_norm.reshape(1, d).astype(F32) if last else None
        assert last, "only the final layer applies the output norm in the scatter kernel"
        yp = _scatter_call(jnp.transpose(slot_p, (0, 2, 1)), ye_p, x1, gate2, wn, row0=0, nseq=bp, t=tp, cap=cap_p,
                           mod_row0=0, mod_per_seq=0)
        ysm = _scatter_call(jnp.transpose(slot_s, (0, 2, 1)), ye_s, x1, gate2, wn, row0=n_p, nseq=bs, t=ts, cap=cap_s,
                            mod_row0=1, mod_per_seq=1)

    y_prompt = yp.reshape(bp, tp, d)
    y_sample = ysm.reshape(bs, ts, d)
    return (y_prompt, y_sample, jnp.stack(new_re, axis=1), jnp.stack(new_im, axis=1), jnp.stack(new_ret, axis=1))
```

```python
import functools
import math

import jax
import jax.numpy as jnp
from jax import lax
from jax.experimental import pallas as pl
from jax.experimental.pallas import tpu as pltpu

F32 = jnp.float32
BF16 = jnp.bfloat16

EPS = 1e-6
GRID_W = 64
S5_WIDTH = 1024
S5_GROUP = 16
S5_STATE = 64
RET_HEADS = 8
RET_DK = 128
RET_DV = 256
ROPE_BASE = 10000.0
RET_CHUNK = 256
N_EXPERTS = 16
CAPACITY_FACTOR = 2

LANES = 128
S5_CHUNK = 8
S5_GB = LANES // S5_GROUP
S5_NGB = S5_WIDTH // LANES
S5_SCOLS = 4 * S5_GB * S5_STATE
ROW_TILE = 256
VMEM_LIMIT = 56 * 1024 * 1024


def _params(*sem):
    return pltpu.CompilerParams(dimension_semantics=sem, vmem_limit_bytes=VMEM_LIMIT)


def _sigmoid(x):
    return 1.0 / (1.0 + jnp.exp(-x))


def _gelu_tanh(x):
    return 0.5 * x * (1.0 + jnp.tanh(math.sqrt(2.0 / math.pi) * (x + 0.044715 * (x * x * x))))


def _dot(a, b):
    return jnp.dot(a, b, preferred_element_type=F32)


def _ada_kernel(c_ref, w_ref, b_ref, o_ref):
    c = c_ref[...]
    s = c * _sigmoid(c)
    w = w_ref[...]
    s_hi = s.astype(BF16)
    s_lo = (s - s_hi.astype(F32)).astype(BF16)
    w_hi = w.astype(BF16)
    w_lo = (w - w_hi.astype(F32)).astype(BF16)
    o_ref[...] = _dot(s_hi, w_hi) + (_dot(s_hi, w_lo) + _dot(s_lo, w_hi)) + b_ref[...]


def _ada_call(cvec, w_ada, b_ada):
    r, d = cvec.shape
    n = w_ada.shape[1]
    tn = 1024 if n % 1024 == 0 else 512
    return pl.pallas_call(
        _ada_kernel,
        grid=(n // tn,),
        in_specs=[pl.BlockSpec((r, d), lambda j: (0, 0)),
                  pl.BlockSpec((d, tn), lambda j: (0, j)),
                  pl.BlockSpec((1, tn), lambda j: (0, j))],
        out_specs=pl.BlockSpec((r, tn), lambda j: (0, j)),
        out_shape=jax.ShapeDtypeStruct((r, n), F32),
        compiler_params=_params("arbitrary"),
        name="ada",
    )(cvec, w_ada, b_ada.reshape(1, n))


def _mod_row(i, n_prompt_tiles, tiles_per_sample_seq):
    return jnp.where(i < n_prompt_tiles, 0, 1 + (i - n_prompt_tiles) // tiles_per_sample_seq)


def _proj_kernel(a_ref, w_ref, o_ref, wb_ref):
    @pl.when(pl.program_id(1) == 0)
    def _():
        wb_ref[...] = w_ref[...].astype(BF16)

    o_ref[...] = _dot(a_ref[...], wb_ref[...]).astype(o_ref.dtype)


def _proj_call(a, w, col0):
    m, k = a.shape
    n = w.shape[1] - col0
    tm = next((c for c in (1536, 1024) if m % c == 0), m)
    tn = 1024 if n % 1024 == 0 else 512
    cb0 = col0 // tn
    return pl.pallas_call(
        _proj_kernel,
        grid=(n // tn, m // tm),
        in_specs=[pl.BlockSpec((tm, k), lambda j, i: (i, 0)),
                  pl.BlockSpec((k, tn), lambda j, i: (0, cb0 + j))],
        out_specs=pl.BlockSpec((tm, tn), lambda j, i: (i, j)),
        out_shape=jax.ShapeDtypeStruct((m, n), BF16),
        scratch_shapes=[pltpu.VMEM((k, tn), BF16)],
        compiler_params=_params("arbitrary", "arbitrary"),
        name="in_proj",
    )(a, w)


def _cmul(ar, ai, br, bi):
    return ar * br - ai * bi, ar * bi + ai * br


def _uproj_kernel(xp_ref, xs_ref, sc_ref, sh_ref, nw_ref, w_ref, h_ref, o_ref, wb_ref, r_scr, *, n_prompt_tiles):
    i = pl.program_id(0)

    @pl.when(i == 0)
    def _():
        wb_ref[...] = w_ref[...].astype(BF16)

    def body(x_ref):
        x = x_ref[...]
        y = x * lax.rsqrt(jnp.mean(x * x, axis=-1, keepdims=True) + EPS)
        hb = ((y * nw_ref[...]) * (1.0 + sc_ref[...]) + sh_ref[...]).astype(BF16)
        h_ref[...] = hb
        res = _dot(hb, wb_ref[...])
        for g in range(S5_NGB):
            r_scr[g] = res[:, g * LANES:(g + 1) * LANES]

    @pl.when(i < n_prompt_tiles)
    def _():
        body(xp_ref)

    @pl.when(i >= n_prompt_tiles)
    def _():
        body(xs_ref)

    rows = h_ref.shape[0] // S5_CHUNK
    for g in range(S5_NGB):
        for tau in range(S5_CHUNK):
            c0 = (g * S5_CHUNK + tau) * LANES
            o_ref[:, c0:c0 + LANES] = r_scr[g, pl.ds(tau, rows, stride=S5_CHUNK), :].astype(o_ref.dtype)


def _uproj_call(xp, xs, scale, shift, nw, w, t_sample):
    n_p, d = xp.shape
    m = n_p + xs.shape[0]
    tm = 2 * ROW_TILE
    npt = n_p // tm
    row = functools.partial(_mod_row, n_prompt_tiles=npt, tiles_per_sample_seq=t_sample // tm)
    mod = pl.BlockSpec((None, 1, d), lambda i: (row(i), 0, 0))
    return pl.pallas_call(
        functools.partial(_uproj_kernel, n_prompt_tiles=npt),
        grid=(m // tm,),
        in_specs=[pl.BlockSpec((tm, d), lambda i: (jnp.minimum(i, npt - 1), 0)),
                  pl.BlockSpec((tm, d), lambda i: (jnp.maximum(i - npt, 0), 0)),
                  mod, mod,
                  pl.BlockSpec((1, d), lambda i: (0, 0)),
                  pl.BlockSpec((d, S5_WIDTH), lambda i: (0, 0))],
        out_specs=[pl.BlockSpec((tm, d), lambda i: (i, 0)),
                   pl.BlockSpec((tm // S5_CHUNK, S5_CHUNK * S5_WIDTH), lambda i: (i, 0))],
        out_shape=[jax.ShapeDtypeStruct((m, d), BF16),
                   jax.ShapeDtypeStruct((m // S5_CHUNK, S5_CHUNK * S5_WIDTH), BF16)],
        scratch_shapes=[pltpu.VMEM((d, S5_WIDTH), BF16), pltpu.VMEM((S5_NGB, tm, LANES), F32)],
        compiler_params=_params("arbitrary"),
        name="norm_u_proj",
    )(xp, xs, scale, shift, nw, w)


def _s5_chunk_weights(a_re, a_im, log_dt, b_re, b_im, c_re, c_im, d_skip):
    L, G, P, HG, GB, NGB = S5_CHUNK, S5_WIDTH // S5_GROUP, S5_STATE, S5_GROUP, S5_GB, S5_NGB
    a_re, a_im = a_re.astype(F32), a_im.astype(F32)
    dt = jnp.exp(log_dt.astype(F32))[..., None]
    adt_re, adt_im = a_re * dt, a_im * dt
    mag = jnp.exp(adt_re)
    lam_re, lam_im = mag * jnp.cos(adt_im), mag * jnp.sin(adt_im)
    den = a_re * a_re + a_im * a_im
    q_re = ((lam_re - 1.0) * a_re + lam_im * a_im) / den
    q_im = (lam_im * a_re - (lam_re - 1.0) * a_im) / den
    bb_re, bb_im = _cmul(q_re[..., None], q_im[..., None], b_re.astype(F32), b_im.astype(F32))
    n = jnp.arange(L + 1, dtype=F32)
    pmag = jnp.exp(adt_re[..., None] * n)
    pw_re, pw_im = pmag * jnp.cos(adt_im[..., None] * n), pmag * jnp.sin(adt_im[..., None] * n)

    def ws_exponents(pw, d):
        e = pw[0][..., :L][..., ::-1] if d == 0 else pw[1][..., :L]
        return jnp.transpose(e, (2, 0, 1))[:, :, None]

    def ws_dir(d):
        bt_re, bt_im = jnp.transpose(bb_re[d], (0, 2, 1)), jnp.transpose(bb_im[d], (0, 2, 1))
        return _cmul(ws_exponents(pw_re, d), ws_exponents(pw_im, d), bt_re[None], bt_im[None])

    (wf_re, wf_im), (wb_re, wb_im) = ws_dir(0), ws_dir(1)
    ws = jnp.concatenate([wf_re, wb_re, wf_im, wb_im], axis=-1).reshape(L, NGB, GB * HG, 4 * P)

    def wy_exponents(pw):
        return jnp.stack([pw[0], pw[1][..., ::-1]])

    ct_re = jnp.transpose(c_re.astype(F32), (0, 1, 3, 2))[:, :, :, None]
    ct_im = jnp.transpose(c_im.astype(F32), (0, 1, 3, 2))[:, :, :, None]
    y_re, y_im = _cmul(wy_exponents(pw_re)[..., None], wy_exponents(pw_im)[..., None], ct_re, ct_im)
    wy = jnp.stack([y_re, -y_im]).reshape(2, 2, NGB, GB * P, (L + 1) * HG)

    def lanes(x):
        return jnp.transpose(x.reshape(2, NGB, GB * P), (1, 0, 2)).reshape(NGB, 1, 2 * GB * P)

    a_step = jnp.concatenate([lanes(pw_re[..., L]), lanes(pw_im[..., L])], axis=1)
    return ws.astype(BF16), wy.astype(BF16), a_step, d_skip.astype(F32).reshape(NGB, 1, LANES)


def _s5_expand(xws_ref, zwy_ref, d_ref, wm, wws, wwy):
    L, hg, p = S5_CHUNK, S5_GROUP, S5_STATE
    lg_hg, lg_p, lg_lanes = hg.bit_length() - 1, p.bit_length() - 1, LANES.bit_length() - 1
    gmask = S5_GB - 1
    rblk = 512

    def expand(x, src_of_col, row_group, col_group, o_ref, cblk):
        rows, k = x.shape
        for c0 in range(0, o_ref.shape[1], cblk):
            kk = lax.broadcasted_iota(jnp.int32, (k, cblk), 0)
            cc = lax.broadcasted_iota(jnp.int32, (k, cblk), 1) + c0
            spread = jnp.where(kk == src_of_col(cc), 1.0, 0.0).astype(BF16)
            rr = lax.broadcasted_iota(jnp.int32, (rblk, cblk), 0)
            c2 = lax.broadcasted_iota(jnp.int32, (rblk, cblk), 1) + c0
            keep = row_group(rr) == col_group(c2)
            for r0 in range(0, rows, rblk):
                full = _dot(x[r0:r0 + rblk, :], spread)
                o_ref[r0:r0 + rblk, c0:c0 + cblk] = jnp.where(keep, full, 0.0).astype(o_ref.dtype)

    lane_group = lambda i: jnp.right_shift(i, lg_hg) & gmask
    state_group = lambda i: jnp.right_shift(i, lg_p) & gmask
    chunk_src = lambda c: jnp.right_shift(c, lg_lanes) * hg + (c & (hg - 1))
    state_src = lambda c: jnp.right_shift(c, lg_p + 3) * p + (c & (p - 1))
    xws = xws_ref[...]
    expand(xws.reshape(L * LANES, xws.shape[-1]), state_src, lane_group, state_group, wws, 1024)
    zwy = zwy_ref[...]
    expand(zwy.reshape(S5_SCOLS, zwy.shape[-1]), chunk_src, state_group, lane_group, wwy, wwy.shape[1])

    h = S5_SCOLS // 4
    last, first = wws[(L - 1) * LANES:L * LANES, :], wws[0:LANES, :]
    pf = _dot(last[:, 0:h], wwy[0:h, :]) + _dot(last[:, 2 * h:3 * h], wwy[2 * h:3 * h, :])
    pb = _dot(first[:, h:2 * h], wwy[h:2 * h, :]) + _dot(first[:, 3 * h:4 * h], wwy[3 * h:4 * h, :])
    eye = lax.broadcasted_iota(jnp.int32, (LANES, LANES), 0) == lax.broadcasted_iota(jnp.int32, (LANES, LANES), 1)
    p0 = pf[:, 0:LANES] + pb[:, L * LANES:(L + 1) * LANES] + jnp.where(eye, d_ref[...], 0.0)
    for s in range(L):
        for t in range(L):
            if t > s:
                blk = pf[:, (t - s) * LANES:(t - s + 1) * LANES]
            elif t < s:
                blk = pb[:, (L - (s - t)) * LANES:(L - (s - t) + 1) * LANES]
            else:
                blk = p0
            wm[s * LANES:(s + 1) * LANES, t * LANES:(t + 1) * LANES] = blk.astype(wm.dtype)


def _s5_kernel(u_ref, xws_ref, zwy_ref, d_ref, a_ref, h0_ref, y_ref, hfin_ref,
               wm, wws, wwy, s_scr, hb_scr, y_scr, *, paths):
    nu = S5_CHUNK
    half = S5_SCOLS // 4
    nsl = half // LANES
    cw = 2 * LANES

    _s5_expand(xws_ref, zwy_ref, d_ref, wm, wws, wwy)
    a = a_ref[...]

    for row0, nseq, nchunk, has_h0 in paths:
        r = nseq * nchunk

        for q in range(4):
            sq = _dot(u_ref[row0:row0 + r, :], wws[:, q * half:(q + 1) * half])
            for k in range(nsl):
                for s in range(nseq):
                    s_scr[q * nsl + k, pl.ds(s, nchunk, stride=nseq), :] = (
                        sq[s * nchunk:(s + 1) * nchunk, k * LANES:(k + 1) * LANES])

        if has_h0:
            h0 = h0_ref[...]
            carry0 = tuple(h0[:, j * LANES:(j + 1) * LANES] for j in range(4 * nsl))
        else:
            carry0 = tuple(jnp.zeros((nseq, LANES), F32) for _ in range(4 * nsl))

        def step(i, carry, nseq=nseq, nchunk=nchunk):
            f0 = pl.multiple_of(2 * i * nseq, nseq)
            f1 = pl.multiple_of((2 * i + 1) * nseq, nseq)
            b0 = pl.multiple_of((nchunk - 1 - 2 * i) * nseq, nseq)
            b1 = pl.multiple_of((nchunk - 2 - 2 * i) * nseq, nseq)
            new = list(carry)
            for k in range(nsl):
                for re, im, r0, r1, acol in ((k, 2 * nsl + k, f0, f1, k), (nsl + k, 3 * nsl + k, b0, b1, nsl + k)):
                    a_re = a[0:1, acol * LANES:(acol + 1) * LANES]
                    a_im = a[1:2, acol * LANES:(acol + 1) * LANES]
                    a2_re, a2_im = _cmul(a_re, a_im, a_re, a_im)
                    s0_re, s0_im = s_scr[re, pl.ds(r0, nseq), :], s_scr[im, pl.ds(r0, nseq), :]
                    s1_re, s1_im = s_scr[re, pl.ds(r1, nseq), :], s_scr[im, pl.ds(r1, nseq), :]
                    h_re, h_im = carry[re], carry[im]
                    s_scr[re, pl.ds(r0, nseq), :] = h_re
                    s_scr[im, pl.ds(r0, nseq), :] = h_im
                    m_re, m_im = _cmul(a_re, a_im, h_re, h_im)
                    s_scr[re, pl.ds(r1, nseq), :] = m_re + s0_re
                    s_scr[im, pl.ds(r1, nseq), :] = m_im + s0_im
                    t_re, t_im = _cmul(a_re, a_im, s0_re, s0_im)
                    n_re, n_im = _cmul(a2_re, a2_im, h_re, h_im)
                    new[re] = n_re + (t_re + s1_re)
                    new[im] = n_im + (t_im + s1_im)
            return tuple(new)

        assert nchunk % 2 == 0
        fin = lax.fori_loop(0, nchunk // 2, step, carry0)
        if not has_h0:
            for j in range(4 * nsl):
                hfin_ref[:, j * LANES:(j + 1) * LANES] = fin[j]

        for j in range(4 * nsl):
            for s in range(nseq):
                hb_scr[s * nchunk:(s + 1) * nchunk, j * LANES:(j + 1) * LANES] = (
                    s_scr[j, pl.ds(s, nchunk, stride=nseq), :].astype(BF16))

        for j in range(nu * LANES // cw):
            c0 = j * cw
            y = _dot(u_ref[row0:row0 + r, :], wm[:, c0:c0 + cw])
            for q in range(4):
                off = c0 + (LANES if q % 2 == 0 else 0)
                y = y + _dot(hb_scr[0:r, q * half:(q + 1) * half], wwy[q * half:(q + 1) * half, off:off + cw])
            g = _gelu_tanh(y)
            for t2 in range(cw // LANES):
                tau = j * (cw // LANES) + t2
                y_scr[pl.ds(row0 * nu + tau, r, stride=nu), :] = g[:, t2 * LANES:(t2 + 1) * LANES]

    y_ref[...] = y_scr[...].astype(y_ref.dtype)


def _s5_call(u_rows, ws_cmp, wy_cmp, a_step, d_rows, h0, paths):
    rows = u_rows.shape[0]
    n_tok = rows * S5_CHUNK
    sc, k = S5_SCOLS, S5_CHUNK * LANES
    rmax = max(p[1] * p[2] for p in paths)
    nseq_fin = [p[1] for p in paths if not p[3]][0]
    blk = lambda a, b: pl.BlockSpec((None, a, b), lambda g: (g, 0, 0))
    return pl.pallas_call(
        functools.partial(_s5_kernel, paths=paths),
        grid=(S5_NGB,),
        in_specs=[pl.BlockSpec((rows, k), lambda g: (0, g)),
                  pl.BlockSpec((S5_CHUNK, None, LANES, ws_cmp.shape[-1]), lambda g: (0, g, 0, 0)),
                  pl.BlockSpec((2, 2, None, sc // 4, wy_cmp.shape[-1]), lambda g: (0, 0, g, 0, 0)),
                  blk(1, LANES), blk(2, sc // 2), blk(h0.shape[1], sc)],
        out_specs=[pl.BlockSpec((n_tok, LANES), lambda g: (0, g)), blk(nseq_fin, sc)],
        out_shape=[jax.ShapeDtypeStruct((n_tok, S5_WIDTH), BF16),
                   jax.ShapeDtypeStruct((S5_NGB, nseq_fin, sc), F32)],
        scratch_shapes=[pltpu.VMEM((k, k), BF16), pltpu.VMEM((k, sc), BF16),
                        pltpu.VMEM((sc, (S5_CHUNK + 1) * LANES), BF16),
                        pltpu.VMEM((4 * (sc // 4 // LANES), rmax, LANES), F32), pltpu.VMEM((rmax, sc), BF16),
                        pltpu.VMEM((n_tok, LANES), F32)],
        compiler_params=_params("arbitrary"),
        name="s5",
    )(u_rows, ws_cmp, wy_cmp, d_rows, a_step, h0)


def _state_to_lanes(s_re, s_im):
    b = s_re.shape[0]

    def one(x):
        return jnp.transpose(x.reshape(b, 2, S5_NGB, S5_GB * S5_STATE), (2, 0, 1, 3)).reshape(S5_NGB, b, -1)

    return jnp.concatenate([one(s_re), one(s_im)], axis=-1)


def _lanes_to_state(h):
    b = h.shape[1]
    x = h.reshape(S5_NGB, b, 2, 2, S5_GB, S5_STATE)
    x = jnp.transpose(x, (2, 1, 3, 0, 4, 5)).reshape(2, b, 2, S5_NGB * S5_GB, S5_STATE)
    return x[0], x[1]


def _ret_kernel(lg_ref, q_ref, k_ref, v_ref, g_ref, gn_ref, *rest, t, chunk, group, latent, want_state):
    rest = list(rest)
    if latent:
        cos_ref, sin_ref, s0_ref = rest[:3]
        rest = rest[3:]
    o_ref = rest.pop(0)
    sfin_ref = rest.pop(0) if want_state else None
    dmat_ref = rest.pop(0)
    n = t // chunk

    h = pl.program_id(0)
    lgf = lg_ref[0, h]
    lgb = lg_ref[1, h]

    @pl.when(pl.program_id(1) == 0)
    def _():
        ti = lax.broadcasted_iota(jnp.int32, (chunk, chunk), 0)
        si = lax.broadcasted_iota(jnp.int32, (chunk, chunk), 1)
        diff = (ti - si).astype(F32)
        dmat_ref[...] = jnp.exp(jnp.where(diff >= 0, lgf * diff, -lgb * diff))

    pos = lax.broadcasted_iota(jnp.int32, (chunk, 1), 0).astype(F32)
    q_dec_f = jnp.exp(lgf * (pos + 1.0))
    q_dec_b = jnp.exp(lgb * (chunk - pos))
    k_dec_f = jnp.exp(lgf * (chunk - 1.0 - pos))
    k_dec_b = jnp.exp(lgb * pos)
    step_f = jnp.exp(lgf * chunk)
    step_b = jnp.exp(lgb * chunk)
    tn = (((0,), (0,)), ((), ()))
    if latent:
        lane = lax.broadcasted_iota(jnp.int32, (chunk, RET_DK), 1)
        first = (lane % (RET_DK // 2)) < (RET_DK // 4)

        def rope(x, rows):
            swapped = jnp.where(first, pltpu.roll(x, RET_DK - RET_DK // 4, 1), pltpu.roll(x, RET_DK // 4, 1))
            return x * cos_ref[rows, :] + swapped * sin_ref[rows, :]

    for j in range(group):
        qs, kf, kr, vs, outs = [], [], [], [], []
        for c in range(n):
            rows = slice(j * t + c * chunk, j * t + (c + 1) * chunk)
            q = q_ref[rows, :].astype(F32) * (RET_DK ** -0.5)
            k = k_ref[rows, :].astype(F32)
            if latent:
                q = rope(q, slice(c * chunk, (c + 1) * chunk))
                k = rope(k, slice(c * chunk, (c + 1) * chunk))
            qb = q.astype(BF16)
            v = v_ref[rows, :]
            scores = lax.dot_general(qb, k.astype(BF16), (((1,), (1,)), ((), ())), preferred_element_type=F32)
            outs.append(_dot((scores * dmat_ref[...]).astype(BF16), v))
            qs.append(qb)
            vs.append(v)
            kf.append((k * k_dec_f).astype(BF16))
            kr.append((k * k_dec_b).astype(BF16))

        state = s0_ref[j, 0] if latent else None
        for c in range(n):
            if state is not None:
                outs[c] = outs[c] + _dot(qs[c], state.astype(BF16)) * q_dec_f
            if c < n - 1 or want_state:
                upd = lax.dot_general(kf[c], vs[c], tn, preferred_element_type=F32)
                state = upd if state is None else step_f * state + upd
        if want_state:
            sfin_ref[j, 0] = state

        state = s0_ref[j, 1] if latent else None
        for c in reversed(range(n)):
            if state is not None:
                outs[c] = outs[c] + _dot(qs[c], state.astype(BF16)) * q_dec_b
            if c > 0 or want_state:
                upd = lax.dot_general(kr[c], vs[c], tn, preferred_element_type=F32)
                state = upd if state is None else step_b * state + upd
        if want_state:
            sfin_ref[j, 1] = state

        for c in range(n):
            rows = slice(j * t + c * chunk, j * t + (c + 1) * chunk)
            o = outs[c]
            mu = jnp.mean(o, axis=-1, keepdims=True)
            d = o - mu
            var = jnp.mean(d * d, axis=-1, keepdims=True)
            on = d * lax.rsqrt(var + EPS) * gn_ref[...]
            g = g_ref[rows, :].astype(F32)
            o_ref[rows, :] = (g * _sigmoid(g) * on).astype(o_ref.dtype)


def _ret_call(proj, log_gamma, gn_w, rope_tabs, s0, *, row0, nseq, t, chunk, group, latent, want_state):
    h, dk, dv = RET_HEADS, RET_DK, RET_DV
    gt = group * t
    rb0 = row0 // gt
    q0 = 0
    k0 = q0 + h
    v0 = (2 * h * dk) // dv
    g0 = v0 + h
    in_specs = [pl.BlockSpec(memory_space=pltpu.SMEM),
                pl.BlockSpec((gt, dk), lambda hh, b: (rb0 + b, q0 + hh)),
                pl.BlockSpec((gt, dk), lambda hh, b: (rb0 + b, k0 + hh)),
                pl.BlockSpec((gt, dv), lambda hh, b: (rb0 + b, v0 + hh)),
                pl.BlockSpec((gt, dv), lambda hh, b: (rb0 + b, g0 + hh)),
                pl.BlockSpec((1, dv), lambda hh, b: (0, hh))]
    args = [log_gamma, proj, proj, proj, proj, gn_w]
    if latent:
        in_specs += [pl.BlockSpec((t, dk), lambda hh, b: (0, 0)),
                     pl.BlockSpec((t, dk), lambda hh, b: (0, 0)),
                     pl.BlockSpec((group, 2, None, dk, dv), lambda hh, b: (b, 0, hh, 0, 0))]
        args += [rope_tabs[0], rope_tabs[1], s0]
    out_specs = [pl.BlockSpec((gt, dv), lambda hh, b: (b, hh))]
    out_shape = [jax.ShapeDtypeStruct((nseq * t, h * dv), BF16)]
    if want_state:
        out_specs.append(pl.BlockSpec((group, 2, None, dk, dv), lambda hh, b: (b, 0, hh, 0, 0)))
        out_shape.append(jax.ShapeDtypeStruct((nseq, 2, h, dk, dv), F32))
    return pl.pallas_call(
        functools.partial(_ret_kernel, t=t, chunk=chunk, group=group, latent=latent, want_state=want_state),
        grid=(h, nseq // group),
        in_specs=in_specs,
        out_specs=out_specs,
        out_shape=out_shape,
        scratch_shapes=[pltpu.VMEM((chunk, chunk), F32)],
        compiler_params=_params("arbitrary", "arbitrary"),
        name="retention_latent" if latent else "retention_ctx",
    )(*args)


def _grid_rope_tables(t):
    quarter = RET_DK // 4
    freqs = ROPE_BASE ** (-jnp.arange(quarter, dtype=F32) / quarter)
    pos = jnp.arange(t)
    row = (pos // GRID_W).astype(F32)[:, None] * freqs[None, :]
    col = (pos % GRID_W).astype(F32)[:, None] * freqs[None, :]
    cos = jnp.concatenate([jnp.cos(row), jnp.cos(row), jnp.cos(col), jnp.cos(col)], axis=-1)
    sin = jnp.concatenate([-jnp.sin(row), jnp.sin(row), -jnp.sin(col), jnp.sin(col)], axis=-1)
    return cos, sin


def _merge_kernel(yg_ref, op_ref, os_ref, ga_ref, gb_ref, wga_ref, wgb_ref, wr_ref, out_ref, wga_b, wgb_b, wr_b,
                  *, n_prompt_tiles):
    i = pl.program_id(1)

    @pl.when(i == 0)
    def _():
        wga_b[...] = wga_ref[...].astype(BF16)
        wgb_b[...] = wgb_ref[...].astype(BF16)
        wr_b[...] = wr_ref[...].astype(BF16)

    def body(o_ref):
        yg = yg_ref[...]
        za = _dot(yg, wga_b[...])
        zb = _dot(yg, wgb_b[...])
        ob = _dot(o_ref[...], wr_b[...])
        out_a = za * _sigmoid(zb)
        ga = ga_ref[...].astype(F32)
        gb = gb_ref[...].astype(F32)
        out_ref[...] = (_sigmoid(ga) * out_a + _sigmoid(gb) * ob).astype(out_ref.dtype)

    @pl.when(i < n_prompt_tiles)
    def _():
        body(op_ref)

    @pl.when(i >= n_prompt_tiles)
    def _():
        body(os_ref)


def _merge_call(yg, o_p, o_s, proj, w_glu, w_ret_out):
    m, ks = yg.shape
    kr = o_p.shape[1]
    d = w_ret_out.shape[1]
    tm = 1024 if m % 1024 == 0 else m
    tn = 512 if d % 512 == 0 else d
    ga0 = (proj.shape[1] - 2 * d) // tn
    gb0 = (proj.shape[1] - d) // tn
    nb = d // tn
    npt = o_p.shape[0] // tm
    return pl.pallas_call(
        functools.partial(_merge_kernel, n_prompt_tiles=npt),
        grid=(nb, m // tm),
        in_specs=[pl.BlockSpec((tm, ks), lambda j, i: (i, 0)),
                  pl.BlockSpec((tm, kr), lambda j, i: (jnp.minimum(i, npt - 1), 0)),
                  pl.BlockSpec((tm, kr), lambda j, i: (jnp.maximum(i - npt, 0), 0)),
                  pl.BlockSpec((tm, tn), lambda j, i: (i, ga0 + j)),
                  pl.BlockSpec((tm, tn), lambda j, i: (i, gb0 + j)),
                  pl.BlockSpec((ks, tn), lambda j, i: (0, j)),
                  pl.BlockSpec((ks, tn), lambda j, i: (0, nb + j)),
                  pl.BlockSpec((kr, tn), lambda j, i: (0, j))],
        out_specs=pl.BlockSpec((tm, tn), lambda j, i: (i, j)),
        out_shape=jax.ShapeDtypeStruct((m, d), BF16),
        scratch_shapes=[pltpu.VMEM((ks, tn), BF16), pltpu.VMEM((ks, tn), BF16), pltpu.VMEM((kr, tn), BF16)],
        compiler_params=_params("arbitrary", "arbitrary"),
        name="merge",
    )(yg, o_p, o_s, proj, proj, w_glu, w_glu, w_ret_out)


def _outproj_kernel(mg_ref, xp_ref, xs_ref, w_ref, g1_ref, sc_ref, sh_ref, n2_ref, wr_ref,
                    x1_ref, h2_ref, aff_ref, *, n_prompt_tiles, n_experts):
    i = pl.program_id(0)
    upd = g1_ref[...] * _dot(mg_ref[...], w_ref[...])

    def body(x_ref):
        x1 = x_ref[...] + upd
        x1_ref[...] = x1
        y = x1 * lax.rsqrt(jnp.mean(x1 * x1, axis=-1, keepdims=True) + EPS)
        h2 = (y * n2_ref[...]) * (1.0 + sc_ref[...]) + sh_ref[...]
        h2_ref[...] = h2.astype(h2_ref.dtype)
        wr = wr_ref[...]
        w_hi = wr.astype(BF16)
        w_lo = (wr - w_hi.astype(F32)).astype(BF16)
        h_hi = h2.astype(BF16)
        h_lo = (h2 - h_hi.astype(F32)).astype(BF16)
        hw = _dot(h_hi, jnp.concatenate([w_hi, w_lo], axis=1))
        logits = hw[:, :LANES] + (hw[:, LANES:] + _dot(h_lo, w_hi))
        lane = lax.broadcasted_iota(jnp.int32, logits.shape, 1)
        logits = jnp.where(lane < n_experts, logits, -jnp.inf)
        e = jnp.exp(logits - jnp.max(logits, axis=-1, keepdims=True))
        aff_ref[...] = (e / jnp.sum(e, axis=-1, keepdims=True)).T

    @pl.when(i < n_prompt_tiles)
    def _():
        body(xp_ref)

    @pl.when(i >= n_prompt_tiles)
    def _():
        body(xs_ref)


def _outproj_call(merged, xp, xs, w_out_b, gate1, scale2, shift2, norm2, w_router_pad, t_sample):
    m, d = merged.shape
    n_p = xp.shape[0]
    tm = 2 * ROW_TILE
    npt = n_p // tm
    tps = t_sample // tm
    row = functools.partial(_mod_row, n_prompt_tiles=npt, tiles_per_sample_seq=tps)
    mod = pl.BlockSpec((None, 1, d), lambda i: (row(i), 0, 0))
    return pl.pallas_call(
        functools.partial(_outproj_kernel, n_prompt_tiles=npt, n_experts=N_EXPERTS),
        grid=(m // tm,),
        in_specs=[pl.BlockSpec((tm, d), lambda i: (i, 0)),
                  pl.BlockSpec((tm, d), lambda i: (jnp.minimum(i, npt - 1), 0)),
                  pl.BlockSpec((tm, d), lambda i: (jnp.maximum(i - npt, 0), 0)),
                  pl.BlockSpec((d, d), lambda i: (0, 0)),
                  mod, mod, mod,
                  pl.BlockSpec((1, d), lambda i: (0, 0)),
                  pl.BlockSpec((d, LANES), lambda i: (0, 0))],
        out_specs=[pl.BlockSpec((tm, d), lambda i: (i, 0)),
                   pl.BlockSpec((tm, d), lambda i: (i, 0)),
                   pl.BlockSpec((LANES, tm), lambda i: (0, i))],
        out_shape=[jax.ShapeDtypeStruct((m, d), F32),
                   jax.ShapeDtypeStruct((m, d), BF16),
                   jax.ShapeDtypeStruct((LANES, m), F32)],
        compiler_params=_params("arbitrary"),
        name="out_proj_router",
    )(merged, xp, xs, w_out_b, gate1, scale2, shift2, norm2, w_router_pad)


def _select_kernel(a_ref, slot_ref, *, cap):
    a = a_ref[...]
    r, t = a.shape

    def as_float(bits):
        return pltpu.bitcast(bits, F32)

    def bisect(_, carry):
        lo, hi = carry
        mid = lo + jnp.right_shift(hi - lo + 1, 1)
        cnt = jnp.sum(jnp.where(a >= as_float(mid), 1.0, 0.0), axis=-1, keepdims=True)
        ok = cnt >= cap
        return jnp.where(ok, mid, lo), jnp.where(ok, hi, mid - 1)

    lo0 = jnp.zeros((r, 1), jnp.int32)
    hi0 = jnp.full((r, 1), 0x3F800000, jnp.int32)
    thr, _ = lax.fori_loop(0, 31, bisect, (lo0, hi0))

    gt = jnp.where(a >= as_float(thr + 1), 1.0, 0.0)
    eq = jnp.where(a >= as_float(thr), 1.0, 0.0) - gt
    need = cap - jnp.sum(gt, axis=-1, keepdims=True)
    before = lax.broadcasted_iota(jnp.int32, (t, t), 0) < lax.broadcasted_iota(jnp.int32, (t, t), 1)
    tri = jnp.where(before, 1.0, 0.0).astype(BF16)
    eq_rank = _dot(eq.astype(BF16), tri)
    sel = gt + eq * jnp.where(eq_rank < need, 1.0, 0.0)
    pos = _dot(sel.astype(BF16), tri)
    slot_ref[...] = jnp.where(sel > 0.5, pos, -1.0).astype(jnp.int32)


def _select_call(aff_t, cap):
    r, t = aff_t.shape
    return pl.pallas_call(
        functools.partial(_select_kernel, cap=cap),
        grid=(1,),
        in_specs=[pl.BlockSpec((r, t), lambda i: (0, 0))],
        out_specs=pl.BlockSpec((r, t), lambda i: (0, 0)),
        out_shape=jax.ShapeDtypeStruct((r, t), jnp.int32),
        compiler_params=_params("arbitrary"),
        name="select",
    )(aff_t)


def _gather_kernel(slot_ref, aff_ref, h_ref, xs_ref, gate_ref, *, cap, group):
    e_total, t = slot_ref.shape
    h = h_ref[...]
    ci = lax.broadcasted_iota(jnp.int32, (cap, t), 0)
    for e0 in range(0, e_total, group):
        hots = []
        for e in range(e0, e0 + group):
            hit = ci == slot_ref[e:e + 1, :]
            hots.append(jnp.where(hit, 1.0, 0.0).astype(BF16))
            gate_ref[e] = jnp.sum(jnp.where(hit, aff_ref[e:e + 1, :], 0.0), axis=-1, keepdims=True)
        onehot = hots[0] if group == 1 else jnp.concatenate(hots, axis=0)
        xs = _dot(onehot, h).astype(xs_ref.dtype)
        xs_ref[e0:e0 + group] = xs.reshape(group, cap, xs.shape[-1])


def _gather_call(slot_t, aff_t, h2, *, row0, nseq, t, cap):
    e = slot_t.shape[1]
    d = h2.shape[1]
    rb0 = row0 // t
    group = max(1, min(e, 512 // cap))
    return pl.pallas_call(
        functools.partial(_gather_kernel, cap=cap, group=group),
        grid=(nseq,),
        in_specs=[pl.BlockSpec((None, e, t), lambda b: (b, 0, 0)),
                  pl.BlockSpec((None, e, t), lambda b: (b, 0, 0)),
                  pl.BlockSpec((t, d), lambda b: (rb0 + b, 0))],
        out_specs=[pl.BlockSpec((e, cap, d), lambda b: (0, b, 0)),
                   pl.BlockSpec((e, cap, 1), lambda b: (0, b, 0))],
        out_shape=[jax.ShapeDtypeStruct((e, nseq * cap, d), BF16),
                   jax.ShapeDtypeStruct((e, nseq * cap, 1), F32)],
        compiler_params=_params("arbitrary"),
        name="gather",
    )(slot_t, aff_t, h2)


def _ffn_kernel(xp_ref, xs_ref, gp_ref, gs_ref, wg_ref, wu_ref, wd_ref, yp_ref, ys_ref,
                accp, accs, wgb, wub, wdb, *, chunk):
    f = pl.program_id(1)

    @pl.when(f == 0)
    def _():
        accp[...] = jnp.zeros_like(accp)
        accs[...] = jnp.zeros_like(accs)

    wgb[...] = wg_ref[...].astype(BF16)
    wub[...] = wu_ref[...].astype(BF16)
    wdb[...] = wd_ref[...].astype(BF16)

    def part(x_ref, acc):
        m = x_ref.shape[0]
        mc = min(chunk, m)
        for m0 in range(0, m, mc):
            x = x_ref[m0:m0 + mc, :]
            hg = _dot(x, wgb[...])
            hu = _dot(x, wub[...])
            hid = (hg * _sigmoid(hg) * hu).astype(BF16)
            acc[m0:m0 + mc, :] += _dot(hid, wdb[...])

    part(xp_ref, accp)
    part(xs_ref, accs)

    @pl.when(f == pl.num_programs(1) - 1)
    def _():
        yp_ref[...] = (accp[...] * gp_ref[...]).astype(yp_ref.dtype)
        ys_ref[...] = (accs[...] * gs_ref[...]).astype(ys_ref.dtype)


def _ffn_call(xs_p, xs_s, gate_p, gate_s, w_gate, w_up, w_down):
    e, mp, d = xs_p.shape
    ms = xs_s.shape[1]
    ff = w_gate.shape[2]
    tf = 256 if ff % 256 == 0 else ff
    tok = lambda m, w: pl.BlockSpec((None, m, w), lambda ee, f: (ee, 0, 0))
    return pl.pallas_call(
        functools.partial(_ffn_kernel, chunk=512),
        grid=(e, ff // tf),
        in_specs=[tok(mp, d), tok(ms, d), tok(mp, 1), tok(ms, 1),
                  pl.BlockSpec((None, d, tf), lambda ee, f: (ee, 0, f)),
                  pl.BlockSpec((None, d, tf), lambda ee, f: (ee, 0, f)),
                  pl.BlockSpec((None, tf, d), lambda ee, f: (ee, f, 0))],
        out_specs=[pl.BlockSpec((None, mp, d), lambda ee, f: (ee, 0, 0), pipeline_mode=pl.Buffered(1)),
                   pl.BlockSpec((None, ms, d), lambda ee, f: (ee, 0, 0), pipeline_mode=pl.Buffered(1))],
        out_shape=[jax.ShapeDtypeStruct((e, mp, d), BF16), jax.ShapeDtypeStruct((e, ms, d), BF16)],
        scratch_shapes=[pltpu.VMEM((mp, d), F32), pltpu.VMEM((ms, d), F32),
                        pltpu.VMEM((d, tf), BF16), pltpu.VMEM((d, tf), BF16), pltpu.VMEM((tf, d), BF16)],
        compiler_params=_params("arbitrary", "arbitrary"),
        name="expert_ffn",
    )(xs_p, xs_s, gate_p, gate_s, w_gate, w_up, w_down)


def _scatter_kernel(slot_ref, y_ref, x1_ref, g2_ref, wn_ref, o_ref, *, cap):
    e_total = y_ref.shape[0]
    tm = slot_ref.shape[0]
    slot = slot_ref[...]
    lane = lax.broadcasted_iota(jnp.int32, (tm, LANES), 1)
    per_block = max(1, LANES // cap)
    blocks = []
    for b0 in range(0, e_total, per_block):
        acc = jnp.zeros((tm, LANES), F32)
        for j in range(per_block):
            s = slot[:, b0 + j:b0 + j + 1]
            key = jnp.where(s >= 0, s + j * cap, -1)
            acc = acc + jnp.where(lane == key, 1.0, 0.0)
        blocks.append(acc.astype(BF16))
    onehot = jnp.concatenate(blocks, axis=1)
    y = y_ref[...].reshape(e_total * cap, y_ref.shape[-1])
    moe = _dot(onehot, y)
    x2 = x1_ref[...] + g2_ref[...] * moe
    o_ref[...] = x2 * lax.rsqrt(jnp.mean(x2 * x2, axis=-1, keepdims=True) + EPS) * wn_ref[...]


def _scatter_call(slot, y, x1, gate2, final_norm, *, row0, nseq, t, cap, mod_row0, mod_per_seq):
    e, _, d = y.shape
    tm = min(2 * ROW_TILE, t)
    nt = t // tm
    rb0 = row0 // tm
    assert cap == LANES or LANES % cap == 0
    return pl.pallas_call(
        functools.partial(_scatter_kernel, cap=cap),
        grid=(nseq, nt),
        in_specs=[pl.BlockSpec((None, tm, e), lambda b, i: (b, i, 0)),
                  pl.BlockSpec((e, cap, d), lambda b, i: (0, b, 0)),
                  pl.BlockSpec((tm, d), lambda b, i: (rb0 + b * nt + i, 0)),
                  pl.BlockSpec((None, 1, d), lambda b, i: (mod_row0 + b * mod_per_seq, 0, 0)),
                  pl.BlockSpec((1, d), lambda b, i: (0, 0))],
        out_specs=pl.BlockSpec((tm, d), lambda b, i: (b * nt + i, 0)),
        out_shape=jax.ShapeDtypeStruct((nseq * t, d), F32),
        compiler_params=_params("arbitrary", "arbitrary"),
        name="scatter_final",
    )(slot, y, x1, gate2, final_norm)


def kernel(x_prompt, x_sample, state_s5_re, state_s5_im, state_ret, c, c_ctx, final_norm, w_ada, b_ada, norm1, norm2, w_in, s5_a_re, s5_a_im, s5_log_dt, s5_b_re, s5_b_im, s5_c_re, s5_c_im, s5_d, w_s5_glu, ret_decay_logit, ret_gn_w, w_ret_out, w_out, w_router, w_exp_gate, w_exp_up, w_exp_down):
    bp, tp, d = x_prompt.shape
    bs, ts, _ = x_sample.shape
    depth = w_ada.shape[0]
    n_p, n_s = bp * tp, bs * ts
    xp = x_prompt.reshape(n_p, d)
    xs = x_sample.reshape(n_s, d)

    mod_rows = 16
    cvec = jnp.concatenate([c_ctx[None].astype(F32), c.astype(F32), jnp.zeros((mod_rows - 1 - bs, d), F32)], axis=0)
    rope_tabs = _grid_rope_tables(ts)
    cap_p = CAPACITY_FACTOR * tp // N_EXPERTS
    cap_s = CAPACITY_FACTOR * ts // N_EXPERTS

    new_re, new_im, new_ret = [], [], []
    for l in range(depth):
        mods = _ada_call(cvec, w_ada[l], b_ada[l])
        shift1, scale1, gate1, shift2, scale2, gate2 = [m.reshape(mod_rows, 1, d) for m in jnp.split(mods, 6, axis=-1)]

        h, u_rows = _uproj_call(xp, xs, scale1, shift1, norm1[l].reshape(1, d), w_in[l], ts)
        proj = _proj_call(h, w_in[l], S5_WIDTH)

        ws_cmp, wy_cmp, a_step, d_rows = _s5_chunk_weights(
            s5_a_re[l], s5_a_im[l], s5_log_dt[l], s5_b_re[l], s5_b_im[l], s5_c_re[l], s5_c_im[l], s5_d[l])
        h0_s = _state_to_lanes(state_s5_re[:, l].astype(F32), state_s5_im[:, l].astype(F32))
        s5_paths = ((0, bp, tp // S5_CHUNK, False), (n_p // S5_CHUNK, bs, ts // S5_CHUNK, True))
        yg, hfin_p = _s5_call(u_rows, ws_cmp, wy_cmp, a_step, d_rows, h0_s, s5_paths)
        s5_re, s5_im = _lanes_to_state(hfin_p)
        new_re.append(s5_re)
        new_im.append(s5_im)

        log_gamma = jax.nn.log_sigmoid(ret_decay_logit[l].astype(F32))
        gn_w = ret_gn_w[l].reshape(1, -1).astype(F32)
        o_p, sfin = _ret_call(proj, log_gamma, gn_w, None, None, row0=0, nseq=bp, t=tp, chunk=RET_CHUNK, group=4, latent=False,
                              want_state=True)
        o_s, = _ret_call(proj, log_gamma, gn_w, rope_tabs, state_ret[:, l].astype(F32), row0=n_p, nseq=bs, t=ts,
                         chunk=RET_CHUNK, group=4, latent=True, want_state=False)
        new_ret.append(sfin)
        merged = _merge_call(yg, o_p, o_s, proj, w_s5_glu[l], w_ret_out[l])
        w_router_pad = jnp.pad(w_router[l].astype(F32), ((0, 0), (0, LANES - N_EXPERTS)))
        x1, h2, aff = _outproj_call(merged, xp, xs, w_out[l].astype(BF16), gate1, scale2, shift2,
                                    norm2[l].reshape(1, d), w_router_pad, ts)

        aff = aff[:N_EXPERTS]
        aff_p = jnp.transpose(aff[:, :n_p].reshape(N_EXPERTS, bp, tp), (1, 0, 2))
        aff_s = jnp.transpose(aff[:, n_p:].reshape(N_EXPERTS, bs, ts), (1, 0, 2))
        slot_p = _select_call(aff_p.reshape(bp * N_EXPERTS, tp), cap_p).reshape(bp, N_EXPERTS, tp)
        slot_s = _select_call(aff_s.reshape(bs * N_EXPERTS, ts), cap_s).reshape(bs, N_EXPERTS, ts)
        xe_p, ge_p = _gather_call(slot_p, aff_p, h2, row0=0, nseq=bp, t=tp, cap=cap_p)
        xe_s, ge_s = _gather_call(slot_s, aff_s, h2, row0=n_p, nseq=bs, t=ts, cap=cap_s)
        ye_p, ye_s = _ffn_call(xe_p, xe_s, ge_p, ge_s, w_exp_gate[l], w_exp_up[l], w_exp_down[l])

        last = l == depth - 1
        wn = final_norm.reshape(1, d).astype(F32) if last else None
        assert last, "only the final layer applies the output norm in the scatter kernel"
        yp = _scatter_call(jnp.transpose(slot_p, (0, 2, 1)), ye_p, x1, gate2, wn, row0=0, nseq=bp, t=tp, cap=cap_p,
                           mod_row0=0, mod_per_seq=0)
        ysm = _scatter_call(jnp.transpose(slot_s, (0, 2, 1)), ye_s, x1, gate2, wn, row0=n_p, nseq=bs, t=ts, cap=cap_s,
                            mod_row0=1, mod_per_seq=1)

    y_prompt = yp.reshape(bp, tp, d)
    y_sample = ysm.reshape(bs, ts, d)
    return (y_prompt, y_sample, jnp.stack(new_re, axis=1), jnp.stack(new_im, axis=1), jnp.stack(new_ret, axis=1))
```

```python
import functools
import math

import jax
import jax.numpy as jnp
from jax import lax
from jax.experimental import pallas as pl
from jax.experimental.pallas import tpu as pltpu

F32 = jnp.float32
BF16 = jnp.bfloat16

EPS = 1e-6
GRID_W = 64
S5_WIDTH = 1024
S5_GROUP = 16
S5_STATE = 64
RET_HEADS = 8
RET_DK = 128
RET_DV = 256
ROPE_BASE = 10000.0
RET_CHUNK = 256
N_EXPERTS = 16
CAPACITY_FACTOR = 2

LANES = 128
S5_CHUNK = 8
S5_GB = LANES // S5_GROUP
S5_NGB = S5_WIDTH // LANES
S5_SCOLS = 4 * S5_GB * S5_STATE
ROW_TILE = 256
VMEM_LIMIT = 56 * 1024 * 1024


def _params(*sem):
    return pltpu.CompilerParams(dimension_semantics=sem, vmem_limit_bytes=VMEM_LIMIT)


def _sigmoid(x):
    return 1.0 / (1.0 + jnp.exp(-x))


def _gelu_tanh(x):
    return 0.5 * x * (1.0 + jnp.tanh(math.sqrt(2.0 / math.pi) * (x + 0.044715 * (x * x * x))))


def _dot(a, b):
    return jnp.dot(a, b, preferred_element_type=F32)


def _ada_kernel(c_ref, w_ref, b_ref, o_ref):
    c = c_ref[...]
    s = c * _sigmoid(c)
    w = w_ref[...]
    s_hi = s.astype(BF16)
    s_lo = (s - s_hi.astype(F32)).astype(BF16)
    w_hi = w.astype(BF16)
    w_lo = (w - w_hi.astype(F32)).astype(BF16)
    o_ref[...] = _dot(s_hi, w_hi) + (_dot(s_hi, w_lo) + _dot(s_lo, w_hi)) + b_ref[...]


def _ada_call(cvec, w_ada, b_ada):
    r, d = cvec.shape
    n = w_ada.shape[1]
    tn = 1024 if n % 1024 == 0 else 512
    return pl.pallas_call(
        _ada_kernel,
        grid=(n // tn,),
        in_specs=[pl.BlockSpec((r, d), lambda j: (0, 0)),
                  pl.BlockSpec((d, tn), lambda j: (0, j)),
                  pl.BlockSpec((1, tn), lambda j: (0, j))],
        out_specs=pl.BlockSpec((r, tn), lambda j: (0, j)),
        out_shape=jax.ShapeDtypeStruct((r, n), F32),
        compiler_params=_params("arbitrary"),
        name="ada",
    )(cvec, w_ada, b_ada.reshape(1, n))


def _mod_row(i, n_prompt_tiles, tiles_per_sample_seq):
    return jnp.where(i < n_prompt_tiles, 0, 1 + (i - n_prompt_tiles) // tiles_per_sample_seq)


def _proj_kernel(a_ref, w_ref, o_ref, wb_ref):
    @pl.when(pl.program_id(1) == 0)
    def _():
        wb_ref[...] = w_ref[...].astype(BF16)

    o_ref[...] = _dot(a_ref[...], wb_ref[...]).astype(o_ref.dtype)


def _proj_call(a, w, col0):
    m, k = a.shape
    n = w.shape[1] - col0
    tm = next((c for c in (1536, 1024) if m % c == 0), m)
    tn = 1024 if n % 1024 == 0 else 512
    cb0 = col0 // tn
    return pl.pallas_call(
        _proj_kernel,
        grid=(n // tn, m // tm),
        in_specs=[pl.BlockSpec((tm, k), lambda j, i: (i, 0)),
                  pl.BlockSpec((k, tn), lambda j, i: (0, cb0 + j))],
        out_specs=pl.BlockSpec((tm, tn), lambda j, i: (i, j)),
        out_shape=jax.ShapeDtypeStruct((m, n), BF16),
        scratch_shapes=[pltpu.VMEM((k, tn), BF16)],
        compiler_params=_params("arbitrary", "arbitrary"),
        name="in_proj",
    )(a, w)


def _cmul(ar, ai, br, bi):
    return ar * br - ai * bi, ar * bi + ai * br


def _uproj_kernel(xp_ref, xs_ref, sc_ref, sh_ref, nw_ref, w_ref, h_ref, o_ref, wb_ref, r_scr, *, n_prompt_tiles):
    i = pl.program_id(0)

    @pl.when(i == 0)
    def _():
        wb_ref[...] = w_ref[...].astype(BF16)

    def body(x_ref):
        x = x_ref[...]
        y = x * lax.rsqrt(jnp.mean(x * x, axis=-1, keepdims=True) + EPS)
        hb = ((y * nw_ref[...]) * (1.0 + sc_ref[...]) + sh_ref[...]).astype(BF16)
        h_ref[...] = hb
        res = _dot(hb, wb_ref[...])
        for g in range(S5_NGB):
            r_scr[g] = res[:, g * LANES:(g + 1) * LANES]

    @pl.when(i < n_prompt_tiles)
    def _():
        body(xp_ref)

    @pl.when(i >= n_prompt_tiles)
    def _():
        body(xs_ref)

    rows = h_ref.shape[0] // S5_CHUNK
    for g in range(S5_NGB):
        for tau in range(S5_CHUNK):
            c0 = (g * S5_CHUNK + tau) * LANES
            o_ref[:, c0:c0 + LANES] = r_scr[g, pl.ds(tau, rows, stride=S5_CHUNK), :].astype(o_ref.dtype)


def _uproj_call(xp, xs, scale, shift, nw, w, t_sample):
    n_p, d = xp.shape
    m = n_p + xs.shape[0]
    tm = 2 * ROW_TILE
    npt = n_p // tm
    row = functools.partial(_mod_row, n_prompt_tiles=npt, tiles_per_sample_seq=t_sample // tm)
    mod = pl.BlockSpec((None, 1, d), lambda i: (row(i), 0, 0))
    return pl.pallas_call(
        functools.partial(_uproj_kernel, n_prompt_tiles=npt),
        grid=(m // tm,),
        in_specs=[pl.BlockSpec((tm, d), lambda i: (jnp.minimum(i, npt - 1), 0)),
                  pl.BlockSpec((tm, d), lambda i: (jnp.maximum(i - npt, 0), 0)),
                  mod, mod,
                  pl.BlockSpec((1, d), lambda i: (0, 0)),
                  pl.BlockSpec((d, S5_WIDTH), lambda i: (0, 0))],
        out_specs=[pl.BlockSpec((tm, d), lambda i: (i, 0)),
                   pl.BlockSpec((tm // S5_CHUNK, S5_CHUNK * S5_WIDTH), lambda i: (i, 0))],
        out_shape=[jax.ShapeDtypeStruct((m, d), BF16),
                   jax.ShapeDtypeStruct((m // S5_CHUNK, S5_CHUNK * S5_WIDTH), BF16)],
        scratch_shapes=[pltpu.VMEM((d, S5_WIDTH), BF16), pltpu.VMEM((S5_NGB, tm, LANES), F32)],
        compiler_params=_params("arbitrary"),
        name="norm_u_proj",
    )(xp, xs, scale, shift, nw, w)


def _s5_chunk_weights(a_re, a_im, log_dt, b_re, b_im, c_re, c_im, d_skip):
    L, G, P, HG, GB, NGB = S5_CHUNK, S5_WIDTH // S5_GROUP, S5_STATE, S5_GROUP, S5_GB, S5_NGB
    a_re, a_im = a_re.astype(F32), a_im.astype(F32)
    dt = jnp.exp(log_dt.astype(F32))[..., None]
    adt_re, adt_im = a_re * dt, a_im * dt
    mag = jnp.exp(adt_re)
    lam_re, lam_im = mag * jnp.cos(adt_im), mag * jnp.sin(adt_im)
    den = a_re * a_re + a_im * a_im
    q_re = ((lam_re - 1.0) * a_re + lam_im * a_im) / den
    q_im = (lam_im * a_re - (lam_re - 1.0) * a_im) / den
    bb_re, bb_im = _cmul(q_re[..., None], q_im[..., None], b_re.astype(F32), b_im.astype(F32))
    n = jnp.arange(L + 1, dtype=F32)
    pmag = jnp.exp(adt_re[..., None] * n)
    pw_re, pw_im = pmag * jnp.cos(adt_im[..., None] * n), pmag * jnp.sin(adt_im[..., None] * n)

    def ws_exponents(pw, d):
        e = pw[0][..., :L][..., ::-1] if d == 0 else pw[1][..., :L]
        return jnp.transpose(e, (2, 0, 1))[:, :, None]

    def ws_dir(d):
        bt_re, bt_im = jnp.transpose(bb_re[d], (0, 2, 1)), jnp.transpose(bb_im[d], (0, 2, 1))
        return _cmul(ws_exponents(pw_re, d), ws_exponents(pw_im, d), bt_re[None], bt_im[None])

    (wf_re, wf_im), (wb_re, wb_im) = ws_dir(0), ws_dir(1)
    ws = jnp.concatenate([wf_re, wb_re, wf_im, wb_im], axis=-1).reshape(L, NGB, GB * HG, 4 * P)

    def wy_exponents(pw):
        return jnp.stack([pw[0], pw[1][..., ::-1]])

    ct_re = jnp.transpose(c_re.astype(F32), (0, 1, 3, 2))[:, :, :, None]
    ct_im = jnp.transpose(c_im.astype(F32), (0, 1, 3, 2))[:, :, :, None]
    y_re, y_im = _cmul(wy_exponents(pw_re)[..., None], wy_exponents(pw_im)[..., None], ct_re, ct_im)
    wy = jnp.stack([y_re, -y_im]).reshape(2, 2, NGB, GB * P, (L + 1) * HG)

    def lanes(x):
        return jnp.transpose(x.reshape(2, NGB, GB * P), (1, 0, 2)).reshape(NGB, 1, 2 * GB * P)

    a_step = jnp.concatenate([lanes(pw_re[..., L]), lanes(pw_im[..., L])], axis=1)
    return ws.astype(BF16), wy.astype(BF16), a_step, d_skip.astype(F32).reshape(NGB, 1, LANES)


def _s5_expand(xws_ref, zwy_ref, d_ref, wm, wws, wwy):
    L, hg, p = S5_CHUNK, S5_GROUP, S5_STATE
    lg_hg, lg_p, lg_lanes = hg.bit_length() - 1, p.bit_length() - 1, LANES.bit_length() - 1
    gmask = S5_GB - 1
    rblk = 512

    def expand(x, src_of_col, row_group, col_group, o_ref, cblk):
        rows, k = x.shape
        for c0 in range(0, o_ref.shape[1], cblk):
            kk = lax.broadcasted_iota(jnp.int32, (k, cblk), 0)
            cc = lax.broadcasted_iota(jnp.int32, (k, cblk), 1) + c0
            spread = jnp.where(kk == src_of_col(cc), 1.0, 0.0).astype(BF16)
            rr = lax.broadcasted_iota(jnp.int32, (rblk, cblk), 0)
            c2 = lax.broadcasted_iota(jnp.int32, (rblk, cblk), 1) + c0
            keep = row_group(rr) == col_group(c2)
            for r0 in range(0, rows, rblk):
                full = _dot(x[r0:r0 + rblk, :], spread)
                o_ref[r0:r0 + rblk, c0:c0 + cblk] = jnp.where(keep, full, 0.0).astype(o_ref.dtype)

    lane_group = lambda i: jnp.right_shift(i, lg_hg) & gmask
    state_group = lambda i: jnp.right_shift(i, lg_p) & gmask
    chunk_src = lambda c: jnp.right_shift(c, lg_lanes) * hg + (c & (hg - 1))
    state_src = lambda c: jnp.right_shift(c, lg_p + 3) * p + (c & (p - 1))
    xws = xws_ref[...]
    expand(xws.reshape(L * LANES, xws.shape[-1]), state_src, lane_group, state_group, wws, 1024)
    zwy = zwy_ref[...]
    expand(zwy.reshape(S5_SCOLS, zwy.shape[-1]), chunk_src, state_group, lane_group, wwy, wwy.shape[1])

    h = S5_SCOLS // 4
    last, first = wws[(L - 1) * LANES:L * LANES, :], wws[0:LANES, :]
    pf = _dot(last[:, 0:h], wwy[0:h, :]) + _dot(last[:, 2 * h:3 * h], wwy[2 * h:3 * h, :])
    pb = _dot(first[:, h:2 * h], wwy[h:2 * h, :]) + _dot(first[:, 3 * h:4 * h], wwy[3 * h:4 * h, :])
    eye = lax.broadcasted_iota(jnp.int32, (LANES, LANES), 0) == lax.broadcasted_iota(jnp.int32, (LANES, LANES), 1)
    p0 = pf[:, 0:LANES] + pb[:, L * LANES:(L + 1) * LANES] + jnp.where(eye, d_ref[...], 0.0)
    for s in range(L):
        for t in range(L):
            if t > s:
                blk = pf[:, (t - s) * LANES:(t - s + 1) * LANES]
            elif t < s:
                blk = pb[:, (L - (s - t)) * LANES:(L - (s - t) + 1) * LANES]
            else:
                blk = p0
            wm[s * LANES:(s + 1) * LANES, t * LANES:(t + 1) * LANES] = blk.astype(wm.dtype)


def _s5_kernel(u_ref, xws_ref, zwy_ref, d_ref, a_ref, h0_ref, y_ref, hfin_ref,
               wm, wws, wwy, s_scr, hb_scr, y_scr, *, paths):
    nu = S5_CHUNK
    half = S5_SCOLS // 4
    nsl = half // LANES
    cw = 2 * LANES

    _s5_expand(xws_ref, zwy_ref, d_ref, wm, wws, wwy)
    a = a_ref[...]

    for row0, nseq, nchunk, has_h0 in paths:
        r = nseq * nchunk

        for q in range(4):
            sq = _dot(u_ref[row0:row0 + r, :], wws[:, q * half:(q + 1) * half])
            for k in range(nsl):
                for s in range(nseq):
                    s_scr[q * nsl + k, pl.ds(s, nchunk, stride=nseq), :] = (
                        sq[s * nchunk:(s + 1) * nchunk, k * LANES:(k + 1) * LANES])

        if has_h0:
            h0 = h0_ref[...]
            carry0 = tuple(h0[:, j * LANES:(j + 1) * LANES] for j in range(4 * nsl))
        else:
            carry0 = tuple(jnp.zeros((nseq, LANES), F32) for _ in range(4 * nsl))

        def step(i, carry, nseq=nseq, nchunk=nchunk):
            f0 = pl.multiple_of(2 * i * nseq, nseq)
            f1 = pl.multiple_of((2 * i + 1) * nseq, nseq)
            b0 = pl.multiple_of((nchunk - 1 - 2 * i) * nseq, nseq)
            b1 = pl.multiple_of((nchunk - 2 - 2 * i) * nseq, nseq)
            new = list(carry)
            for k in range(nsl):
                for re, im, r0, r1, acol in ((k, 2 * nsl + k, f0, f1, k), (nsl + k, 3 * nsl + k, b0, b1, nsl + k)):
                    a_re = a[0:1, acol * LANES:(acol + 1) * LANES]
                    a_im = a[1:2, acol * LANES:(acol + 1) * LANES]
                    a2_re, a2_im = _cmul(a_re, a_im, a_re, a_im)
                    s0_re, s0_im = s_scr[re, pl.ds(r0, nseq), :], s_scr[im, pl.ds(r0, nseq), :]
                    s1_re, s1_im = s_scr[re, pl.ds(r1, nseq), :], s_scr[im, pl.ds(r1, nseq), :]
                    h_re, h_im = carry[re], carry[im]
                    s_scr[re, pl.ds(r0, nseq), :] = h_re
                    s_scr[im, pl.ds(r0, nseq), :] = h_im
                    m_re, m_im = _cmul(a_re, a_im, h_re, h_im)
                    s_scr[re, pl.ds(r1, nseq), :] = m_re + s0_re
                    s_scr[im, pl.ds(r1, nseq), :] = m_im + s0_im
                    t_re, t_im = _cmul(a_re, a_im, s0_re, s0_im)
                    n_re, n_im = _cmul(a2_re, a2_im, h_re, h_im)
                    new[re] = n_re + (t_re + s1_re)
                    new[im] = n_im + (t_im + s1_im)
            return tuple(new)

        assert nchunk % 2 == 0
        fin = lax.fori_loop(0, nchunk // 2, step, carry0)
        if not has_h0:
            for j in range(4 * nsl):
                hfin_ref[:, j * LANES:(j + 1) * LANES] = fin[j]

        for j in range(4 * nsl):
            for s in range(nseq):
                hb_scr[s * nchunk:(s + 1) * nchunk, j * LANES:(j + 1) * LANES] = (
                    s_scr[j, pl.ds(s, nchunk, stride=nseq), :].astype(BF16))

        for j in range(nu * LANES // cw):
            c0 = j * cw
            y = _dot(u_ref[row0:row0 + r, :], wm[:, c0:c0 + cw])
            for q in range(4):
                off = c0 + (LANES if q % 2 == 0 else 0)
                y = y + _dot(hb_scr[0:r, q * half:(q + 1) * half], wwy[q * half:(q + 1) * half, off:off + cw])
            g = _gelu_tanh(y)
            for t2 in range(cw // LANES):
                tau = j * (cw // LANES) + t2
                y_scr[pl.ds(row0 * nu + tau, r, stride=nu), :] = g[:, t2 * LANES:(t2 + 1) * LANES]

    y_ref[...] = y_scr[...].astype(y_ref.dtype)


def _s5_call(u_rows, ws_cmp, wy_cmp, a_step, d_rows, h0, paths):
    rows = u_rows.shape[0]
    n_tok = rows * S5_CHUNK
    sc, k = S5_SCOLS, S5_CHUNK * LANES
    rmax = max(p[1] * p[2] for p in paths)
    nseq_fin = [p[1] for p in paths if not p[3]][0]
    blk = lambda a, b: pl.BlockSpec((None, a, b), lambda g: (g, 0, 0))
    return pl.pallas_call(
        functools.partial(_s5_kernel, paths=paths),
        grid=(S5_NGB,),
        in_specs=[pl.BlockSpec((rows, k), lambda g: (0, g)),
                  pl.BlockSpec((S5_CHUNK, None, LANES, ws_cmp.shape[-1]), lambda g: (0, g, 0, 0)),
                  pl.BlockSpec((2, 2, None, sc // 4, wy_cmp.shape[-1]), lambda g: (0, 0, g, 0, 0)),
                  blk(1, LANES), blk(2, sc // 2), blk(h0.shape[1], sc)],
        out_specs=[pl.BlockSpec((n_tok, LANES), lambda g: (0, g)), blk(nseq_fin, sc)],
        out_shape=[jax.ShapeDtypeStruct((n_tok, S5_WIDTH), BF16),
                   jax.ShapeDtypeStruct((S5_NGB, nseq_fin, sc), F32)],
        scratch_shapes=[pltpu.VMEM((k, k), BF16), pltpu.VMEM((k, sc), BF16),
                        pltpu.VMEM((sc, (S5_CHUNK + 1) * LANES), BF16),
                        pltpu.VMEM((4 * (sc // 4 // LANES), rmax, LANES), F32), pltpu.VMEM((rmax, sc), BF16),
                        pltpu.VMEM((n_tok, LANES), F32)],
        compiler_params=_params("arbitrary"),
        name="s5",
    )(u_rows, ws_cmp, wy_cmp, d_rows, a_step, h0)


def _state_to_lanes(s_re, s_im):
    b = s_re.shape[0]

    def one(x):
        return jnp.transpose(x.reshape(b, 2, S5_NGB, S5_GB * S5_STATE), (2, 0, 1, 3)).reshape(S5_NGB, b, -1)

    return jnp.concatenate([one(s_re), one(s_im)], axis=-1)


def _lanes_to_state(h):
    b = h.shape[1]
    x = h.reshape(S5_NGB, b, 2, 2, S5_GB, S5_STATE)
    x = jnp.transpose(x, (2, 1, 3, 0, 4, 5)).reshape(2, b, 2, S5_NGB * S5_GB, S5_STATE)
    return x[0], x[1]


def _ret_kernel(lg_ref, q_ref, k_ref, v_ref, g_ref, gn_ref, *rest, t, chunk, group, latent, want_state):
    rest = list(rest)
    if latent:
        cos_ref, sin_ref, s0_ref = rest[:3]
        rest = rest[3:]
    o_ref = rest.pop(0)
    sfin_ref = rest.pop(0) if want_state else None
    dmat_ref = rest.pop(0)
    n = t // chunk

    h = pl.program_id(0)
    lgf = lg_ref[0, h]
    lgb = lg_ref[1, h]

    @pl.when(pl.program_id(1) == 0)
    def _():
        ti = lax.broadcasted_iota(jnp.int32, (chunk, chunk), 0)
        si = lax.broadcasted_iota(jnp.int32, (chunk, chunk), 1)
        diff = (ti - si).astype(F32)
        dmat_ref[...] = jnp.exp(jnp.where(diff >= 0, lgf * diff, -lgb * diff))

    pos = lax.broadcasted_iota(jnp.int32, (chunk, 1), 0).astype(F32)
    q_dec_f = jnp.exp(lgf * (pos + 1.0))
    q_dec_b = jnp.exp(lgb * (chunk - pos))
    k_dec_f = jnp.exp(lgf * (chunk - 1.0 - pos))
    k_dec_b = jnp.exp(lgb * pos)
    step_f = jnp.exp(lgf * chunk)
    step_b = jnp.exp(lgb * chunk)
    tn = (((0,), (0,)), ((), ()))
    if latent:
        lane = lax.broadcasted_iota(jnp.int32, (chunk, RET_DK), 1)
        first = (lane % (RET_DK // 2)) < (RET_DK // 4)

        def rope(x, rows):
            swapped = jnp.where(first, pltpu.roll(x, RET_DK - RET_DK // 4, 1), pltpu.roll(x, RET_DK // 4, 1))
            return x * cos_ref[rows, :] + swapped * sin_ref[rows, :]

    for j in range(group):
        qs, kf, kr, vs, outs = [], [], [], [], []
        for c in range(n):
            rows = slice(j * t + c * chunk, j * t + (c + 1) * chunk)
            q = q_ref[rows, :].astype(F32) * (RET_DK ** -0.5)
            k = k_ref[rows, :].astype(F32)
            if latent:
                q = rope(q, slice(c * chunk, (c + 1) * chunk))
                k = rope(k, slice(c * chunk, (c + 1) * chunk))
            qb = q.astype(BF16)
            v = v_ref[rows, :]
            scores = lax.dot_general(qb, k.astype(BF16), (((1,), (1,)), ((), ())), preferred_element_type=F32)
            outs.append(_dot((scores * dmat_ref[...]).astype(BF16), v))
            qs.append(qb)
            vs.append(v)
            kf.append((k * k_dec_f).astype(BF16))
            kr.append((k * k_dec_b).astype(BF16))

        state = s0_ref[j, 0] if latent else None
        for c in range(n):
            if state is not None:
                outs[c] = outs[c] + _dot(qs[c], state.astype(BF16)) * q_dec_f
            if c < n - 1 or want_state:
                upd = lax.dot_general(kf[c], vs[c], tn, preferred_element_type=F32)
                state = upd if state is None else step_f * state + upd
        if want_state:
            sfin_ref[j, 0] = state

        state = s0_ref[j, 1] if latent else None
        for c in reversed(range(n)):
            if state is not None:
                outs[c] = outs[c] + _dot(qs[c], state.astype(BF16)) * q_dec_b
            if c > 0 or want_state:
                upd = lax.dot_general(kr[c], vs[c], tn, preferred_element_type=F32)
                state = upd if state is None else step_b * state + upd
        if want_state:
            sfin_ref[j, 1] = state

        for c in range(n):
            rows = slice(j * t + c * chunk, j * t + (c + 1) * chunk)
            o = outs[c]
            mu = jnp.mean(o, axis=-1, keepdims=True)
            d = o - mu
            var = jnp.mean(d * d, axis=-1, keepdims=True)
            on = d * lax.rsqrt(var + EPS) * gn_ref[...]
            g = g_ref[rows, :].astype(F32)
            o_ref[rows, :] = (g * _sigmoid(g) * on).astype(o_ref.dtype)


def _ret_call(proj, log_gamma, gn_w, rope_tabs, s0, *, row0, nseq, t, chunk, group, latent, want_state):
    h, dk, dv = RET_HEADS, RET_DK, RET_DV
    gt = group * t
    rb0 = row0 // gt
    q0 = 0
    k0 = q0 + h
    v0 = (2 * h * dk) // dv
    g0 = v0 + h
    in_specs = [pl.BlockSpec(memory_space=pltpu.SMEM),
                pl.BlockSpec((gt, dk), lambda hh, b: (rb0 + b, q0 + hh)),
                pl.BlockSpec((gt, dk), lambda hh, b: (rb0 + b, k0 + hh)),
                pl.BlockSpec((gt, dv), lambda hh, b: (rb0 + b, v0 + hh)),
                pl.BlockSpec((gt, dv), lambda hh, b: (rb0 + b, g0 + hh)),
                pl.BlockSpec((1, dv), lambda hh, b: (0, hh))]
    args = [log_gamma, proj, proj, proj, proj, gn_w]
    if latent:
        in_specs += [pl.BlockSpec((t, dk), lambda hh, b: (0, 0)),
                     pl.BlockSpec((t, dk), lambda hh, b: (0, 0)),
                     pl.BlockSpec((group, 2, None, dk, dv), lambda hh, b: (b, 0, hh, 0, 0))]
        args += [rope_tabs[0], rope_tabs[1], s0]
    out_specs = [pl.BlockSpec((gt, dv), lambda hh, b: (b, hh))]
    out_shape = [jax.ShapeDtypeStruct((nseq * t, h * dv), BF16)]
    if want_state:
        out_specs.append(pl.BlockSpec((group, 2, None, dk, dv), lambda hh, b: (b, 0, hh, 0, 0)))
        out_shape.append(jax.ShapeDtypeStruct((nseq, 2, h, dk, dv), F32))
    return pl.pallas_call(
        functools.partial(_ret_kernel, t=t, chunk=chunk, group=group, latent=latent, want_state=want_state),
        grid=(h, nseq // group),
        in_specs=in_specs,
        out_specs=out_specs,
        out_shape=out_shape,
        scratch_shapes=[pltpu.VMEM((chunk, chunk), F32)],
        compiler_params=_params("arbitrary", "arbitrary"),
        name="retention_latent" if latent else "retention_ctx",
    )(*args)


def _grid_rope_tables(t):
    quarter = RET_DK // 4
    freqs = ROPE_BASE ** (-jnp.arange(quarter, dtype=F32) / quarter)
    pos = jnp.arange(t)
    row = (pos // GRID_W).astype(F32)[:, None] * freqs[None, :]
    col = (pos % GRID_W).astype(F32)[:, None] * freqs[None, :]
    cos = jnp.concatenate([jnp.cos(row), jnp.cos(row), jnp.cos(col), jnp.cos(col)], axis=-1)
    sin = jnp.concatenate([-jnp.sin(row), jnp.sin(row), -jnp.sin(col), jnp.sin(col)], axis=-1)
    return cos, sin


def _merge_kernel(yg_ref, op_ref, os_ref, ga_ref, gb_ref, wga_ref, wgb_ref, wr_ref, out_ref, wga_b, wgb_b, wr_b,
                  *, n_prompt_tiles):
    i = pl.program_id(1)

    @pl.when(i == 0)
    def _():
        wga_b[...] = wga_ref[...].astype(BF16)
        wgb_b[...] = wgb_ref[...].astype(BF16)
        wr_b[...] = wr_ref[...].astype(BF16)

    def body(o_ref):
        yg = yg_ref[...]
        za = _dot(yg, wga_b[...])
        zb = _dot(yg, wgb_b[...])
        ob = _dot(o_ref[...], wr_b[...])
        out_a = za * _sigmoid(zb)
        ga = ga_ref[...].astype(F32)
        gb = gb_ref[...].astype(F32)
        out_ref[...] = (_sigmoid(ga) * out_a + _sigmoid(gb) * ob).astype(out_ref.dtype)

    @pl.when(i < n_prompt_tiles)
    def _():
        body(op_ref)

    @pl.when(i >= n_prompt_tiles)
    def _():
        body(os_ref)


def _merge_call(yg, o_p, o_s, proj, w_glu, w_ret_out):
    m, ks = yg.shape
    kr = o_p.shape[1]
    d = w_ret_out.shape[1]
    tm = 1024 if m % 1024 == 0 else m
    tn = 512 if d % 512 == 0 else d
    ga0 = (proj.shape[1] - 2 * d) // tn
    gb0 = (proj.shape[1] - d) // tn
    nb = d // tn
    npt = o_p.shape[0] // tm
    return pl.pallas_call(
        functools.partial(_merge_kernel, n_prompt_tiles=npt),
        grid=(nb, m // tm),
        in_specs=[pl.BlockSpec((tm, ks), lambda j, i: (i, 0)),
                  pl.BlockSpec((tm, kr), lambda j, i: (jnp.minimum(i, npt - 1), 0)),
                  pl.BlockSpec((tm, kr), lambda j, i: (jnp.maximum(i - npt, 0), 0)),
                  pl.BlockSpec((tm, tn), lambda j, i: (i, ga0 + j)),
                  pl.BlockSpec((tm, tn), lambda j, i: (i, gb0 + j)),
                  pl.BlockSpec((ks, tn), lambda j, i: (0, j)),
                  pl.BlockSpec((ks, tn), lambda j, i: (0, nb + j)),
                  pl.BlockSpec((kr, tn), lambda j, i: (0, j))],
        out_specs=pl.BlockSpec((tm, tn), lambda j, i: (i, j)),
        out_shape=jax.ShapeDtypeStruct((m, d), BF16),
        scratch_shapes=[pltpu.VMEM((ks, tn), BF16), pltpu.VMEM((ks, tn), BF16), pltpu.VMEM((kr, tn), BF16)],
        compiler_params=_params("arbitrary", "arbitrary"),
        name="merge",
    )(yg, o_p, o_s, proj, proj, w_glu, w_glu, w_ret_out)


def _outproj_kernel(mg_ref, xp_ref, xs_ref, w_ref, g1_ref, sc_ref, sh_ref, n2_ref, wr_ref,
                    x1_ref, h2_ref, aff_ref, *, n_prompt_tiles, n_experts):
    i = pl.program_id(0)
    upd = g1_ref[...] * _dot(mg_ref[...], w_ref[...])

    def body(x_ref):
        x1 = x_ref[...] + upd
        x1_ref[...] = x1
        y = x1 * lax.rsqrt(jnp.mean(x1 * x1, axis=-1, keepdims=True) + EPS)
        h2 = (y * n2_ref[...]) * (1.0 + sc_ref[...]) + sh_ref[...]
        h2_ref[...] = h2.astype(h2_ref.dtype)
        wr = wr_ref[...]
        w_hi = wr.astype(BF16)
        w_lo = (wr - w_hi.astype(F32)).astype(BF16)
        h_hi = h2.astype(BF16)
        h_lo = (h2 - h_hi.astype(F32)).astype(BF16)
        hw = _dot(h_hi, jnp.concatenate([w_hi, w_lo], axis=1))
        logits = hw[:, :LANES] + (hw[:, LANES:] + _dot(h_lo, w_hi))
        lane = lax.broadcasted_iota(jnp.int32, logits.shape, 1)
        logits = jnp.where(lane < n_experts, logits, -jnp.inf)
        e = jnp.exp(logits - jnp.max(logits, axis=-1, keepdims=True))
        aff_ref[...] = (e / jnp.sum(e, axis=-1, keepdims=True)).T

    @pl.when(i < n_prompt_tiles)
    def _():
        body(xp_ref)

    @pl.when(i >= n_prompt_tiles)
    def _():
        body(xs_ref)


def _outproj_call(merged, xp, xs, w_out_b, gate1, scale2, shift2, norm2, w_router_pad, t_sample):
    m, d = merged.shape
    n_p = xp.shape[0]
    tm = 2 * ROW_TILE
    npt = n_p // tm
    tps = t_sample // tm
    row = functools.partial(_mod_row, n_prompt_tiles=npt, tiles_per_sample_seq=tps)
    mod = pl.BlockSpec((None, 1, d), lambda i: (row(i), 0, 0))
    return pl.pallas_call(
        functools.partial(_outproj_kernel, n_prompt_tiles=npt, n_experts=N_EXPERTS),
        grid=(m // tm,),
        in_specs=[pl.BlockSpec((tm, d), lambda i: (i, 0)),
                  pl.BlockSpec((tm, d), lambda i: (jnp.minimum(i, npt - 1), 0)),
                  pl.BlockSpec((tm, d), lambda i: (jnp.maximum(i - npt, 0), 0)),
                  pl.BlockSpec((d, d), lambda i: (0, 0)),
                  mod, mod, mod,
                  pl.BlockSpec((1, d), lambda i: (0, 0)),
                  pl.BlockSpec((d, LANES), lambda i: (0, 0))],
        out_specs=[pl.BlockSpec((tm, d), lambda i: (i, 0)),
                   pl.BlockSpec((tm, d), lambda i: (i, 0)),
                   pl.BlockSpec((LANES, tm), lambda i: (0, i))],
        out_shape=[jax.ShapeDtypeStruct((m, d), F32),
                   jax.ShapeDtypeStruct((m, d), BF16),
                   jax.ShapeDtypeStruct((LANES, m), F32)],
        compiler_params=_params("arbitrary"),
        name="out_proj_router",
    )(merged, xp, xs, w_out_b, gate1, scale2, shift2, norm2, w_router_pad)


def _select_kernel(a_ref, slot_ref, *, cap):
    a = a_ref[...]
    r, t = a.shape

    def as_float(bits):
        return pltpu.bitcast(bits, F32)

    def bisect(_, carry):
        lo, hi = carry
        mid = lo + jnp.right_shift(hi - lo + 1, 1)
        cnt = jnp.sum(jnp.where(a >= as_float(mid), 1.0, 0.0), axis=-1, keepdims=True)
        ok = cnt >= cap
        return jnp.where(ok, mid, lo), jnp.where(ok, hi, mid - 1)

    lo0 = jnp.zeros((r, 1), jnp.int32)
    hi0 = jnp.full((r, 1), 0x3F800000, jnp.int32)
    thr, _ = lax.fori_loop(0, 31, bisect, (lo0, hi0))

    gt = jnp.where(a >= as_float(thr + 1), 1.0, 0.0)
    eq = jnp.where(a >= as_float(thr), 1.0, 0.0) - gt
    need = cap - jnp.sum(gt, axis=-1, keepdims=True)
    before = lax.broadcasted_iota(jnp.int32, (t, t), 0) < lax.broadcasted_iota(jnp.int32, (t, t), 1)
    tri = jnp.where(before, 1.0, 0.0).astype(BF16)
    eq_rank = _dot(eq.astype(BF16), tri)
    sel = gt + eq * jnp.where(eq_rank < need, 1.0, 0.0)
    pos = _dot(sel.astype(BF16), tri)
    slot_ref[...] = jnp.where(sel > 0.5, pos, -1.0).astype(jnp.int32)


def _select_call(aff_t, cap):
    r, t = aff_t.shape
    return pl.pallas_call(
        functools.partial(_select_kernel, cap=cap),
        grid=(1,),
        in_specs=[pl.BlockSpec((r, t), lambda i: (0, 0))],
        out_specs=pl.BlockSpec((r, t), lambda i: (0, 0)),
        out_shape=jax.ShapeDtypeStruct((r, t), jnp.int32),
        compiler_params=_params("arbitrary"),
        name="select",
    )(aff_t)


def _gather_kernel(slot_ref, aff_ref, h_ref, xs_ref, gate_ref, *, cap, group, seqs):
    _, e_total, t = slot_ref.shape
    ci = lax.broadcasted_iota(jnp.int32, (cap, t), 0)
    for s in range(seqs):
        h = h_ref[s * t:(s + 1) * t, :]
        rows = slice(s * cap, (s + 1) * cap)
        for e0 in range(0, e_total, group):
            hots = []
            for e in range(e0, e0 + group):
                hit = ci == slot_ref[s, e:e + 1, :]
                hots.append(jnp.where(hit, 1.0, 0.0).astype(BF16))
                gate_ref[e, rows, :] = jnp.sum(jnp.where(hit, aff_ref[s, e:e + 1, :], 0.0), axis=-1, keepdims=True)
            onehot = hots[0] if group == 1 else jnp.concatenate(hots, axis=0)
            xs = _dot(onehot, h).astype(xs_ref.dtype)
            xs_ref[e0:e0 + group, rows, :] = xs.reshape(group, cap, xs.shape[-1])


def _gather_call(slot_t, aff_t, h2, *, row0, nseq, t, cap, seqs):
    e = slot_t.shape[1]
    d = h2.shape[1]
    rb0 = row0 // (seqs * t)
    group = max(1, min(e, 512 // cap))
    return pl.pallas_call(
        functools.partial(_gather_kernel, cap=cap, group=group, seqs=seqs),
        grid=(nseq // seqs,),
        in_specs=[pl.BlockSpec((seqs, e, t), lambda b: (b, 0, 0)),
                  pl.BlockSpec((seqs, e, t), lambda b: (b, 0, 0)),
                  pl.BlockSpec((seqs * t, d), lambda b: (rb0 + b, 0))],
        out_specs=[pl.BlockSpec((e, seqs * cap, d), lambda b: (0, b, 0)),
                   pl.BlockSpec((e, seqs * cap, 1), lambda b: (0, b, 0))],
        out_shape=[jax.ShapeDtypeStruct((e, nseq * cap, d), BF16),
                   jax.ShapeDtypeStruct((e, nseq * cap, 1), F32)],
        compiler_params=_params("arbitrary"),
        name="gather",
    )(slot_t, aff_t, h2)


def _ffn_kernel(xp_ref, xs_ref, gp_ref, gs_ref, wg_ref, wu_ref, wd_ref, yp_ref, ys_ref,
                accp, accs, wgb, wub, wdb, *, chunk):
    f = pl.program_id(1)

    @pl.when(f == 0)
    def _():
        accp[...] = jnp.zeros_like(accp)
        accs[...] = jnp.zeros_like(accs)

    wgb[...] = wg_ref[...].astype(BF16)
    wub[...] = wu_ref[...].astype(BF16)
    wdb[...] = wd_ref[...].astype(BF16)

    def part(x_ref, acc):
        m = x_ref.shape[0]
        mc = min(chunk, m)
        for m0 in range(0, m, mc):
            x = x_ref[m0:m0 + mc, :]
            hg = _dot(x, wgb[...])
            hu = _dot(x, wub[...])
            hid = (hg * _sigmoid(hg) * hu).astype(BF16)
            acc[m0:m0 + mc, :] += _dot(hid, wdb[...])

    part(xp_ref, accp)
    part(xs_ref, accs)

    @pl.when(f == pl.num_programs(1) - 1)
    def _():
        yp_ref[...] = (accp[...] * gp_ref[...]).astype(yp_ref.dtype)
        ys_ref[...] = (accs[...] * gs_ref[...]).astype(ys_ref.dtype)


def _ffn_call(xs_p, xs_s, gate_p, gate_s, w_gate, w_up, w_down):
    e, mp, d = xs_p.shape
    ms = xs_s.shape[1]
    ff = w_gate.shape[2]
    tf = 256 if ff % 256 == 0 else ff
    tok = lambda m, w: pl.BlockSpec((None, m, w), lambda ee, f: (ee, 0, 0))
    return pl.pallas_call(
        functools.partial(_ffn_kernel, chunk=512),
        grid=(e, ff // tf),
        in_specs=[tok(mp, d), tok(ms, d), tok(mp, 1), tok(ms, 1),
                  pl.BlockSpec((None, d, tf), lambda ee, f: (ee, 0, f)),
                  pl.BlockSpec((None, d, tf), lambda ee, f: (ee, 0, f)),
                  pl.BlockSpec((None, tf, d), lambda ee, f: (ee, f, 0))],
        out_specs=[pl.BlockSpec((None, mp, d), lambda ee, f: (ee, 0, 0), pipeline_mode=pl.Buffered(1)),
                   pl.BlockSpec((None, ms, d), lambda ee, f: (ee, 0, 0), pipeline_mode=pl.Buffered(1))],
        out_shape=[jax.ShapeDtypeStruct((e, mp, d), BF16), jax.ShapeDtypeStruct((e, ms, d), BF16)],
        scratch_shapes=[pltpu.VMEM((mp, d), F32), pltpu.VMEM((ms, d), F32),
                        pltpu.VMEM((d, tf), BF16), pltpu.VMEM((d, tf), BF16), pltpu.VMEM((tf, d), BF16)],
        compiler_params=_params("arbitrary", "arbitrary"),
        name="expert_ffn",
    )(xs_p, xs_s, gate_p, gate_s, w_gate, w_up, w_down)


def _scatter_kernel(slot_ref, y_ref, x1_ref, g2_ref, wn_ref, o_ref, *, cap, seqs):
    e_total = y_ref.shape[0]
    tm = slot_ref.shape[1]
    lane = lax.broadcasted_iota(jnp.int32, (tm, LANES), 1)
    per_block = max(1, LANES // cap)
    for q in range(seqs):
        slot = slot_ref[q]
        blocks = []
        for b0 in range(0, e_total, per_block):
            acc = jnp.zeros((tm, LANES), F32)
            for j in range(per_block):
                s = slot[:, b0 + j:b0 + j + 1]
                key = jnp.where(s >= 0, s + j * cap, -1)
                acc = acc + jnp.where(lane == key, 1.0, 0.0)
            blocks.append(acc.astype(BF16))
        onehot = jnp.concatenate(blocks, axis=1)
        y = y_ref[:, q * cap:(q + 1) * cap, :].reshape(e_total * cap, y_ref.shape[-1])
        moe = _dot(onehot, y)
        rows = slice(q * tm, (q + 1) * tm)
        x2 = x1_ref[rows, :] + g2_ref[...] * moe
        o_ref[rows, :] = x2 * lax.rsqrt(jnp.mean(x2 * x2, axis=-1, keepdims=True) + EPS) * wn_ref[...]


def _scatter_call(slot, y, x1, gate2, final_norm, *, row0, nseq, t, cap, seqs, mod_row0, mod_per_seq):
    e, _, d = y.shape
    tm = min(2 * ROW_TILE, t)
    nt = t // tm
    assert cap == LANES or LANES % cap == 0
    assert seqs == 1 or (nt == 1 and mod_per_seq == 0)
    rb0 = row0 // (seqs * tm)
    return pl.pallas_call(
        functools.partial(_scatter_kernel, cap=cap, seqs=seqs),
        grid=(nseq // seqs, nt),
        in_specs=[pl.BlockSpec((seqs, tm, e), lambda b, i: (b, i, 0)),
                  pl.BlockSpec((e, seqs * cap, d), lambda b, i: (0, b, 0)),
                  pl.BlockSpec((seqs * tm, d), lambda b, i: (rb0 + b * nt + i, 0)),
                  pl.BlockSpec((None, 1, d), lambda b, i: (mod_row0 + b * mod_per_seq, 0, 0)),
                  pl.BlockSpec((1, d), lambda b, i: (0, 0))],
        out_specs=pl.BlockSpec((seqs * tm, d), lambda b, i: (b * nt + i, 0)),
        out_shape=jax.ShapeDtypeStruct((nseq * t, d), F32),
        compiler_params=_params("arbitrary", "arbitrary"),
        name="scatter_final",
    )(slot, y, x1, gate2, final_norm)


def kernel(x_prompt, x_sample, state_s5_re, state_s5_im, state_ret, c, c_ctx, final_norm, w_ada, b_ada, norm1, norm2, w_in, s5_a_re, s5_a_im, s5_log_dt, s5_b_re, s5_b_im, s5_c_re, s5_c_im, s5_d, w_s5_glu, ret_decay_logit, ret_gn_w, w_ret_out, w_out, w_router, w_exp_gate, w_exp_up, w_exp_down):
    bp, tp, d = x_prompt.shape
    bs, ts, _ = x_sample.shape
    depth = w_ada.shape[0]
    n_p, n_s = bp * tp, bs * ts
    xp = x_prompt.reshape(n_p, d)
    xs = x_sample.reshape(n_s, d)

    mod_rows = 16
    cvec = jnp.concatenate([c_ctx[None].astype(F32), c.astype(F32), jnp.zeros((mod_rows - 1 - bs, d), F32)], axis=0)
    rope_tabs = _grid_rope_tables(ts)
    cap_p = CAPACITY_FACTOR * tp // N_EXPERTS
    cap_s = CAPACITY_FACTOR * ts // N_EXPERTS

    new_re, new_im, new_ret = [], [], []
    for l in range(depth):
        mods = _ada_call(cvec, w_ada[l], b_ada[l])
        shift1, scale1, gate1, shift2, scale2, gate2 = [m.reshape(mod_rows, 1, d) for m in jnp.split(mods, 6, axis=-1)]

        h, u_rows = _uproj_call(xp, xs, scale1, shift1, norm1[l].reshape(1, d), w_in[l], ts)
        proj = _proj_call(h, w_in[l], S5_WIDTH)

        ws_cmp, wy_cmp, a_step, d_rows = _s5_chunk_weights(
            s5_a_re[l], s5_a_im[l], s5_log_dt[l], s5_b_re[l], s5_b_im[l], s5_c_re[l], s5_c_im[l], s5_d[l])
        h0_s = _state_to_lanes(state_s5_re[:, l].astype(F32), state_s5_im[:, l].astype(F32))
        s5_paths = ((0, bp, tp // S5_CHUNK, False), (n_p // S5_CHUNK, bs, ts // S5_CHUNK, True))
        yg, hfin_p = _s5_call(u_rows, ws_cmp, wy_cmp, a_step, d_rows, h0_s, s5_paths)
        s5_re, s5_im = _lanes_to_state(hfin_p)
        new_re.append(s5_re)
        new_im.append(s5_im)

        log_gamma = jax.nn.log_sigmoid(ret_decay_logit[l].astype(F32))
        gn_w = ret_gn_w[l].reshape(1, -1).astype(F32)
        o_p, sfin = _ret_call(proj, log_gamma, gn_w, None, None, row0=0, nseq=bp, t=tp, chunk=RET_CHUNK, group=8, latent=False,
                              want_state=True)
        o_s, = _ret_call(proj, log_gamma, gn_w, rope_tabs, state_ret[:, l].astype(F32), row0=n_p, nseq=bs, t=ts,
                         chunk=RET_CHUNK, group=4, latent=True, want_state=False)
        new_ret.append(sfin)
        merged = _merge_call(yg, o_p, o_s, proj, w_s5_glu[l], w_ret_out[l])
        w_router_pad = jnp.pad(w_router[l].astype(F32), ((0, 0), (0, LANES - N_EXPERTS)))
        x1, h2, aff = _outproj_call(merged, xp, xs, w_out[l].astype(BF16), gate1, scale2, shift2,
                                    norm2[l].reshape(1, d), w_router_pad, ts)

        aff = aff[:N_EXPERTS]
        aff_p = jnp.transpose(aff[:, :n_p].reshape(N_EXPERTS, bp, tp), (1, 0, 2))
        aff_s = jnp.transpose(aff[:, n_p:].reshape(N_EXPERTS, bs, ts), (1, 0, 2))
        slot_p = _select_call(aff_p.reshape(bp * N_EXPERTS, tp), cap_p).reshape(bp, N_EXPERTS, tp)
        slot_s = _select_call(aff_s.reshape(bs * N_EXPERTS, ts), cap_s).reshape(bs, N_EXPERTS, ts)
        xe_p, ge_p = _gather_call(slot_p, aff_p, h2, row0=0, nseq=bp, t=tp, cap=cap_p, seqs=2)
        xe_s, ge_s = _gather_call(slot_s, aff_s, h2, row0=n_p, nseq=bs, t=ts, cap=cap_s, seqs=1)
        ye_p, ye_s = _ffn_call(xe_p, xe_s, ge_p, ge_s, w_exp_gate[l], w_exp_up[l], w_exp_down[l])

        last = l == depth - 1
        wn = final_norm.reshape(1, d).astype(F32) if last else None
        assert last, "only the final layer applies the output norm in the scatter kernel"
        yp = _scatter_call(jnp.transpose(slot_p, (0, 2, 1)), ye_p, x1, gate2, wn, row0=0, nseq=bp, t=tp, cap=cap_p,
                           seqs=2, mod_row0=0, mod_per_seq=0)
        ysm = _scatter_call(jnp.transpose(slot_s, (0, 2, 1)), ye_s, x1, gate2, wn, row0=n_p, nseq=bs, t=ts, cap=cap_s,
                            seqs=1, mod_row0=1, mod_per_seq=1)

    y_prompt = yp.reshape(bp, tp, d)
    y_sample = ysm.reshape(bs, ts, d)
    return (y_prompt, y_sample, jnp.stack(new_re, axis=1), jnp.stack(new_im, axis=1), jnp.stack(new_ret, axis=1))
```

```python
import functools
import math

import jax
import jax.numpy as jnp
from jax import lax
from jax.experimental import pallas as pl
from jax.experimental.pallas import tpu as pltpu

F32 = jnp.float32
BF16 = jnp.bfloat16

EPS = 1e-6
GRID_W = 64
S5_WIDTH = 1024
S5_GROUP = 16
S5_STATE = 64
RET_HEADS = 8
RET_DK = 128
RET_DV = 256
ROPE_BASE = 10000.0
RET_CHUNK = 256
N_EXPERTS = 16
CAPACITY_FACTOR = 2

LANES = 128
S5_CHUNK = 8
S5_GB = LANES // S5_GROUP
S5_NGB = S5_WIDTH // LANES
S5_SCOLS = 4 * S5_GB * S5_STATE
ROW_TILE = 256
VMEM_LIMIT = 56 * 1024 * 1024


def _params(*sem):
    return pltpu.CompilerParams(dimension_semantics=sem, vmem_limit_bytes=VMEM_LIMIT)


def _sigmoid(x):
    return 1.0 / (1.0 + jnp.exp(-x))


def _gelu_tanh(x):
    return 0.5 * x * (1.0 + jnp.tanh(math.sqrt(2.0 / math.pi) * (x + 0.044715 * (x * x * x))))


def _dot(a, b):
    return jnp.dot(a, b, preferred_element_type=F32)


def _ada_kernel(c_ref, w_ref, b_ref, o_ref):
    c = c_ref[...]
    s = c * _sigmoid(c)
    w = w_ref[...]
    s_hi = s.astype(BF16)
    s_lo = (s - s_hi.astype(F32)).astype(BF16)
    w_hi = w.astype(BF16)
    w_lo = (w - w_hi.astype(F32)).astype(BF16)
    o_ref[...] = _dot(s_hi, w_hi) + (_dot(s_hi, w_lo) + _dot(s_lo, w_hi)) + b_ref[...]


def _ada_call(cvec, w_ada, b_ada):
    r, d = cvec.shape
    n = w_ada.shape[1]
    tn = 1024 if n % 1024 == 0 else 512
    return pl.pallas_call(
        _ada_kernel,
        grid=(n // tn,),
        in_specs=[pl.BlockSpec((r, d), lambda j: (0, 0)),
                  pl.BlockSpec((d, tn), lambda j: (0, j)),
                  pl.BlockSpec((1, tn), lambda j: (0, j))],
        out_specs=pl.BlockSpec((r, tn), lambda j: (0, j)),
        out_shape=jax.ShapeDtypeStruct((r, n), F32),
        compiler_params=_params("arbitrary"),
        name="ada",
    )(cvec, w_ada, b_ada.reshape(1, n))


def _mod_row(i, n_prompt_tiles, tiles_per_sample_seq):
    return jnp.where(i < n_prompt_tiles, 0, 1 + (i - n_prompt_tiles) // tiles_per_sample_seq)


def _proj_kernel(a_ref, w_ref, o_ref, wb_ref):
    @pl.when(pl.program_id(1) == 0)
    def _():
        wb_ref[...] = w_ref[...].astype(BF16)

    o_ref[...] = _dot(a_ref[...], wb_ref[...]).astype(o_ref.dtype)


def _proj_call(a, w, col0):
    m, k = a.shape
    n = w.shape[1] - col0
    tm = next((c for c in (1536, 1024) if m % c == 0), m)
    tn = 1024 if n % 1024 == 0 else 512
    cb0 = col0 // tn
    return pl.pallas_call(
        _proj_kernel,
        grid=(n // tn, m // tm),
        in_specs=[pl.BlockSpec((tm, k), lambda j, i: (i, 0)),
                  pl.BlockSpec((k, tn), lambda j, i: (0, cb0 + j))],
        out_specs=pl.BlockSpec((tm, tn), lambda j, i: (i, j)),
        out_shape=jax.ShapeDtypeStruct((m, n), BF16),
        scratch_shapes=[pltpu.VMEM((k, tn), BF16)],
        compiler_params=_params("arbitrary", "arbitrary"),
        name="in_proj",
    )(a, w)


def _cmul(ar, ai, br, bi):
    return ar * br - ai * bi, ar * bi + ai * br


def _uproj_kernel(xp_ref, xs_ref, sc_ref, sh_ref, nw_ref, w_ref, h_ref, o_ref, wb_ref, r_scr, *, n_prompt_tiles):
    i = pl.program_id(0)

    @pl.when(i == 0)
    def _():
        wb_ref[...] = w_ref[...].astype(BF16)

    def body(x_ref):
        x = x_ref[...]
        y = x * lax.rsqrt(jnp.mean(x * x, axis=-1, keepdims=True) + EPS)
        hb = ((y * nw_ref[...]) * (1.0 + sc_ref[...]) + sh_ref[...]).astype(BF16)
        h_ref[...] = hb
        res = _dot(hb, wb_ref[...])
        for g in range(S5_NGB):
            r_scr[g] = res[:, g * LANES:(g + 1) * LANES]

    @pl.when(i < n_prompt_tiles)
    def _():
        body(xp_ref)

    @pl.when(i >= n_prompt_tiles)
    def _():
        body(xs_ref)

    rows = h_ref.shape[0] // S5_CHUNK
    for g in range(S5_NGB):
        for tau in range(S5_CHUNK):
            c0 = (g * S5_CHUNK + tau) * LANES
            o_ref[:, c0:c0 + LANES] = r_scr[g, pl.ds(tau, rows, stride=S5_CHUNK), :].astype(o_ref.dtype)


def _uproj_call(xp, xs, scale, shift, nw, w, t_sample):
    n_p, d = xp.shape
    m = n_p + xs.shape[0]
    tm = 2 * ROW_TILE
    npt = n_p // tm
    row = functools.partial(_mod_row, n_prompt_tiles=npt, tiles_per_sample_seq=t_sample // tm)
    mod = pl.BlockSpec((None, 1, d), lambda i: (row(i), 0, 0))
    return pl.pallas_call(
        functools.partial(_uproj_kernel, n_prompt_tiles=npt),
        grid=(m // tm,),
        in_specs=[pl.BlockSpec((tm, d), lambda i: (jnp.minimum(i, npt - 1), 0)),
                  pl.BlockSpec((tm, d), lambda i: (jnp.maximum(i - npt, 0), 0)),
                  mod, mod,
                  pl.BlockSpec((1, d), lambda i: (0, 0)),
                  pl.BlockSpec((d, S5_WIDTH), lambda i: (0, 0))],
        out_specs=[pl.BlockSpec((tm, d), lambda i: (i, 0)),
                   pl.BlockSpec((tm // S5_CHUNK, S5_CHUNK * S5_WIDTH), lambda i: (i, 0))],
        out_shape=[jax.ShapeDtypeStruct((m, d), BF16),
                   jax.ShapeDtypeStruct((m // S5_CHUNK, S5_CHUNK * S5_WIDTH), BF16)],
        scratch_shapes=[pltpu.VMEM((d, S5_WIDTH), BF16), pltpu.VMEM((S5_NGB, tm, LANES), F32)],
        compiler_params=_params("arbitrary"),
        name="norm_u_proj",
    )(xp, xs, scale, shift, nw, w)


def _s5_chunk_weights(a_re, a_im, log_dt, b_re, b_im, c_re, c_im, d_skip):
    L, G, P, HG, GB, NGB = S5_CHUNK, S5_WIDTH // S5_GROUP, S5_STATE, S5_GROUP, S5_GB, S5_NGB
    a_re, a_im = a_re.astype(F32), a_im.astype(F32)
    dt = jnp.exp(log_dt.astype(F32))[..., None]
    adt_re, adt_im = a_re * dt, a_im * dt
    mag = jnp.exp(adt_re)
    lam_re, lam_im = mag * jnp.cos(adt_im), mag * jnp.sin(adt_im)
    den = a_re * a_re + a_im * a_im
    q_re = ((lam_re - 1.0) * a_re + lam_im * a_im) / den
    q_im = (lam_im * a_re - (lam_re - 1.0) * a_im) / den
    bb_re, bb_im = _cmul(q_re[..., None], q_im[..., None], b_re.astype(F32), b_im.astype(F32))
    n = jnp.arange(L + 1, dtype=F32)
    pmag = jnp.exp(adt_re[..., None] * n)
    pw_re, pw_im = pmag * jnp.cos(adt_im[..., None] * n), pmag * jnp.sin(adt_im[..., None] * n)

    def ws_exponents(pw, d):
        e = pw[0][..., :L][..., ::-1] if d == 0 else pw[1][..., :L]
        return jnp.transpose(e, (2, 0, 1))[:, :, None]

    def ws_dir(d):
        bt_re, bt_im = jnp.transpose(bb_re[d], (0, 2, 1)), jnp.transpose(bb_im[d], (0, 2, 1))
        return _cmul(ws_exponents(pw_re, d), ws_exponents(pw_im, d), bt_re[None], bt_im[None])

    (wf_re, wf_im), (wb_re, wb_im) = ws_dir(0), ws_dir(1)
    ws = jnp.concatenate([wf_re, wb_re, wf_im, wb_im], axis=-1).reshape(L, NGB, GB * HG, 4 * P)

    def wy_exponents(pw):
        return jnp.stack([pw[0], pw[1][..., ::-1]])

    ct_re = jnp.transpose(c_re.astype(F32), (0, 1, 3, 2))[:, :, :, None]
    ct_im = jnp.transpose(c_im.astype(F32), (0, 1, 3, 2))[:, :, :, None]
    y_re, y_im = _cmul(wy_exponents(pw_re)[..., None], wy_exponents(pw_im)[..., None], ct_re, ct_im)
    wy = jnp.stack([y_re, -y_im]).reshape(2, 2, NGB, GB * P, (L + 1) * HG)

    def lanes(x):
        return jnp.transpose(x.reshape(2, NGB, GB * P), (1, 0, 2)).reshape(NGB, 1, 2 * GB * P)

    a_step = jnp.concatenate([lanes(pw_re[..., L]), lanes(pw_im[..., L])], axis=1)
    return ws.astype(BF16), wy.astype(BF16), a_step, d_skip.astype(F32).reshape(NGB, 1, LANES)


def _s5_expand(xws_ref, zwy_ref, d_ref, wm, wws, wwy):
    L, hg, p = S5_CHUNK, S5_GROUP, S5_STATE
    lg_hg, lg_p, lg_lanes = hg.bit_length() - 1, p.bit_length() - 1, LANES.bit_length() - 1
    gmask = S5_GB - 1
    rblk = 512

    def expand(x, src_of_col, row_group, col_group, o_ref, cblk):
        rows, k = x.shape
        for c0 in range(0, o_ref.shape[1], cblk):
            kk = lax.broadcasted_iota(jnp.int32, (k, cblk), 0)
            cc = lax.broadcasted_iota(jnp.int32, (k, cblk), 1) + c0
            spread = jnp.where(kk == src_of_col(cc), 1.0, 0.0).astype(BF16)
            rr = lax.broadcasted_iota(jnp.int32, (rblk, cblk), 0)
            c2 = lax.broadcasted_iota(jnp.int32, (rblk, cblk), 1) + c0
            keep = row_group(rr) == col_group(c2)
            for r0 in range(0, rows, rblk):
                full = _dot(x[r0:r0 + rblk, :], spread)
                o_ref[r0:r0 + rblk, c0:c0 + cblk] = jnp.where(keep, full, 0.0).astype(o_ref.dtype)

    lane_group = lambda i: jnp.right_shift(i, lg_hg) & gmask
    state_group = lambda i: jnp.right_shift(i, lg_p) & gmask
    chunk_src = lambda c: jnp.right_shift(c, lg_lanes) * hg + (c & (hg - 1))
    state_src = lambda c: jnp.right_shift(c, lg_p + 3) * p + (c & (p - 1))
    xws = xws_ref[...]
    expand(xws.reshape(L * LANES, xws.shape[-1]), state_src, lane_group, state_group, wws, 1024)
    zwy = zwy_ref[...]
    expand(zwy.reshape(S5_SCOLS, zwy.shape[-1]), chunk_src, state_group, lane_group, wwy, wwy.shape[1])

    h = S5_SCOLS // 4
    last, first = wws[(L - 1) * LANES:L * LANES, :], wws[0:LANES, :]
    pf = _dot(last[:, 0:h], wwy[0:h, :]) + _dot(last[:, 2 * h:3 * h], wwy[2 * h:3 * h, :])
    pb = _dot(first[:, h:2 * h], wwy[h:2 * h, :]) + _dot(first[:, 3 * h:4 * h], wwy[3 * h:4 * h, :])
    eye = lax.broadcasted_iota(jnp.int32, (LANES, LANES), 0) == lax.broadcasted_iota(jnp.int32, (LANES, LANES), 1)
    p0 = pf[:, 0:LANES] + pb[:, L * LANES:(L + 1) * LANES] + jnp.where(eye, d_ref[...], 0.0)
    for s in range(L):
        for t in range(L):
            if t > s:
                blk = pf[:, (t - s) * LANES:(t - s + 1) * LANES]
            elif t < s:
                blk = pb[:, (L - (s - t)) * LANES:(L - (s - t) + 1) * LANES]
            else:
                blk = p0
            wm[s * LANES:(s + 1) * LANES, t * LANES:(t + 1) * LANES] = blk.astype(wm.dtype)


def _s5_kernel(u_ref, xws_ref, zwy_ref, d_ref, a_ref, h0_ref, y_ref, hfin_ref,
               wm, wws, wwy, s_scr, hb_scr, y_scr, *, paths):
    nu = S5_CHUNK
    half = S5_SCOLS // 4
    nsl = half // LANES
    cw = 2 * LANES

    _s5_expand(xws_ref, zwy_ref, d_ref, wm, wws, wwy)
    a = a_ref[...]

    for row0, nseq, nchunk, has_h0 in paths:
        r = nseq * nchunk

        for q in range(4):
            sq = _dot(u_ref[row0:row0 + r, :], wws[:, q * half:(q + 1) * half])
            for k in range(nsl):
                for s in range(nseq):
                    s_scr[q * nsl + k, pl.ds(s, nchunk, stride=nseq), :] = (
                        sq[s * nchunk:(s + 1) * nchunk, k * LANES:(k + 1) * LANES])

        if has_h0:
            h0 = h0_ref[...]
            carry0 = tuple(h0[:, j * LANES:(j + 1) * LANES] for j in range(4 * nsl))
        else:
            carry0 = tuple(jnp.zeros((nseq, LANES), F32) for _ in range(4 * nsl))

        def step(i, carry, nseq=nseq, nchunk=nchunk):
            f0 = pl.multiple_of(2 * i * nseq, nseq)
            f1 = pl.multiple_of((2 * i + 1) * nseq, nseq)
            b0 = pl.multiple_of((nchunk - 1 - 2 * i) * nseq, nseq)
            b1 = pl.multiple_of((nchunk - 2 - 2 * i) * nseq, nseq)
            new = list(carry)
            for k in range(nsl):
                for re, im, r0, r1, acol in ((k, 2 * nsl + k, f0, f1, k), (nsl + k, 3 * nsl + k, b0, b1, nsl + k)):
                    a_re = a[0:1, acol * LANES:(acol + 1) * LANES]
                    a_im = a[1:2, acol * LANES:(acol + 1) * LANES]
                    a2_re, a2_im = _cmul(a_re, a_im, a_re, a_im)
                    s0_re, s0_im = s_scr[re, pl.ds(r0, nseq), :], s_scr[im, pl.ds(r0, nseq), :]
                    s1_re, s1_im = s_scr[re, pl.ds(r1, nseq), :], s_scr[im, pl.ds(r1, nseq), :]
                    h_re, h_im = carry[re], carry[im]
                    s_scr[re, pl.ds(r0, nseq), :] = h_re
                    s_scr[im, pl.ds(r0, nseq), :] = h_im
                    m_re, m_im = _cmul(a_re, a_im, h_re, h_im)
                    s_scr[re, pl.ds(r1, nseq), :] = m_re + s0_re
                    s_scr[im, pl.ds(r1, nseq), :] = m_im + s0_im
                    t_re, t_im = _cmul(a_re, a_im, s0_re, s0_im)
                    n_re, n_im = _cmul(a2_re, a2_im, h_re, h_im)
                    new[re] = n_re + (t_re + s1_re)
                    new[im] = n_im + (t_im + s1_im)
            return tuple(new)

        assert nchunk % 2 == 0
        fin = lax.fori_loop(0, nchunk // 2, step, carry0)
        if not has_h0:
            for j in range(4 * nsl):
                hfin_ref[:, j * LANES:(j + 1) * LANES] = fin[j]

        for j in range(4 * nsl):
            for s in range(nseq):
                hb_scr[s * nchunk:(s + 1) * nchunk, j * LANES:(j + 1) * LANES] = (
                    s_scr[j, pl.ds(s, nchunk, stride=nseq), :].astype(BF16))

        for j in range(nu * LANES // cw):
            c0 = j * cw
            y = _dot(u_ref[row0:row0 + r, :], wm[:, c0:c0 + cw])
            for q in range(4):
                off = c0 + (LANES if q % 2 == 0 else 0)
                y = y + _dot(hb_scr[0:r, q * half:(q + 1) * half], wwy[q * half:(q + 1) * half, off:off + cw])
            g = _gelu_tanh(y)
            for t2 in range(cw // LANES):
                tau = j * (cw // LANES) + t2
                y_scr[pl.ds(row0 * nu + tau, r, stride=nu), :] = g[:, t2 * LANES:(t2 + 1) * LANES]

    y_ref[...] = y_scr[...].astype(y_ref.dtype)


def _s5_call(u_rows, ws_cmp, wy_cmp, a_step, d_rows, h0, paths):
    rows = u_rows.shape[0]
    n_tok = rows * S5_CHUNK
    sc, k = S5_SCOLS, S5_CHUNK * LANES
    rmax = max(p[1] * p[2] for p in paths)
    nseq_fin = [p[1] for p in paths if not p[3]][0]
    blk = lambda a, b: pl.BlockSpec((None, a, b), lambda g: (g, 0, 0))
    return pl.pallas_call(
        functools.partial(_s5_kernel, paths=paths),
        grid=(S5_NGB,),
        in_specs=[pl.BlockSpec((rows, k), lambda g: (0, g)),
                  pl.BlockSpec((S5_CHUNK, None, LANES, ws_cmp.shape[-1]), lambda g: (0, g, 0, 0)),
                  pl.BlockSpec((2, 2, None, sc // 4, wy_cmp.shape[-1]), lambda g: (0, 0, g, 0, 0)),
                  blk(1, LANES), blk(2, sc // 2), blk(h0.shape[1], sc)],
        out_specs=[pl.BlockSpec((n_tok, LANES), lambda g: (0, g)), blk(nseq_fin, sc)],
        out_shape=[jax.ShapeDtypeStruct((n_tok, S5_WIDTH), BF16),
                   jax.ShapeDtypeStruct((S5_NGB, nseq_fin, sc), F32)],
        scratch_shapes=[pltpu.VMEM((k, k), BF16), pltpu.VMEM((k, sc), BF16),
                        pltpu.VMEM((sc, (S5_CHUNK + 1) * LANES), BF16),
                        pltpu.VMEM((4 * (sc // 4 // LANES), rmax, LANES), F32), pltpu.VMEM((rmax, sc), BF16),
                        pltpu.VMEM((n_tok, LANES), F32)],
        compiler_params=_params("arbitrary"),
        name="s5",
    )(u_rows, ws_cmp, wy_cmp, d_rows, a_step, h0)


def _state_to_lanes(s_re, s_im):
    b = s_re.shape[0]

    def one(x):
        return jnp.transpose(x.reshape(b, 2, S5_NGB, S5_GB * S5_STATE), (2, 0, 1, 3)).reshape(S5_NGB, b, -1)

    return jnp.concatenate([one(s_re), one(s_im)], axis=-1)


def _lanes_to_state(h):
    b = h.shape[1]
    x = h.reshape(S5_NGB, b, 2, 2, S5_GB, S5_STATE)
    x = jnp.transpose(x, (2, 1, 3, 0, 4, 5)).reshape(2, b, 2, S5_NGB * S5_GB, S5_STATE)
    return x[0], x[1]


def _ret_kernel(lg_ref, q_ref, k_ref, v_ref, g_ref, gn_ref, *rest, t, chunk, group, latent, want_state):
    rest = list(rest)
    if latent:
        cos_ref, sin_ref, s0_ref = rest[:3]
        rest = rest[3:]
    o_ref = rest.pop(0)
    sfin_ref = rest.pop(0) if want_state else None
    dmat_ref = rest.pop(0)
    n = t // chunk

    h = pl.program_id(0)
    lgf = lg_ref[0, h]
    lgb = lg_ref[1, h]

    @pl.when(pl.program_id(1) == 0)
    def _():
        ti = lax.broadcasted_iota(jnp.int32, (chunk, chunk), 0)
        si = lax.broadcasted_iota(jnp.int32, (chunk, chunk), 1)
        diff = (ti - si).astype(F32)
        dmat_ref[...] = jnp.exp(jnp.where(diff >= 0, lgf * diff, -lgb * diff))

    pos = lax.broadcasted_iota(jnp.int32, (chunk, 1), 0).astype(F32)
    q_dec_f = jnp.exp(lgf * (pos + 1.0))
    q_dec_b = jnp.exp(lgb * (chunk - pos))
    k_dec_f = jnp.exp(lgf * (chunk - 1.0 - pos))
    k_dec_b = jnp.exp(lgb * pos)
    step_f = jnp.exp(lgf * chunk)
    step_b = jnp.exp(lgb * chunk)
    tn = (((0,), (0,)), ((), ()))
    if latent:
        lane = lax.broadcasted_iota(jnp.int32, (chunk, RET_DK), 1)
        first = (lane % (RET_DK // 2)) < (RET_DK // 4)

        def rope(x, rows):
            swapped = jnp.where(first, pltpu.roll(x, RET_DK - RET_DK // 4, 1), pltpu.roll(x, RET_DK // 4, 1))
            return x * cos_ref[rows, :] + swapped * sin_ref[rows, :]

    for j in range(group):
        qs, kf, kr, vs, outs = [], [], [], [], []
        for c in range(n):
            rows = slice(j * t + c * chunk, j * t + (c + 1) * chunk)
            q = q_ref[rows, :].astype(F32) * (RET_DK ** -0.5)
            k = k_ref[rows, :].astype(F32)
            if latent:
                q = rope(q, slice(c * chunk, (c + 1) * chunk))
                k = rope(k, slice(c * chunk, (c + 1) * chunk))
            qb = q.astype(BF16)
            v = v_ref[rows, :]
            scores = lax.dot_general(qb, k.astype(BF16), (((1,), (1,)), ((), ())), preferred_element_type=F32)
            outs.append(_dot((scores * dmat_ref[...]).astype(BF16), v))
            qs.append(qb)
            vs.append(v)
            kf.append((k * k_dec_f).astype(BF16))
            kr.append((k * k_dec_b).astype(BF16))

        state = s0_ref[j, 0] if latent else None
        for c in range(n):
            if state is not None:
                outs[c] = outs[c] + _dot(qs[c], state.astype(BF16)) * q_dec_f
            if c < n - 1 or want_state:
                upd = lax.dot_general(kf[c], vs[c], tn, preferred_element_type=F32)
                state = upd if state is None else step_f * state + upd
        if want_state:
            sfin_ref[j, 0] = state

        state = s0_ref[j, 1] if latent else None
        for c in reversed(range(n)):
            if state is not None:
                outs[c] = outs[c] + _dot(qs[c], state.astype(BF16)) * q_dec_b
            if c > 0 or want_state:
                upd = lax.dot_general(kr[c], vs[c], tn, preferred_element_type=F32)
                state = upd if state is None else step_b * state + upd
        if want_state:
            sfin_ref[j, 1] = state

        for c in range(n):
            rows = slice(j * t + c * chunk, j * t + (c + 1) * chunk)
            o = outs[c]
            mu = jnp.mean(o, axis=-1, keepdims=True)
            d = o - mu
            var = jnp.mean(d * d, axis=-1, keepdims=True)
            on = d * lax.rsqrt(var + EPS) * gn_ref[...]
            g = g_ref[rows, :].astype(F32)
            o_ref[rows, :] = (g * _sigmoid(g) * on).astype(o_ref.dtype)


def _ret_call(proj, log_gamma, gn_w, rope_tabs, s0, *, row0, nseq, t, chunk, group, latent, want_state):
    h, dk, dv = RET_HEADS, RET_DK, RET_DV
    gt = group * t
    rb0 = row0 // gt
    q0 = 0
    k0 = q0 + h
    v0 = (2 * h * dk) // dv
    g0 = v0 + h
    in_specs = [pl.BlockSpec(memory_space=pltpu.SMEM),
                pl.BlockSpec((gt, dk), lambda hh, b: (rb0 + b, q0 + hh)),
                pl.BlockSpec((gt, dk), lambda hh, b: (rb0 + b, k0 + hh)),
                pl.BlockSpec((gt, dv), lambda hh, b: (rb0 + b, v0 + hh)),
                pl.BlockSpec((gt, dv), lambda hh, b: (rb0 + b, g0 + hh)),
                pl.BlockSpec((1, dv), lambda hh, b: (0, hh))]
    args = [log_gamma, proj, proj, proj, proj, gn_w]
    if latent:
        in_specs += [pl.BlockSpec((t, dk), lambda hh, b: (0, 0)),
                     pl.BlockSpec((t, dk), lambda hh, b: (0, 0)),
                     pl.BlockSpec((group, 2, None, dk, dv), lambda hh, b: (b, 0, hh, 0, 0))]
        args += [rope_tabs[0], rope_tabs[1], s0]
    out_specs = [pl.BlockSpec((gt, dv), lambda hh, b: (b, hh))]
    out_shape = [jax.ShapeDtypeStruct((nseq * t, h * dv), BF16)]
    if want_state:
        out_specs.append(pl.BlockSpec((group, 2, None, dk, dv), lambda hh, b: (b, 0, hh, 0, 0)))
        out_shape.append(jax.ShapeDtypeStruct((nseq, 2, h, dk, dv), F32))
    return pl.pallas_call(
        functools.partial(_ret_kernel, t=t, chunk=chunk, group=group, latent=latent, want_state=want_state),
        grid=(h, nseq // group),
        in_specs=in_specs,
        out_specs=out_specs,
        out_shape=out_shape,
        scratch_shapes=[pltpu.VMEM((chunk, chunk), F32)],
        compiler_params=_params("arbitrary", "arbitrary"),
        name="retention_latent" if latent else "retention_ctx",
    )(*args)


def _grid_rope_tables(t):
    quarter = RET_DK // 4
    freqs = ROPE_BASE ** (-jnp.arange(quarter, dtype=F32) / quarter)
    pos = jnp.arange(t)
    row = (pos // GRID_W).astype(F32)[:, None] * freqs[None, :]
    col = (pos % GRID_W).astype(F32)[:, None] * freqs[None, :]
    cos = jnp.concatenate([jnp.cos(row), jnp.cos(row), jnp.cos(col), jnp.cos(col)], axis=-1)
    sin = jnp.concatenate([-jnp.sin(row), jnp.sin(row), -jnp.sin(col), jnp.sin(col)], axis=-1)
    return cos, sin


def _merge_kernel(yg_ref, op_ref, os_ref, ga_ref, gb_ref, wga_ref, wgb_ref, wr_ref, out_ref, wga_b, wgb_b, wr_b,
                  *, n_prompt_tiles):
    i = pl.program_id(1)

    @pl.when(i == 0)
    def _():
        wga_b[...] = wga_ref[...].astype(BF16)
        wgb_b[...] = wgb_ref[...].astype(BF16)
        wr_b[...] = wr_ref[...].astype(BF16)

    def body(o_ref):
        yg = yg_ref[...]
        za = _dot(yg, wga_b[...])
        zb = _dot(yg, wgb_b[...])
        ob = _dot(o_ref[...], wr_b[...])
        out_a = za * _sigmoid(zb)
        ga = ga_ref[...].astype(F32)
        gb = gb_ref[...].astype(F32)
        out_ref[...] = (_sigmoid(ga) * out_a + _sigmoid(gb) * ob).astype(out_ref.dtype)

    @pl.when(i < n_prompt_tiles)
    def _():
        body(op_ref)

    @pl.when(i >= n_prompt_tiles)
    def _():
        body(os_ref)


def _merge_call(yg, o_p, o_s, proj, w_glu, w_ret_out):
    m, ks = yg.shape
    kr = o_p.shape[1]
    d = w_ret_out.shape[1]
    tm = 1024 if m % 1024 == 0 else m
    tn = 512 if d % 512 == 0 else d
    ga0 = (proj.shape[1] - 2 * d) // tn
    gb0 = (proj.shape[1] - d) // tn
    nb = d // tn
    npt = o_p.shape[0] // tm
    return pl.pallas_call(
        functools.partial(_merge_kernel, n_prompt_tiles=npt),
        grid=(nb, m // tm),
        in_specs=[pl.BlockSpec((tm, ks), lambda j, i: (i, 0)),
                  pl.BlockSpec((tm, kr), lambda j, i: (jnp.minimum(i, npt - 1), 0)),
                  pl.BlockSpec((tm, kr), lambda j, i: (jnp.maximum(i - npt, 0), 0)),
                  pl.BlockSpec((tm, tn), lambda j, i: (i, ga0 + j)),
                  pl.BlockSpec((tm, tn), lambda j, i: (i, gb0 + j)),
                  pl.BlockSpec((ks, tn), lambda j, i: (0, j)),
                  pl.BlockSpec((ks, tn), lambda j, i: (0, nb + j)),
                  pl.BlockSpec((kr, tn), lambda j, i: (0, j))],
        out_specs=pl.BlockSpec((tm, tn), lambda j, i: (i, j)),
        out_shape=jax.ShapeDtypeStruct((m, d), BF16),
        scratch_shapes=[pltpu.VMEM((ks, tn), BF16), pltpu.VMEM((ks, tn), BF16), pltpu.VMEM((kr, tn), BF16)],
        compiler_params=_params("arbitrary", "arbitrary"),
        name="merge",
    )(yg, o_p, o_s, proj, proj, w_glu, w_glu, w_ret_out)


def _outproj_kernel(mg_hbm, xp_ref, xs_ref, w_ref, g1_ref, sc_ref, sh_ref, n2_ref, wr_ref,
                    x1_ref, h2_ref, aff_ref, mg_buf, mg_sem, *, n_prompt_tiles, n_experts, n_steps):
    i = pl.program_id(0)
    slots, tm = mg_buf.shape[0], mg_buf.shape[1]

    def mg_copy(step):
        slot = lax.rem(step, slots)
        rows = pl.ds(pl.multiple_of(step * tm, tm), tm)
        return pltpu.make_async_copy(mg_hbm.at[rows, :], mg_buf.at[slot], mg_sem.at[slot])

    @pl.when(i == 0)
    def _():
        for s in range(slots - 1):
            mg_copy(s).start()

    @pl.when(i + (slots - 1) < n_steps)
    def _():
        mg_copy(i + (slots - 1)).start()

    mg_copy(i).wait()
    upd = g1_ref[...] * _dot(mg_buf[lax.rem(i, slots)], w_ref[...])

    def body(x_ref):
        x1 = x_ref[...] + upd
        x1_ref[...] = x1
        y = x1 * lax.rsqrt(jnp.mean(x1 * x1, axis=-1, keepdims=True) + EPS)
        h2 = (y * n2_ref[...]) * (1.0 + sc_ref[...]) + sh_ref[...]
        h2_ref[...] = h2.astype(h2_ref.dtype)
        wr = wr_ref[...]
        w_hi = wr.astype(BF16)
        w_lo = (wr - w_hi.astype(F32)).astype(BF16)
        h_hi = h2.astype(BF16)
        h_lo = (h2 - h_hi.astype(F32)).astype(BF16)
        hw = _dot(h_hi, jnp.concatenate([w_hi, w_lo], axis=1))
        logits = hw[:, :LANES] + (hw[:, LANES:] + _dot(h_lo, w_hi))
        lane = lax.broadcasted_iota(jnp.int32, logits.shape, 1)
        logits = jnp.where(lane < n_experts, logits, -jnp.inf)
        e = jnp.exp(logits - jnp.max(logits, axis=-1, keepdims=True))
        aff_ref[...] = (e / jnp.sum(e, axis=-1, keepdims=True)).T

    @pl.when(i < n_prompt_tiles)
    def _():
        body(xp_ref)

    @pl.when(i >= n_prompt_tiles)
    def _():
        body(xs_ref)


def _outproj_call(merged, xp, xs, w_out_b, gate1, scale2, shift2, norm2, w_router_pad, t_sample):
    m, d = merged.shape
    n_p = xp.shape[0]
    tm = 2 * ROW_TILE
    npt = n_p // tm
    tps = t_sample // tm
    row = functools.partial(_mod_row, n_prompt_tiles=npt, tiles_per_sample_seq=tps)
    mod = pl.BlockSpec((None, 1, d), lambda i: (row(i), 0, 0))
    return pl.pallas_call(
        functools.partial(_outproj_kernel, n_prompt_tiles=npt, n_experts=N_EXPERTS, n_steps=m // tm),
        grid=(m // tm,),
        in_specs=[pl.BlockSpec(memory_space=pl.ANY),
                  pl.BlockSpec((tm, d), lambda i: (jnp.minimum(i, npt - 1), 0)),
                  pl.BlockSpec((tm, d), lambda i: (jnp.maximum(i - npt, 0), 0)),
                  pl.BlockSpec((d, d), lambda i: (0, 0)),
                  mod, mod, mod,
                  pl.BlockSpec((1, d), lambda i: (0, 0)),
                  pl.BlockSpec((d, LANES), lambda i: (0, 0))],
        out_specs=[pl.BlockSpec((tm, d), lambda i: (i, 0)),
                   pl.BlockSpec((tm, d), lambda i: (i, 0)),
                   pl.BlockSpec((LANES, tm), lambda i: (0, i))],
        out_shape=[jax.ShapeDtypeStruct((m, d), F32),
                   jax.ShapeDtypeStruct((m, d), BF16),
                   jax.ShapeDtypeStruct((LANES, m), F32)],
        scratch_shapes=[pltpu.VMEM((3, tm, d), BF16), pltpu.SemaphoreType.DMA((3,))],
        compiler_params=_params("arbitrary"),
        name="out_proj_router",
    )(merged, xp, xs, w_out_b, gate1, scale2, shift2, norm2, w_router_pad)


def _select_kernel(a_ref, slot_ref, *, cap):
    a = a_ref[...]
    r, t = a.shape

    def as_float(bits):
        return pltpu.bitcast(bits, F32)

    def bisect(_, carry):
        lo, hi = carry
        mid = lo + jnp.right_shift(hi - lo + 1, 1)
        cnt = jnp.sum(jnp.where(a >= as_float(mid), 1.0, 0.0), axis=-1, keepdims=True)
        ok = cnt >= cap
        return jnp.where(ok, mid, lo), jnp.where(ok, hi, mid - 1)

    lo0 = jnp.zeros((r, 1), jnp.int32)
    hi0 = jnp.full((r, 1), 0x3F800000, jnp.int32)
    thr, _ = lax.fori_loop(0, 31, bisect, (lo0, hi0))

    gt = jnp.where(a >= as_float(thr + 1), 1.0, 0.0)
    eq = jnp.where(a >= as_float(thr), 1.0, 0.0) - gt
    need = cap - jnp.sum(gt, axis=-1, keepdims=True)
    before = lax.broadcasted_iota(jnp.int32, (t, t), 0) < lax.broadcasted_iota(jnp.int32, (t, t), 1)
    tri = jnp.where(before, 1.0, 0.0).astype(BF16)
    eq_rank = _dot(eq.astype(BF16), tri)
    sel = gt + eq * jnp.where(eq_rank < need, 1.0, 0.0)
    pos = _dot(sel.astype(BF16), tri)
    slot_ref[...] = jnp.where(sel > 0.5, pos, -1.0).astype(jnp.int32)


def _select_call(aff_t, cap):
    r, t = aff_t.shape
    return pl.pallas_call(
        functools.partial(_select_kernel, cap=cap),
        grid=(1,),
        in_specs=[pl.BlockSpec((r, t), lambda i: (0, 0))],
        out_specs=pl.BlockSpec((r, t), lambda i: (0, 0)),
        out_shape=jax.ShapeDtypeStruct((r, t), jnp.int32),
        compiler_params=_params("arbitrary"),
        name="select",
    )(aff_t)


def _gather_kernel(slot_ref, aff_ref, h_ref, xs_ref, gate_ref, *, cap, group, seqs):
    _, e_total, t = slot_ref.shape
    ci = lax.broadcasted_iota(jnp.int32, (cap, t), 0)
    for s in range(seqs):
        h = h_ref[s * t:(s + 1) * t, :]
        rows = slice(s * cap, (s + 1) * cap)
        for e0 in range(0, e_total, group):
            hots = []
            for e in range(e0, e0 + group):
                hit = ci == slot_ref[s, e:e + 1, :]
                hots.append(jnp.where(hit, 1.0, 0.0).astype(BF16))
                gate_ref[e, rows, :] = jnp.sum(jnp.where(hit, aff_ref[s, e:e + 1, :], 0.0), axis=-1, keepdims=True)
            onehot = hots[0] if group == 1 else jnp.concatenate(hots, axis=0)
            xs = _dot(onehot, h).astype(xs_ref.dtype)
            xs_ref[e0:e0 + group, rows, :] = xs.reshape(group, cap, xs.shape[-1])


def _gather_call(slot_t, aff_t, h2, *, row0, nseq, t, cap, seqs):
    e = slot_t.shape[1]
    d = h2.shape[1]
    rb0 = row0 // (seqs * t)
    group = max(1, min(e, 512 // cap))
    return pl.pallas_call(
        functools.partial(_gather_kernel, cap=cap, group=group, seqs=seqs),
        grid=(nseq // seqs,),
        in_specs=[pl.BlockSpec((seqs, e, t), lambda b: (b, 0, 0)),
                  pl.BlockSpec((seqs, e, t), lambda b: (b, 0, 0)),
                  pl.BlockSpec((seqs * t, d), lambda b: (rb0 + b, 0))],
        out_specs=[pl.BlockSpec((e, seqs * cap, d), lambda b: (0, b, 0)),
                   pl.BlockSpec((e, seqs * cap, 1), lambda b: (0, b, 0))],
        out_shape=[jax.ShapeDtypeStruct((e, nseq * cap, d), BF16),
                   jax.ShapeDtypeStruct((e, nseq * cap, 1), F32)],
        compiler_params=_params("arbitrary"),
        name="gather",
    )(slot_t, aff_t, h2)


def _ffn_kernel(xp_ref, xs_ref, gp_ref, gs_ref, wg_ref, wu_ref, wd_ref, yp_ref, ys_ref,
                accp, accs, wgb, wub, wdb, *, chunk):
    f = pl.program_id(1)

    @pl.when(f == 0)
    def _():
        accp[...] = jnp.zeros_like(accp)
        accs[...] = jnp.zeros_like(accs)

    wgb[...] = wg_ref[...].astype(BF16)
    wub[...] = wu_ref[...].astype(BF16)
    wdb[...] = wd_ref[...].astype(BF16)

    def part(x_ref, acc):
        m = x_ref.shape[0]
        mc = min(chunk, m)
        for m0 in range(0, m, mc):
            x = x_ref[m0:m0 + mc, :]
            hg = _dot(x, wgb[...])
            hu = _dot(x, wub[...])
            hid = (hg * _sigmoid(hg) * hu).astype(BF16)
            acc[m0:m0 + mc, :] += _dot(hid, wdb[...])

    part(xp_ref, accp)
    part(xs_ref, accs)

    @pl.when(f == pl.num_programs(1) - 1)
    def _():
        yp_ref[...] = (accp[...] * gp_ref[...]).astype(yp_ref.dtype)
        ys_ref[...] = (accs[...] * gs_ref[...]).astype(ys_ref.dtype)


def _ffn_call(xs_p, xs_s, gate_p, gate_s, w_gate, w_up, w_down):
    e, mp, d = xs_p.shape
    ms = xs_s.shape[1]
    ff = w_gate.shape[2]
    tf = 256 if ff % 256 == 0 else ff
    tok = lambda m, w: pl.BlockSpec((None, m, w), lambda ee, f: (ee, 0, 0))
    return pl.pallas_call(
        functools.partial(_ffn_kernel, chunk=512),
        grid=(e, ff // tf),
        in_specs=[tok(mp, d), tok(ms, d), tok(mp, 1), tok(ms, 1),
                  pl.BlockSpec((None, d, tf), lambda ee, f: (ee, 0, f)),
                  pl.BlockSpec((None, d, tf), lambda ee, f: (ee, 0, f)),
                  pl.BlockSpec((None, tf, d), lambda ee, f: (ee, f, 0))],
        out_specs=[pl.BlockSpec((None, mp, d), lambda ee, f: (ee, 0, 0), pipeline_mode=pl.Buffered(1)),
                   pl.BlockSpec((None, ms, d), lambda ee, f: (ee, 0, 0), pipeline_mode=pl.Buffered(1))],
        out_shape=[jax.ShapeDtypeStruct((e, mp, d), BF16), jax.ShapeDtypeStruct((e, ms, d), BF16)],
        scratch_shapes=[pltpu.VMEM((mp, d), F32), pltpu.VMEM((ms, d), F32),
                        pltpu.VMEM((d, tf), BF16), pltpu.VMEM((d, tf), BF16), pltpu.VMEM((tf, d), BF16)],
        compiler_params=_params("arbitrary", "arbitrary"),
        name="expert_ffn",
    )(xs_p, xs_s, gate_p, gate_s, w_gate, w_up, w_down)


def _scatter_kernel(slot_ref, y_ref, x1_ref, g2_ref, wn_ref, o_ref, *, cap, seqs):
    e_total = y_ref.shape[0]
    tm = slot_ref.shape[1]
    lane = lax.broadcasted_iota(jnp.int32, (tm, LANES), 1)
    per_block = max(1, LANES // cap)
    for q in range(seqs):
        slot = slot_ref[q]
        blocks = []
        for b0 in range(0, e_total, per_block):
            acc = jnp.zeros((tm, LANES), F32)
            for j in range(per_block):
                s = slot[:, b0 + j:b0 + j + 1]
                key = jnp.where(s >= 0, s + j * cap, -1)
                acc = acc + jnp.where(lane == key, 1.0, 0.0)
            blocks.append(acc.astype(BF16))
        onehot = jnp.concatenate(blocks, axis=1)
        y = y_ref[:, q * cap:(q + 1) * cap, :].reshape(e_total * cap, y_ref.shape[-1])
        moe = _dot(onehot, y)
        rows = slice(q * tm, (q + 1) * tm)
        x2 = x1_ref[rows, :] + g2_ref[...] * moe
        o_ref[rows, :] = x2 * lax.rsqrt(jnp.mean(x2 * x2, axis=-1, keepdims=True) + EPS) * wn_ref[...]


def _scatter_call(slot, y, x1, gate2, final_norm, *, row0, nseq, t, cap, seqs, mod_row0, mod_per_seq):
    e, _, d = y.shape
    tm = min(2 * ROW_TILE, t)
    nt = t // tm
    assert cap == LANES or LANES % cap == 0
    assert seqs == 1 or (nt == 1 and mod_per_seq == 0)
    rb0 = row0 // (seqs * tm)
    return pl.pallas_call(
        functools.partial(_scatter_kernel, cap=cap, seqs=seqs),
        grid=(nseq // seqs, nt),
        in_specs=[pl.BlockSpec((seqs, tm, e), lambda b, i: (b, i, 0)),
                  pl.BlockSpec((e, seqs * cap, d), lambda b, i: (0, b, 0)),
                  pl.BlockSpec((seqs * tm, d), lambda b, i: (rb0 + b * nt + i, 0)),
                  pl.BlockSpec((None, 1, d), lambda b, i: (mod_row0 + b * mod_per_seq, 0, 0)),
                  pl.BlockSpec((1, d), lambda b, i: (0, 0))],
        out_specs=pl.BlockSpec((seqs * tm, d), lambda b, i: (b * nt + i, 0)),
        out_shape=jax.ShapeDtypeStruct((nseq * t, d), F32),
        compiler_params=_params("arbitrary", "arbitrary"),
        name="scatter_final",
    )(slot, y, x1, gate2, final_norm)


def kernel(x_prompt, x_sample, state_s5_re, state_s5_im, state_ret, c, c_ctx, final_norm, w_ada, b_ada, norm1, norm2, w_in, s5_a_re, s5_a_im, s5_log_dt, s5_b_re, s5_b_im, s5_c_re, s5_c_im, s5_d, w_s5_glu, ret_decay_logit, ret_gn_w, w_ret_out, w_out, w_router, w_exp_gate, w_exp_up, w_exp_down):
    bp, tp, d = x_prompt.shape
    bs, ts, _ = x_sample.shape
    depth = w_ada.shape[0]
    n_p, n_s = bp * tp, bs * ts
    xp = x_prompt.reshape(n_p, d)
    xs = x_sample.reshape(n_s, d)

    mod_rows = 16
    cvec = jnp.concatenate([c_ctx[None].astype(F32), c.astype(F32), jnp.zeros((mod_rows - 1 - bs, d), F32)], axis=0)
    rope_tabs = _grid_rope_tables(ts)
    cap_p = CAPACITY_FACTOR * tp // N_EXPERTS
    cap_s = CAPACITY_FACTOR * ts // N_EXPERTS

    new_re, new_im, new_ret = [], [], []
    for l in range(depth):
        mods = _ada_call(cvec, w_ada[l], b_ada[l])
        shift1, scale1, gate1, shift2, scale2, gate2 = [m.reshape(mod_rows, 1, d) for m in jnp.split(mods, 6, axis=-1)]

        h, u_rows = _uproj_call(xp, xs, scale1, shift1, norm1[l].reshape(1, d), w_in[l], ts)
        proj = _proj_call(h, w_in[l], S5_WIDTH)

        ws_cmp, wy_cmp, a_step, d_rows = _s5_chunk_weights(
            s5_a_re[l], s5_a_im[l], s5_log_dt[l], s5_b_re[l], s5_b_im[l], s5_c_re[l], s5_c_im[l], s5_d[l])
        h0_s = _state_to_lanes(state_s5_re[:, l].astype(F32), state_s5_im[:, l].astype(F32))
        s5_paths = ((0, bp, tp // S5_CHUNK, False), (n_p // S5_CHUNK, bs, ts // S5_CHUNK, True))
        yg, hfin_p = _s5_call(u_rows, ws_cmp, wy_cmp, a_step, d_rows, h0_s, s5_paths)
        s5_re, s5_im = _lanes_to_state(hfin_p)
        new_re.append(s5_re)
        new_im.append(s5_im)

        log_gamma = jax.nn.log_sigmoid(ret_decay_logit[l].astype(F32))
        gn_w = ret_gn_w[l].reshape(1, -1).astype(F32)
        o_p, sfin = _ret_call(proj, log_gamma, gn_w, None, None, row0=0, nseq=bp, t=tp, chunk=RET_CHUNK, group=8, latent=False,
                              want_state=True)
        o_s, = _ret_call(proj, log_gamma, gn_w, rope_tabs, state_ret[:, l].astype(F32), row0=n_p, nseq=bs, t=ts,
                         chunk=RET_CHUNK, group=4, latent=True, want_state=False)
        new_ret.append(sfin)
        merged = _merge_call(yg, o_p, o_s, proj, w_s5_glu[l], w_ret_out[l])
        w_router_pad = jnp.pad(w_router[l].astype(F32), ((0, 0), (0, LANES - N_EXPERTS)))
        x1, h2, aff = _outproj_call(merged, xp, xs, w_out[l].astype(BF16), gate1, scale2, shift2,
                                    norm2[l].reshape(1, d), w_router_pad, ts)

        aff = aff[:N_EXPERTS]
        aff_p = jnp.transpose(aff[:, :n_p].reshape(N_EXPERTS, bp, tp), (1, 0, 2))
        aff_s = jnp.transpose(aff[:, n_p:].reshape(N_EXPERTS, bs, ts), (1, 0, 2))
        slot_p = _select_call(aff_p.reshape(bp * N_EXPERTS, tp), cap_p).reshape(bp, N_EXPERTS, tp)
        slot_s = _select_call(aff_s.reshape(bs * N_EXPERTS, ts), cap_s).reshape(bs, N_EXPERTS, ts)
        xe_p, ge_p = _gather_call(slot_p, aff_p, h2, row0=0, nseq=bp, t=tp, cap=cap_p, seqs=2)
        xe_s, ge_s = _gather_call(slot_s, aff_s, h2, row0=n_p, nseq=bs, t=ts, cap=cap_s, seqs=1)
        ye_p, ye_s = _ffn_call(xe_p, xe_s, ge_p, ge_s, w_exp_gate[l], w_exp_up[l], w_exp_down[l])

        last = l == depth - 1
        wn = final_norm.reshape(1, d).astype(F32) if last else None
        assert last, "only the final layer applies the output norm in the scatter kernel"
        yp = _scatter_call(jnp.transpose(slot_p, (0, 2, 1)), ye_p, x1, gate2, wn, row0=0, nseq=bp, t=tp, cap=cap_p,
                           seqs=2, mod_row0=0, mod_per_seq=0)
        ysm = _scatter_call(jnp.transpose(slot_s, (0, 2, 1)), ye_s, x1, gate2, wn, row0=n_p, nseq=bs, t=ts, cap=cap_s,
                            seqs=1, mod_row0=1, mod_per_seq=1)

    y_prompt = yp.reshape(bp, tp, d)
    y_sample = ysm.reshape(bs, ts, d)
    return (y_prompt, y_sample, jnp.stack(new_re, axis=1), jnp.stack(new_im, axis=1), jnp.stack(new_ret, axis=1))
```
